```python
import math
import jax
import jax.numpy as jnp
from jax import lax
import numpy as np

D_MODEL = 1024
BATCH = 4
SEQ = 4096
DEPTH = 2

CHUNK = 64
Q_BLOCK = 128
NORM_EPS = 1e-6
NEG_INF = -1e30

N_EVEN = (DEPTH + 1) // 2
N_ODD = DEPTH // 2

RWKV_HEAD_DIM = 64
RWKV_WIDTH = D_MODEL // 2
RWKV_HEADS = RWKV_WIDTH // RWKV_HEAD_DIM
DECAY_LORA = 64
ICLR_LORA = 64
GATE_LORA = 160
RWKV_GN_EPS = 1e-5 * RWKV_HEAD_DIM
RWKV_SPLITS = (RWKV_WIDTH, 2 * RWKV_WIDTH, 3 * RWKV_WIDTH,
               3 * RWKV_WIDTH + DECAY_LORA, 3 * RWKV_WIDTH + DECAY_LORA + ICLR_LORA)
RWKV_COLS = 3 * RWKV_WIDTH + DECAY_LORA + ICLR_LORA + GATE_LORA

S5_WIDTH = D_MODEL - RWKV_WIDTH
S5_GROUP = 16
S5_GROUPS = S5_WIDTH // S5_GROUP
S5_STATE = 64
S5_DT_MIN = 1e-3
S5_DT_MAX = 1e-1

IN_COLS = RWKV_COLS + S5_WIDTH
MIX_WIDTH = RWKV_WIDTH + S5_WIDTH

DIFF_HEAD_DIM = 64
DIFF_V_DIM = 2 * DIFF_HEAD_DIM
DIFF_HEADS = D_MODEL // DIFF_V_DIM
DIFF_WIDTH = DIFF_HEADS * DIFF_V_DIM

MEM_LEN = 256
MEM_HEADS = 4
MEM_HEAD_DIM = D_MODEL // MEM_HEADS

D_FF = 2816
N_EXPERTS = 8
TOP_K = 2
D_FF_EXPERT = 3584

kernel_name = 'hybrid_rwkv7_s5_diffattn_moe_encoder'

F32 = jnp.float32


def rms_norm(x, gain):
    xf = x.astype(F32)
    y = xf * lax.rsqrt(jnp.mean(xf * xf, axis=-1, keepdims=True) + NORM_EPS)
    return (y * gain.astype(F32)).astype(x.dtype)


def token_shift(z):
    return jnp.pad(z, ((0, 0), (1, 0), (0, 0)))[:, :-1]


def alibi_slopes(n_heads):
    return 2.0 ** (-8.0 * jnp.arange(1, n_heads + 1, dtype=F32) / n_heads)


def rwkv7_recurrence(r, decay, k, v, kk, a):
    b, _, h, n = r.shape

    def step(state, inp):
        r_t, w_t, k_t, v_t, kk_t, a_t = inp
        s_kk = jnp.einsum('bhvk,bhk->bhv', state, kk_t)
        state = (state * w_t[:, :, None, :]
                 - s_kk[..., None] * (kk_t * a_t)[:, :, None, :]
                 + v_t[..., None] * k_t[:, :, None, :])
        return state, jnp.einsum('bhvk,bhk->bhv', state, r_t)

    xs = tuple(jnp.moveaxis(t, 1, 0) for t in (r, decay, k, v, kk, a))
    _, out = lax.scan(step, jnp.zeros((b, h, n, n), F32), xs)
    return jnp.moveaxis(out, 0, 1)


def rwkv7_time_mix(zr, mu, w0, w_up, a0, a_up, g_up, k_k, k_a, r_k, ln_w, ln_b):
    b, t, _ = zr.shape
    zr = zr.astype(F32)
    zr = zr + (token_shift(zr) - zr) * mu.astype(F32)
    r, k, v, wd, ad, gd = jnp.split(zr, RWKV_SPLITS, axis=-1)
    heads = lambda u: u.reshape(b, t, RWKV_HEADS, RWKV_HEAD_DIM)
    w_log = -jax.nn.softplus(-(w0.astype(F32) + jnp.tanh(wd) @ w_up.astype(F32))) - 0.5
    decay = jnp.exp(-jnp.exp(w_log))
    a = jax.nn.sigmoid(a0.astype(F32) + ad @ a_up.astype(F32))
    g = jax.nn.sigmoid(gd) @ g_up.astype(F32)
    kk = heads(k * k_k.astype(F32))
    kk = kk * lax.rsqrt(jnp.maximum(jnp.sum(kk * kk, axis=-1, keepdims=True), 1e-24))
    k = k * (1.0 + (a - 1.0) * k_a.astype(F32))
    r_h, k_h, v_h = heads(r), heads(k), heads(v)
    o = rwkv7_recurrence(r_h, heads(decay), k_h, v_h, kk, heads(a))
    mean = jnp.mean(o, axis=-1, keepdims=True)
    var = jnp.mean(jnp.square(o - mean), axis=-1, keepdims=True)
    o = ((o - mean) * lax.rsqrt(var + RWKV_GN_EPS)).reshape(b, t, RWKV_WIDTH)
    o = o * ln_w.astype(F32) + ln_b.astype(F32)
    bonus = jnp.sum(r_h * k_h * r_k.astype(F32), axis=-1, keepdims=True) * v_h
    return (o + bonus.reshape(b, t, RWKV_WIDTH)) * g


def s5_ssm(u, lam_re, lam_im, log_dt, b_re, b_im, c_re, c_im, d_skip, w_glu, b_glu):
    b, t, _ = u.shape
    ug = u.astype(F32).reshape(b, t, S5_GROUPS, S5_GROUP)
    lam = lax.complex(lam_re.astype(F32), lam_im.astype(F32))
    dt = jnp.exp(log_dt.astype(F32))[:, None]
    lam_bar = jnp.exp(lam * dt)
    b_bar = ((lam_bar - 1.0) / lam)[..., None] * lax.complex(b_re.astype(F32), b_im.astype(F32))
    bu = jnp.einsum('gpi,btgi->btgp', b_bar, ug.astype(jnp.complex64))

    def combine(left, right):
        a_l, x_l = left
        a_r, x_r = right
        return a_r * a_l, a_r * x_l + x_r

    _, states = lax.associative_scan(combine, (jnp.broadcast_to(lam_bar, bu.shape), bu), axis=1)
    c_mat = lax.complex(c_re.astype(F32), c_im.astype(F32))
    y = jnp.real(jnp.einsum('gip,btgp->btgi', c_mat, states))
    y = y + d_skip.astype(F32).reshape(S5_GROUPS, S5_GROUP) * ug
    y = jax.nn.gelu(y.reshape(b, t, S5_WIDTH))
    return y * jax.nn.sigmoid(y @ w_glu.astype(F32) + b_glu.astype(F32))


def diff_attention(h, w_qkv, q_gain, k_gain, lam_q1, lam_k1, lam_q2, lam_k2, sub_gain, w_o, lambda_init):
    b, t, _ = h.shape
    q, k, v = jnp.split(h @ w_qkv, 3, axis=-1)
    q = rms_norm(q.reshape(b, t, DIFF_HEADS, 2, DIFF_HEAD_DIM), q_gain).astype(F32)
    k = rms_norm(k.reshape(b, t, DIFF_HEADS, 2, DIFF_HEAD_DIM), k_gain).astype(F32)
    v = v.reshape(b, t, DIFF_HEADS, DIFF_V_DIM).astype(F32)
    lam = (jnp.exp(jnp.sum(lam_q1.astype(F32) * lam_k1.astype(F32)))
           - jnp.exp(jnp.sum(lam_q2.astype(F32) * lam_k2.astype(F32))) + lambda_init)
    slopes = alibi_slopes(DIFF_HEADS)
    key_pos = jnp.arange(t)
    n_blocks = t // Q_BLOCK
    q_blocks = jnp.moveaxis(q.reshape(b, n_blocks, Q_BLOCK, DIFF_HEADS, 2, DIFF_HEAD_DIM), 1, 0)
    starts = jnp.arange(n_blocks) * Q_BLOCK
    scale = DIFF_HEAD_DIM ** -0.5

    def attend_block(args):
        qb, start = args
        q_pos = start + jnp.arange(Q_BLOCK)
        s = jnp.einsum('bqhcd,bshcd->bhcqs', qb, k) * scale
        dist = jnp.abs(q_pos[:, None] - key_pos[None, :]).astype(F32)
        bias = -slopes[:, None, None, None] * dist
        visible = (key_pos[None, :] // CHUNK) <= (q_pos[:, None] // CHUNK)
        p = jax.nn.softmax(jnp.where(visible, s + bias, NEG_INF), axis=-1)
        attn = p[:, :, 0] - lam * p[:, :, 1]
        return jnp.einsum('bhqs,bshe->bqhe', attn, v)

    o = lax.map(attend_block, (q_blocks, starts))
    o = jnp.moveaxis(o, 0, 1).reshape(b, t, DIFF_HEADS, DIFF_V_DIM)
    o = rms_norm(o, sub_gain) * (1.0 - lambda_init)
    return o.reshape(b, t, DIFF_WIDTH).astype(h.dtype) @ w_o


def memory_cross_attention(h, mem_n, w_q, w_kv, q_gain, k_gain, w_o):
    b, t, _ = h.shape
    m = mem_n.shape[1]
    q = rms_norm((h @ w_q).reshape(b, t, MEM_HEADS, MEM_HEAD_DIM), q_gain).astype(F32)
    k, v = jnp.split(mem_n @ w_kv, 2, axis=-1)
    k = rms_norm(k.reshape(b, m, MEM_HEADS, MEM_HEAD_DIM), k_gain).astype(F32)
    v = v.reshape(b, m, MEM_HEADS, MEM_HEAD_DIM).astype(F32)
    p = jax.nn.softmax(jnp.einsum('bthd,bmhd->bhtm', q, k) * MEM_HEAD_DIM ** -0.5, axis=-1)
    o = jnp.einsum('bhtm,bmhd->bthd', p, v).reshape(b, t, D_MODEL)
    return o.astype(h.dtype) @ w_o


def swiglu(h, w_gate, w_up, w_down):
    return (jax.nn.silu(h @ w_gate) * (h @ w_up)) @ w_down


def moe_swiglu(h, w_router, b_router, w_gate, w_up, w_down):
    logits = (h @ w_router).astype(F32) + b_router.astype(F32)
    top_logits, top_idx = lax.top_k(logits, TOP_K)
    top_w = jax.nn.softmax(top_logits, axis=-1)
    gates = jnp.sum(jax.nn.one_hot(top_idx, N_EXPERTS, dtype=F32) * top_w[..., None], axis=-2)
    out = jnp.zeros(h.shape, F32)
    for e in range(N_EXPERTS):
        y_e = swiglu(h, w_gate[e], w_up[e], w_down[e]).astype(F32)
        out = out + gates[..., e:e + 1] * y_e
    return out.astype(h.dtype)


def setup_inputs(seed: int = 0) -> dict:
    key = jax.random.key(seed)
    keys = iter(jax.random.split(key, 80))

    def nrm(shape, scale):
        return jax.random.normal(next(keys), shape, F32) * scale

    def gain(shape):
        return 1.0 + nrm(shape, 0.02)

    D, L, NE, NO = D_MODEL, DEPTH, N_EVEN, N_ODD
    G, P, I = S5_GROUPS, S5_STATE, S5_GROUP
    decay_ramp = -6.5 + 5.0 * jnp.linspace(0.0, 1.0, RWKV_WIDTH, dtype=F32) ** 0.85
    return {
        'x': nrm((BATCH, SEQ, D), 1.0),
        'mem': nrm((BATCH, MEM_LEN, D), 1.0),
        'norm_mix': gain((L, D)),
        'norm_xattn': gain((L, D)),
        'norm_mem': gain((L, D)),
        'norm_ffn': gain((L, D)),
        'xa_w_q': nrm((L, D, D), D ** -0.5),
        'xa_w_kv': nrm((L, D, 2 * D), D ** -0.5),
        'xa_q_gain': gain((L, MEM_HEAD_DIM)),
        'xa_k_gain': gain((L, MEM_HEAD_DIM)),
        'xa_w_o': nrm((L, D, D), D ** -0.5),
        'hy_w_in': nrm((NE, D, IN_COLS), D ** -0.5),
        'rw_mu': jax.random.uniform(next(keys), (NE, RWKV_COLS), F32),
        'rw_w0': decay_ramp + nrm((NE, RWKV_WIDTH), 0.01),
        'rw_w_up': nrm((NE, DECAY_LORA, RWKV_WIDTH), 0.1),
        'rw_a0': nrm((NE, RWKV_WIDTH), 0.1),
        'rw_a_up': nrm((NE, ICLR_LORA, RWKV_WIDTH), ICLR_LORA ** -0.5),
        'rw_g_up': nrm((NE, GATE_LORA, RWKV_WIDTH), GATE_LORA ** -0.5),
        'rw_k_k': 0.85 + nrm((NE, RWKV_WIDTH), 0.02),
        'rw_k_a': gain((NE, RWKV_WIDTH)),
        'rw_r_k': nrm((NE, RWKV_HEADS, RWKV_HEAD_DIM), 0.1),
        'rw_ln_w': gain((NE, RWKV_WIDTH)),
        'rw_ln_b': nrm((NE, RWKV_WIDTH), 0.02),
        's5_lam_re': -0.5 + nrm((NE, G, P), 0.01),
        's5_lam_im': jnp.pi * jnp.arange(P, dtype=F32) + nrm((NE, G, P), 0.01),
        's5_log_dt': jax.random.uniform(next(keys), (NE, G), F32, math.log(S5_DT_MIN), math.log(S5_DT_MAX)),
        's5_b_re': nrm((NE, G, P, I), (2 * I) ** -0.5),
        's5_b_im': nrm((NE, G, P, I), (2 * I) ** -0.5),
        's5_c_re': nrm((NE, G, I, P), P ** -0.5),
        's5_c_im': nrm((NE, G, I, P), P ** -0.5),
        's5_d': nrm((NE, S5_WIDTH), 1.0),
        's5_w_glu': nrm((NE, S5_WIDTH, S5_WIDTH), S5_WIDTH ** -0.5),
        's5_b_glu': nrm((NE, S5_WIDTH), 0.02),
        'hy_w_out': nrm((NE, MIX_WIDTH, D), MIX_WIDTH ** -0.5),
        'ff_w_gate': nrm((NE, D, D_FF), D ** -0.5),
        'ff_w_up': nrm((NE, D, D_FF), D ** -0.5),
        'ff_w_down': nrm((NE, D_FF, D), D_FF ** -0.5),
        'da_w_qkv': nrm((NO, D, 3 * DIFF_WIDTH), D ** -0.5),
        'da_q_gain': gain((NO, DIFF_HEAD_DIM)),
        'da_k_gain': gain((NO, DIFF_HEAD_DIM)),
        'da_lam_q1': nrm((NO, DIFF_HEAD_DIM), 0.1),
        'da_lam_k1': nrm((NO, DIFF_HEAD_DIM), 0.1),
        'da_lam_q2': nrm((NO, DIFF_HEAD_DIM), 0.1),
        'da_lam_k2': nrm((NO, DIFF_HEAD_DIM), 0.1),
        'da_sub_gain': gain((NO, DIFF_V_DIM)),
        'da_w_o': nrm((NO, DIFF_WIDTH, D), DIFF_WIDTH ** -0.5),
        'moe_w_router': nrm((NO, D, N_EXPERTS), D ** -0.5),
        'moe_b_router': nrm((NO, N_EXPERTS), 0.01),
        'moe_w_gate': nrm((NO, N_EXPERTS, D, D_FF_EXPERT), D ** -0.5),
        'moe_w_up': nrm((NO, N_EXPERTS, D, D_FF_EXPERT), D ** -0.5),
        'moe_w_down': nrm((NO, N_EXPERTS, D_FF_EXPERT, D), D_FF_EXPERT ** -0.5),
    }


def reference(x, mem, norm_mix, norm_xattn, norm_mem, norm_ffn,
              xa_w_q, xa_w_kv, xa_q_gain, xa_k_gain, xa_w_o,
              hy_w_in, rw_mu, rw_w0, rw_w_up, rw_a0, rw_a_up, rw_g_up, rw_k_k, rw_k_a, rw_r_k,
              rw_ln_w, rw_ln_b,
              s5_lam_re, s5_lam_im, s5_log_dt, s5_b_re, s5_b_im, s5_c_re, s5_c_im, s5_d,
              s5_w_glu, s5_b_glu, hy_w_out,
              ff_w_gate, ff_w_up, ff_w_down,
              da_w_qkv, da_q_gain, da_k_gain, da_lam_q1, da_lam_k1, da_lam_q2, da_lam_k2,
              da_sub_gain, da_w_o,
              moe_w_router, moe_b_router, moe_w_gate, moe_w_up, moe_w_down):
    h = x
    for l in range(DEPTH):
        i = l // 2
        hn = rms_norm(h, norm_mix[l])
        if l % 2 == 0:
            z = hn @ hy_w_in[i]
            y_a = rwkv7_time_mix(z[..., :RWKV_COLS], rw_mu[i], rw_w0[i], rw_w_up[i], rw_a0[i],
                                 rw_a_up[i], rw_g_up[i], rw_k_k[i], rw_k_a[i], rw_r_k[i],
                                 rw_ln_w[i], rw_ln_b[i])
            y_b = s5_ssm(z[..., RWKV_COLS:], s5_lam_re[i], s5_lam_im[i], s5_log_dt[i], s5_b_re[i],
                         s5_b_im[i], s5_c_re[i], s5_c_im[i], s5_d[i], s5_w_glu[i], s5_b_glu[i])
            y_mix = jnp.concatenate([y_a, y_b], axis=-1).astype(h.dtype) @ hy_w_out[i]
        else:
            lambda_init = 0.8 - 0.6 * math.exp(-0.3 * l)
            y_mix = diff_attention(hn, da_w_qkv[i], da_q_gain[i], da_k_gain[i], da_lam_q1[i],
                                   da_lam_k1[i], da_lam_q2[i], da_lam_k2[i], da_sub_gain[i],
                                   da_w_o[i], lambda_init)
        h = h + y_mix.astype(h.dtype)
        y_mem = memory_cross_attention(rms_norm(h, norm_xattn[l]), rms_norm(mem, norm_mem[l]),
                                       xa_w_q[l], xa_w_kv[l], xa_q_gain[l], xa_k_gain[l], xa_w_o[l])
        h = h + y_mem.astype(h.dtype)
        hn = rms_norm(h, norm_ffn[l])
        if l % 2 == 0:
            y_ff = swiglu(hn, ff_w_gate[i], ff_w_up[i], ff_w_down[i])
        else:
            y_ff = moe_swiglu(hn, moe_w_router[i], moe_b_router[i], moe_w_gate[i],
                              moe_w_up[i], moe_w_down[i])
        h = h + y_ff.astype(h.dtype)
    return h
```

```python
import functools
import math

import jax
import jax.numpy as jnp
from jax import lax
from jax.experimental import pallas as pl
from jax.experimental.pallas import tpu as pltpu

F32 = jnp.float32
BF16 = jnp.bfloat16

NORM_EPS = 1e-6
NEG_INF = -1e30
ATTN_CHUNK = 64

RWKV_HEAD = 64
RWKV_CHUNK = 64
DECAY_LORA = 64
ICLR_LORA = 64
GATE_LORA = 160
LORA_PAD = 384
RWKV_GN_EPS = 1e-5 * RWKV_HEAD

S5_GROUP = 16
S5_STATE = 64
S5_TILE_GROUPS = 8
S5_SEGMENTS = 8

DIFF_HEAD = 64
MEM_HEADS = 4
N_EXPERTS = 8
LANES = 128
ROUTER_LANES = LANES

V7X_VMEM_BYTES = 64 * 1024 * 1024


def _params(semantics, vmem_mib):
    assert vmem_mib * 1024 * 1024 < V7X_VMEM_BYTES
    return pltpu.CompilerParams(dimension_semantics=semantics,
                                vmem_limit_bytes=vmem_mib * 1024 * 1024)


def _rms(x, gain):
    return x * lax.rsqrt(jnp.mean(x * x, axis=-1, keepdims=True) + NORM_EPS) * gain


def _dot(a, b):
    return jnp.dot(a, b, preferred_element_type=F32)


def _dot_nt(a, b):
    return lax.dot_general(a, b, (((1,), (1,)), ((), ())), preferred_element_type=F32)


def _dot_tn(a, b):
    return lax.dot_general(a, b, (((0,), (0,)), ((), ())), preferred_element_type=F32)


def _group_ones(width, group):
    r = lax.broadcasted_iota(jnp.int32, (width, width), 0) // group
    c = lax.broadcasted_iota(jnp.int32, (width, width), 1) // group
    return jnp.where(r == c, 1.0, 0.0).astype(BF16)


def _group_sum(x, ones):
    hi = x.astype(BF16)
    lo = (x - hi.astype(F32)).astype(BF16)
    return _dot(hi, ones) + _dot(lo, ones)


def _in_proj_kernel(x_ref, g_ref, w_ref, zr_ref, u_ref, *, n_r):
    xn = _rms(x_ref[...], g_ref[...]).astype(BF16)
    z = _dot(xn, w_ref[...])
    zr_ref[...] = z[:, :n_r]
    u_ref[...] = z[:, n_r:].astype(u_ref.dtype)


def _in_proj(x, gain, w, n_r, tm):
    n, d = x.shape
    c = w.shape[1]
    tm = min(tm, n)
    return pl.pallas_call(
        functools.partial(_in_proj_kernel, n_r=n_r),
        grid=(n // tm,),
        in_specs=[pl.BlockSpec((tm, d), lambda i: (i, 0)),
                  pl.BlockSpec((1, d), lambda i: (0, 0)),
                  pl.BlockSpec((d, c), lambda i: (0, 0))],
        out_specs=[pl.BlockSpec((tm, n_r), lambda i: (i, 0)),
                   pl.BlockSpec((tm, c - n_r), lambda i: (i, 0))],
        out_shape=[jax.ShapeDtypeStruct((n, n_r), F32),
                   jax.ShapeDtypeStruct((n, c - n_r), BF16)],
        compiler_params=_params(("parallel",), 48),
        name="in_proj",
    )(x, gain, w)


def _rwkv_prep_kernel(z_ref, mu_ref, wl_ref, w0_ref, a0_ref, kk_ref, ka_ref,
                      r_ref, lw_ref, k_ref, v_ref, kn_ref, a_ref, g_ref, carry_ref, *, width):
    i = pl.program_id(1)

    @pl.when(i == 0)
    def _():
        carry_ref[...] = jnp.zeros_like(carry_ref)

    z = z_ref[0]
    rows = lax.broadcasted_iota(jnp.int32, z.shape, 0)
    prev = jnp.where(rows == 0, carry_ref[...], pltpu.roll(z, 1, axis=0))
    carry_ref[...] = z[z.shape[0] - 1:, :]
    zs = z + (prev - z) * mu_ref[...]

    w = width
    r = zs[:, :w]
    k = zs[:, w:2 * w]
    v = zs[:, 2 * w:3 * w]
    lo = zs[:, 3 * w:]
    lane = lax.broadcasted_iota(jnp.int32, lo.shape, 1)
    lo = jnp.where(lane < DECAY_LORA, jnp.tanh(lo),
                   jnp.where(lane < DECAY_LORA + ICLR_LORA, lo, jax.nn.sigmoid(lo)))
    m = _dot(lo.astype(BF16), wl_ref[...])
    w_log = -jax.nn.softplus(-(w0_ref[...] + m[:, :w])) - 0.5
    a = jax.nn.sigmoid(a0_ref[...] + m[:, w:2 * w])
    kn = k * kk_ref[...]
    ss = _group_sum(kn * kn, _group_ones(w, RWKV_HEAD))
    kn = kn * lax.rsqrt(jnp.maximum(ss, 1e-24))

    r_ref[0] = r
    lw_ref[0] = -jnp.exp(w_log)
    k_ref[0] = k * (1.0 + (a - 1.0) * ka_ref[...])
    v_ref[0] = v
    kn_ref[0] = kn
    a_ref[0] = a
    g_ref[0] = m[:, 2 * w:]


def _rwkv_prep(zr, mu, wl, w0, a0, k_k, k_a, width, tm):
    b, t, c = zr.shape
    tm = min(tm, t)
    row = lambda n: pl.BlockSpec((1, n), lambda bi, i: (0, 0))
    out = pl.BlockSpec((1, tm, width), lambda bi, i: (bi, i, 0))
    return pl.pallas_call(
        functools.partial(_rwkv_prep_kernel, width=width),
        grid=(b, t // tm),
        in_specs=[pl.BlockSpec((1, tm, c), lambda bi, i: (bi, i, 0)),
                  row(c),
                  pl.BlockSpec(wl.shape, lambda bi, i: (0, 0)),
                  row(width), row(width), row(width), row(width)],
        out_specs=[out] * 7,
        out_shape=[jax.ShapeDtypeStruct((b, t, width), F32)] * 7,
        scratch_shapes=[pltpu.VMEM((1, c), F32)],
        compiler_params=_params(("parallel", "arbitrary"), 48),
        name="rwkv_prep",
    )(zr, mu, wl, w0, a0, k_k, k_a)


def _rwkv_chunk_kernel(r_ref, lw_ref, k_ref, v_ref, kn_ref, a_ref, g_ref,
                       rk_ref, lnw_ref, lnb_ref, o_ref, st_ref, *, heads):
    c = pl.program_id(2)

    @pl.when(c == 0)
    def _():
        st_ref[...] = jnp.zeros_like(st_ref)

    L = r_ref.shape[1]
    n = RWKV_HEAD
    ri = lax.broadcasted_iota(jnp.int32, (L, L), 0)
    ci = lax.broadcasted_iota(jnp.int32, (L, L), 1)
    tri = jnp.where(ci <= ri, 1.0, 0.0).astype(BF16)
    strict = ci < ri
    incl = ci <= ri
    eye = jnp.where(ci == ri, 1.0, 0.0).astype(F32)

    lw_all = lw_ref[0]
    hi = lw_all.astype(BF16)
    lo = (lw_all - hi.astype(F32)).astype(BF16)
    cum_all = _dot(tri, hi) + _dot(tri, lo)
    r_all, k_all, v_all = r_ref[0], k_ref[0], v_ref[0]
    kn_all, a_all, g_all = kn_ref[0], a_ref[0], g_ref[0]
    rk_all, lnw_all, lnb_all = rk_ref[...], lnw_ref[...], lnb_ref[...]

    outs = []
    for h in range(heads):
        s = slice(h * n, (h + 1) * n)
        r, k, v, kn, a = r_all[:, s], k_all[:, s], v_all[:, s], kn_all[:, s], a_all[:, s]
        lw, cum = lw_all[:, s], cum_all[:, s]
        cum_ex = cum - lw
        c_mid = cum[L // 2 - 1:L // 2, :]
        c_end = cum[L - 1:L, :]
        alpha = -kn
        beta = kn * a
        e_in = jnp.exp(cum - c_mid)
        e_ex = jnp.exp(cum_ex - c_mid)
        e_inv = jnp.exp(c_mid - cum)
        e_abs = jnp.exp(cum)
        e_abs_ex = jnp.exp(cum_ex)
        e_end = jnp.exp(c_end - cum)
        lhs = jnp.concatenate([alpha * e_ex, r * e_in], axis=0).astype(BF16)
        m_b = _dot_nt(lhs, (beta * e_inv).astype(BF16))
        m_k = _dot_nt(lhs, (k * e_inv).astype(BF16))
        m_ab = jnp.where(strict, m_b[:L], 0.0)
        m_ak = jnp.where(strict, m_k[:L], 0.0).astype(BF16)
        m_rb = jnp.where(incl, m_b[L:], 0.0).astype(BF16)
        m_rk = jnp.where(incl, m_k[L:], 0.0).astype(BF16)

        inv = eye + m_ab
        p = m_ab
        steps = max(1, int(math.ceil(math.log2(L))) - 1)
        for _ in range(steps):
            pb = p.astype(BF16)
            p = _dot(pb, pb)
            inv = inv + _dot(inv.astype(BF16), p.astype(BF16))

        st = st_ref[h]
        v_b = v.astype(BF16)
        abs_lhs = jnp.concatenate([alpha * e_abs_ex, r * e_abs], axis=0).astype(BF16)
        from_state = _dot_nt(abs_lhs, st.astype(BF16))
        u = _dot(inv.astype(BF16), (from_state[:L] + _dot(m_ak, v_b)).astype(BF16))
        u_b = u.astype(BF16)
        o = from_state[L:] + _dot(m_rb, u_b) + _dot(m_rk, v_b)
        st_ref[h] = (st * jnp.exp(c_end)
                     + _dot_tn(u_b, (beta * e_end).astype(BF16))
                     + _dot_tn(v_b, (k * e_end).astype(BF16)))

        mean = jnp.mean(o, axis=-1, keepdims=True)
        d = o - mean
        var = jnp.mean(d * d, axis=-1, keepdims=True)
        o = d * lax.rsqrt(var + RWKV_GN_EPS) * lnw_all[:, s] + lnb_all[:, s]
        bonus = jnp.sum(r * k * rk_all[:, s], axis=-1, keepdims=True) * v
        outs.append((o + bonus) * g_all[:, s])
    o_ref[0] = jnp.concatenate(outs, axis=-1).astype(o_ref.dtype)


def _rwkv_chunk(r, lw, k, v, kn, a, g, r_k, ln_w, ln_b, heads_per_step):
    b, t, width = r.shape
    L = min(RWKV_CHUNK, t)
    lanes = heads_per_step * RWKV_HEAD
    seq = pl.BlockSpec((1, L, lanes), lambda bi, hi, ci: (bi, ci, hi))
    row = pl.BlockSpec((1, lanes), lambda bi, hi, ci: (0, hi))
    return pl.pallas_call(
        functools.partial(_rwkv_chunk_kernel, heads=heads_per_step),
        grid=(b, width // lanes, t // L),
        in_specs=[seq] * 7 + [row] * 3,
        out_specs=seq,
        out_shape=jax.ShapeDtypeStruct((b, t, width), BF16),
        scratch_shapes=[pltpu.VMEM((heads_per_step, RWKV_HEAD, RWKV_HEAD), F32)],
        compiler_params=_params(("parallel", "parallel", "arbitrary"), 32),
        name="rwkv_chunk",
    )(r, lw, k, v, kn, a, g, r_k, ln_w, ln_b)


def _gelu_tanh(x):
    return 0.5 * x * (1.0 + jnp.tanh(math.sqrt(2.0 / math.pi) * (x + 0.044715 * (x * x * x))))


def _s5_kernel(u_ref, bb_ref, cb_ref, lr_ref, li_ref, d_ref, y_ref, x_ref, pr_ref, pi_ref, *, seg_len):
    half = lr_ref.shape[-1]
    nseg = S5_SEGMENTS
    lr = lr_ref[0]
    li = li_ref[0]

    nb = half // LANES
    for cblk in range(2 * nb):
        x_ref[cblk] = _dot(u_ref[0], bb_ref[0, :, cblk * LANES:(cblk + 1) * LANES])

    pr_ref[0:1, :] = lr
    pi_ref[0:1, :] = li
    filled = 1
    while filled < seg_len:
        n = min(filled, seg_len - filled)
        tr = pr_ref[filled - 1:filled, :]
        ti = pi_ref[filled - 1:filled, :]
        sr = pr_ref[0:n, :]
        si = pi_ref[0:n, :]
        pr_ref[filled:filled + n, :] = sr * tr - si * ti
        pi_ref[filled:filled + n, :] = sr * ti + si * tr
        filled += n

    lrb = jnp.broadcast_to(lr, (nseg, half))
    lib = jnp.broadcast_to(li, (nseg, half))

    def step(i, carry):
        rows = pl.ds(i, nseg, stride=seg_len)
        out = []
        for cblk in range(nb):
            sr, si = carry[cblk]
            lanes = slice(cblk * LANES, (cblk + 1) * LANES)
            nr = lrb[:, lanes] * sr - lib[:, lanes] * si + x_ref[cblk, rows, :]
            ni = lrb[:, lanes] * si + lib[:, lanes] * sr + x_ref[nb + cblk, rows, :]
            x_ref[cblk, rows, :] = nr
            x_ref[nb + cblk, rows, :] = ni
            out.append((nr, ni))
        return tuple(out)

    zero = jnp.zeros((nseg, LANES), F32)
    lax.fori_loop(0, seg_len, step, tuple((zero, zero) for _ in range(nb)))

    lam_seg_r = pr_ref[seg_len - 1:seg_len, :]
    lam_seg_i = pi_ref[seg_len - 1:seg_len, :]
    cr = jnp.zeros((1, half), F32)
    ci = jnp.zeros((1, half), F32)
    d = d_ref[...]
    cb = cb_ref[0]
    for s in range(nseg):
        rows = slice(s * seg_len, (s + 1) * seg_len)
        xr = jnp.concatenate([x_ref[cblk, rows, :] for cblk in range(nb)], axis=-1)
        xi = jnp.concatenate([x_ref[nb + cblk, rows, :] for cblk in range(nb)], axis=-1)
        last_r = xr[seg_len - 1:, :]
        last_i = xi[seg_len - 1:, :]
        if s > 0:
            xr = xr + (pr_ref[...] * cr - pi_ref[...] * ci)
            xi = xi + (pr_ref[...] * ci + pi_ref[...] * cr)
        cr, ci = (last_r + lam_seg_r * cr - lam_seg_i * ci,
                  last_i + lam_seg_r * ci + lam_seg_i * cr)
        x = jnp.concatenate([xr, xi], axis=-1).astype(BF16)
        y = _dot(x, cb) + d * u_ref[0, rows, :].astype(F32)
        y_ref[0, rows, :] = _gelu_tanh(y).astype(y_ref.dtype)


def _s5(u, bblk, cblk, lam_r, lam_i, d_skip):
    b, t, width = u.shape
    lanes = S5_TILE_GROUPS * S5_GROUP
    half = S5_TILE_GROUPS * S5_STATE
    seg_len = t // S5_SEGMENTS
    tile = lambda shape: pl.BlockSpec((1,) + shape, lambda bi, ci: (ci, 0, 0))
    return pl.pallas_call(
        functools.partial(_s5_kernel, seg_len=seg_len),
        grid=(b, width // lanes),
        in_specs=[pl.BlockSpec((1, t, lanes), lambda bi, ci: (bi, 0, ci)),
                  tile((lanes, 2 * half)), tile((2 * half, lanes)),
                  tile((1, half)), tile((1, half)),
                  pl.BlockSpec((1, lanes), lambda bi, ci: (0, ci))],
        out_specs=pl.BlockSpec((1, t, lanes), lambda bi, ci: (bi, 0, ci)),
        out_shape=jax.ShapeDtypeStruct((b, t, width), BF16),
        scratch_shapes=[pltpu.VMEM((2 * half // LANES, t, LANES), F32),
                        pltpu.VMEM((seg_len, half), F32),
                        pltpu.VMEM((seg_len, half), F32)],
        compiler_params=_params(("parallel", "parallel"), 48),
        name="s5",
    )(u, bblk, cblk, lam_r, lam_i, d_skip)


def _s5_tables(lam_re, lam_im, log_dt, b_re, b_im, c_re, c_im):
    g, p = lam_re.shape
    tiles = g // S5_TILE_GROUPS
    lam = lax.complex(lam_re.astype(F32), lam_im.astype(F32))
    dt = jnp.exp(log_dt.astype(F32))[:, None]
    lam_bar = jnp.exp(lam * dt)
    b_bar = ((lam_bar - 1.0) / lam)[..., None] * lax.complex(b_re.astype(F32), b_im.astype(F32))
    eye = jnp.eye(S5_TILE_GROUPS, dtype=F32)

    def pack_b(m):
        m = m.reshape(tiles, S5_TILE_GROUPS, p, S5_GROUP)
        return jnp.einsum('tgpi,gh->tgihp', m, eye).reshape(tiles, S5_TILE_GROUPS * S5_GROUP,
                                                            S5_TILE_GROUPS * p)

    def pack_c(m):
        m = m.reshape(tiles, S5_TILE_GROUPS, S5_GROUP, p)
        return jnp.einsum('tgip,gh->tgphi', m, eye).reshape(tiles, S5_TILE_GROUPS * p,
                                                            S5_TILE_GROUPS * S5_GROUP)

    bblk = jnp.concatenate([pack_b(jnp.real(b_bar)), pack_b(jnp.imag(b_bar))], axis=-1).astype(BF16)
    cblk = jnp.concatenate([pack_c(c_re.astype(F32)), -pack_c(c_im.astype(F32))], axis=1).astype(BF16)
    lam_r = jnp.real(lam_bar).reshape(tiles, 1, S5_TILE_GROUPS * p)
    lam_i = jnp.imag(lam_bar).reshape(tiles, 1, S5_TILE_GROUPS * p)
    return bblk, cblk, lam_r, lam_i


def _mix_out_kernel(h_ref, ya_ref, yb_ref, wglu_ref, bglu_ref, woa_ref, wob_ref, o_ref):
    yb = yb_ref[...]
    gate = jax.nn.sigmoid(_dot(yb, wglu_ref[...]) + bglu_ref[...])
    yb = (yb.astype(F32) * gate).astype(BF16)
    o_ref[...] = h_ref[...] + _dot(ya_ref[...], woa_ref[...]) + _dot(yb, wob_ref[...])


def _mix_out(h, ya, yb, w_glu, b_glu, wo_a, wo_b, tm):
    n, d = h.shape
    tm = min(tm, n)
    wa, wb = ya.shape[1], yb.shape[1]
    full = lambda a: pl.BlockSpec(a.shape, lambda i: (0, 0))
    return pl.pallas_call(
        _mix_out_kernel,
        grid=(n // tm,),
        in_specs=[pl.BlockSpec((tm, d), lambda i: (i, 0)),
                  pl.BlockSpec((tm, wa), lambda i: (i, 0)),
                  pl.BlockSpec((tm, wb), lambda i: (i, 0)),
                  full(w_glu), full(b_glu), full(wo_a), full(wo_b)],
        out_specs=pl.BlockSpec((tm, d), lambda i: (i, 0)),
        out_shape=jax.ShapeDtypeStruct((n, d), F32),
        compiler_params=_params(("parallel",), 32),
        name="mix_out",
    )(h, ya, yb, w_glu, b_glu, wo_a, wo_b)


def _mem_kv_kernel(m_ref, g_ref, w_ref, kg_ref, k_ref, v_ref, *, heads):
    mn = _rms(m_ref[0], g_ref[...]).astype(BF16)
    kv = _dot(mn, w_ref[...])
    d = kv.shape[1] // 2
    hd = d // heads
    for h in range(heads):
        s = slice(h * hd, (h + 1) * hd)
        k_ref[0, :, s] = _rms(kv[:, s], kg_ref[...]).astype(k_ref.dtype)
    v_ref[0] = kv[:, d:].astype(v_ref.dtype)


def _mem_kv(mem, gain, w_kv, k_gain, heads):
    b, m, d = mem.shape
    blk = pl.BlockSpec((1, m, d), lambda bi: (bi, 0, 0))
    return pl.pallas_call(
        functools.partial(_mem_kv_kernel, heads=heads),
        grid=(b,),
        in_specs=[blk, pl.BlockSpec((1, d), lambda bi: (0, 0)),
                  pl.BlockSpec(w_kv.shape, lambda bi: (0, 0)),
                  pl.BlockSpec(k_gain.shape, lambda bi: (0, 0))],
        out_specs=[blk, blk],
        out_shape=[jax.ShapeDtypeStruct((b, m, d), BF16)] * 2,
        compiler_params=_params(("parallel",), 32),
        name="mem_kv",
    )(mem, gain, w_kv, k_gain)


def _xattn_kernel(h_ref, g_ref, wq_ref, qg_ref, k_ref, v_ref, wo_ref, o_ref, *, heads):
    h = h_ref[0]
    hn = _rms(h, g_ref[...]).astype(BF16)
    q = _dot(hn, wq_ref[...])
    hd = q.shape[1] // heads
    k = k_ref[0]
    v = v_ref[0]
    outs = []
    for i in range(heads):
        s = slice(i * hd, (i + 1) * hd)
        qh = (_rms(q[:, s], qg_ref[...]) * hd ** -0.5).astype(BF16)
        sc = _dot_nt(qh, k[:, s])
        sc = sc - jnp.max(sc, axis=-1, keepdims=True)
        p = jnp.exp(sc)
        p = p / jnp.sum(p, axis=-1, keepdims=True)
        outs.append(_dot(p.astype(BF16), v[:, s]).astype(BF16))
    o = jnp.concatenate(outs, axis=-1)
    o_ref[0] = h + _dot(o, wo_ref[...])


def _xattn(h, gain, w_q, q_gain, k, v, w_o, heads, tm):
    b, t, d = h.shape
    m = k.shape[1]
    tm = min(tm, t)
    full = lambda a: pl.BlockSpec(a.shape, lambda bi, i: (0,) * a.ndim)
    tok = pl.BlockSpec((1, tm, d), lambda bi, i: (bi, i, 0))
    mem = pl.BlockSpec((1, m, d), lambda bi, i: (bi, 0, 0))
    return pl.pallas_call(
        functools.partial(_xattn_kernel, heads=heads),
        grid=(b, t // tm),
        in_specs=[tok, full(gain), full(w_q), full(q_gain), mem, mem, full(w_o)],
        out_specs=tok,
        out_shape=jax.ShapeDtypeStruct((b, t, d), F32),
        compiler_params=_params(("parallel", "parallel"), 40),
        name="xattn",
    )(h, gain, w_q, q_gain, k, v, w_o)


def _ffn_dense_kernel(h_ref, g_ref, wg_ref, wu_ref, wd_ref, o_ref, xn_ref, acc_ref):
    j = pl.program_id(1)

    @pl.when(j == 0)
    def _():
        xn_ref[...] = _rms(h_ref[...], g_ref[...]).astype(BF16)
        acc_ref[...] = jnp.zeros_like(acc_ref)

    xn = xn_ref[...]
    act = (jax.nn.silu(_dot(xn, wg_ref[...])) * _dot(xn, wu_ref[...])).astype(BF16)
    acc_ref[...] += _dot(act, wd_ref[...])

    @pl.when(j == pl.num_programs(1) - 1)
    def _():
        o_ref[...] = h_ref[...] + acc_ref[...]


def _ffn_dense(h, gain, w_gate, w_up, w_down, tm, tf):
    n, d = h.shape
    f = w_gate.shape[1]
    tm, tf = min(tm, n), min(tf, f)
    return pl.pallas_call(
        _ffn_dense_kernel,
        grid=(n // tm, f // tf),
        in_specs=[pl.BlockSpec((tm, d), lambda i, j: (i, 0)),
                  pl.BlockSpec((1, d), lambda i, j: (0, 0)),
                  pl.BlockSpec((d, tf), lambda i, j: (0, j)),
                  pl.BlockSpec((d, tf), lambda i, j: (0, j)),
                  pl.BlockSpec((tf, d), lambda i, j: (j, 0))],
        out_specs=pl.BlockSpec((tm, d), lambda i, j: (i, 0)),
        out_shape=jax.ShapeDtypeStruct((n, d), F32),
        scratch_shapes=[pltpu.VMEM((tm, d), BF16), pltpu.VMEM((tm, d), F32)],
        compiler_params=_params(("parallel", "arbitrary"), 48),
        name="ffn_dense",
    )(h, gain, w_gate, w_up, w_down)


def _ffn_expert_kernel(te_ref, nu_ref, x_ref, wg_ref, wu_ref, wd_ref, o_ref, acc_ref):
    i = pl.program_id(0)
    j = pl.program_id(1)

    @pl.when(i < nu_ref[0])
    def _():
        @pl.when(j == 0)
        def _():
            acc_ref[...] = jnp.zeros_like(acc_ref)

        x = x_ref[...]
        act = (jax.nn.silu(_dot(x, wg_ref[0])) * _dot(x, wu_ref[0])).astype(BF16)
        acc_ref[...] += _dot(act, wd_ref[0])

    @pl.when(j == pl.num_programs(1) - 1)
    def _():
        o_ref[...] = acc_ref[...].astype(o_ref.dtype)


def _ffn_expert(tile_expert, n_used, xs, w_gate, w_up, w_down, tm, tf):
    p, d = xs.shape
    f = w_gate.shape[2]
    tf = min(tf, f)
    nj = f // tf

    def col(i, j, te, nu):
        return (te[i], 0, jnp.where(i < nu[0], j, nj - 1))

    def rowb(i, j, te, nu):
        return (te[i], jnp.where(i < nu[0], j, nj - 1), 0)

    grid_spec = pltpu.PrefetchScalarGridSpec(
        num_scalar_prefetch=2,
        grid=(p // tm, nj),
        in_specs=[pl.BlockSpec((tm, d), lambda i, j, te, nu: (i, 0)),
                  pl.BlockSpec((1, d, tf), col),
                  pl.BlockSpec((1, d, tf), col),
                  pl.BlockSpec((1, tf, d), rowb)],
        out_specs=pl.BlockSpec((tm, d), lambda i, j, te, nu: (i, 0)),
        scratch_shapes=[pltpu.VMEM((tm, d), F32)],
    )
    return pl.pallas_call(
        _ffn_expert_kernel,
        grid_spec=grid_spec,
        out_shape=jax.ShapeDtypeStruct((p, d), BF16),
        compiler_params=_params(("arbitrary", "arbitrary"), 48),
        name="ffn_expert",
    )(tile_expert, n_used, xs, w_gate, w_up, w_down)


def _qkv_kernel(x_ref, g_ref, w_ref, qkg_ref, qk_ref, v_ref, *, n_qk):
    xn = _rms(x_ref[...], g_ref[...]).astype(BF16)
    z = _dot(xn, w_ref[...])
    blk = 256
    ones = _group_ones(blk, DIFF_HEAD)
    for c in range(n_qk // blk):
        s = slice(c * blk, (c + 1) * blk)
        zc = z[:, s]
        ms = _group_sum(zc * zc, ones) * (1.0 / DIFF_HEAD)
        qk_ref[:, s] = (zc * lax.rsqrt(ms + NORM_EPS) * qkg_ref[:, s]).astype(qk_ref.dtype)
    v_ref[...] = z[:, n_qk:].astype(v_ref.dtype)


def _qkv(x, gain, w, qk_gain, tm):
    n, d = x.shape
    c = w.shape[1]
    n_qk = qk_gain.shape[1]
    tm = min(tm, n)
    return pl.pallas_call(
        functools.partial(_qkv_kernel, n_qk=n_qk),
        grid=(n // tm,),
        in_specs=[pl.BlockSpec((tm, d), lambda i: (i, 0)),
                  pl.BlockSpec((1, d), lambda i: (0, 0)),
                  pl.BlockSpec((d, c), lambda i: (0, 0)),
                  pl.BlockSpec((1, n_qk), lambda i: (0, 0))],
        out_specs=[pl.BlockSpec((tm, n_qk), lambda i: (i, 0)),
                   pl.BlockSpec((tm, c - n_qk), lambda i: (i, 0))],
        out_shape=[jax.ShapeDtypeStruct((n, n_qk), BF16),
                   jax.ShapeDtypeStruct((n, c - n_qk), BF16)],
        compiler_params=_params(("parallel",), 48),
        name="qkv",
    )(x, gain, w, qk_gain)


def _diff_attn_kernel(slope_ref, lam_ref, q_ref, k_ref, v_ref, sg_ref, o_ref,
                      qs_ref, m_ref, l_ref, acc_ref, *, out_scale):
    h = pl.program_id(1)
    i = pl.program_id(2)
    j = pl.program_id(3)
    tq = q_ref.shape[1]
    tk = k_ref.shape[1]
    slope = slope_ref[h]

    @pl.when(j == 0)
    def _():
        q = q_ref[0]
        lane = lax.broadcasted_iota(jnp.int32, q.shape, 1)
        zero = jnp.zeros_like(q)
        qs_ref[0:tq, :] = jnp.where(lane < DIFF_HEAD, q, zero)
        qs_ref[tq:2 * tq, :] = jnp.where(lane < DIFF_HEAD, zero, q)
        m_ref[...] = jnp.full_like(m_ref, NEG_INF)
        l_ref[...] = jnp.zeros_like(l_ref)
        acc_ref[...] = jnp.zeros_like(acc_ref)

    def update(s):
        m_prev = m_ref[...]
        m_new = jnp.maximum(m_prev, jnp.max(s, axis=-1, keepdims=True))
        corr = jnp.exp(m_prev - m_new)
        p = jnp.exp(s - m_new)
        l_ref[...] = corr * l_ref[...] + jnp.sum(p, axis=-1, keepdims=True)
        acc_ref[...] = corr * acc_ref[...] + _dot(p.astype(BF16), v_ref[0])
        m_ref[...] = m_new

    def rel():
        r = lax.broadcasted_iota(jnp.int32, (2 * tq, tk), 0)
        c = lax.broadcasted_iota(jnp.int32, (2 * tq, tk), 1)
        r = jnp.where(r >= tq, r - tq, r)
        return r, c

    @pl.when(j < i)
    def _():
        r, c = rel()
        dist = (r - c + (i - j) * tq).astype(F32)
        update(_dot_nt(qs_ref[...], k_ref[0]) - slope * dist)

    @pl.when(j == i)
    def _():
        r, c = rel()
        dist = jnp.abs(r - c).astype(F32)
        s = _dot_nt(qs_ref[...], k_ref[0]) - slope * dist
        update(jnp.where(c // ATTN_CHUNK <= r // ATTN_CHUNK, s, NEG_INF))

    @pl.when(j == pl.num_programs(3) - 1)
    def _():
        o = acc_ref[...] / l_ref[...]
        o = o[:tq] - lam_ref[0] * o[tq:]
        o = o * lax.rsqrt(jnp.mean(o * o, axis=-1, keepdims=True) + NORM_EPS) * sg_ref[...]
        o_ref[0] = (o * out_scale).astype(o_ref.dtype)


def _diff_attn(slopes, lam, qk, v, sub_gain, heads, out_scale, tq):
    b, t, _ = v.shape
    hw = 2 * DIFF_HEAD
    tq = min(tq, t)
    nt = t // tq
    smem = pl.BlockSpec(memory_space=pltpu.SMEM)
    return pl.pallas_call(
        functools.partial(_diff_attn_kernel, out_scale=out_scale),
        grid=(b, heads, nt, nt),
        in_specs=[smem, smem,
                  pl.BlockSpec((1, tq, hw), lambda bi, h, i, j: (bi, i, h)),
                  pl.BlockSpec((1, tq, hw), lambda bi, h, i, j: (bi, jnp.minimum(j, i), heads + h)),
                  pl.BlockSpec((1, tq, hw), lambda bi, h, i, j: (bi, jnp.minimum(j, i), h)),
                  pl.BlockSpec((1, hw), lambda bi, h, i, j: (0, 0))],
        out_specs=pl.BlockSpec((1, tq, hw), lambda bi, h, i, j: (bi, i, h)),
        out_shape=jax.ShapeDtypeStruct((b, t, heads * hw), BF16),
        scratch_shapes=[pltpu.VMEM((2 * tq, hw), BF16),
                        pltpu.VMEM((2 * tq, 1), F32),
                        pltpu.VMEM((2 * tq, 1), F32),
                        pltpu.VMEM((2 * tq, hw), F32)],
        compiler_params=_params(("parallel", "parallel", "parallel", "arbitrary"), 32),
        name="diff_attn",
    )(slopes, lam, qk, qk, v, sub_gain)


def _proj_residual_kernel(h_ref, a_ref, w_ref, o_ref):
    o_ref[...] = h_ref[...] + _dot(a_ref[...], w_ref[...])


def _proj_residual(h, a, w, tm):
    n, d = h.shape
    tm = min(tm, n)
    return pl.pallas_call(
        _proj_residual_kernel,
        grid=(n // tm,),
        in_specs=[pl.BlockSpec((tm, d), lambda i: (i, 0)),
                  pl.BlockSpec((tm, a.shape[1]), lambda i: (i, 0)),
                  pl.BlockSpec(w.shape, lambda i: (0, 0))],
        out_specs=pl.BlockSpec((tm, d), lambda i: (i, 0)),
        out_shape=jax.ShapeDtypeStruct((n, d), F32),
        compiler_params=_params(("parallel",), 32),
        name="proj_residual",
    )(h, a, w)


def _router_kernel(h_ref, g_ref, w_ref, b_ref, xn_ref, r_ref):
    xn = _rms(h_ref[...], g_ref[...])
    xn_ref[...] = xn.astype(xn_ref.dtype)
    logits = jnp.dot(xn, w_ref[...], preferred_element_type=F32,
                     precision=lax.Precision.HIGHEST) + b_ref[...]
    lane = lax.broadcasted_iota(jnp.int32, logits.shape, 1).astype(F32)
    big = float(ROUTER_LANES)
    l1 = jnp.max(logits, axis=-1, keepdims=True)
    i1 = jnp.min(jnp.where(logits == l1, lane, big), axis=-1, keepdims=True)
    rest = jnp.where(lane == i1, NEG_INF, logits)
    l2 = jnp.max(rest, axis=-1, keepdims=True)
    i2 = jnp.min(jnp.where(rest == l2, lane, big), axis=-1, keepdims=True)
    e = jnp.exp(l2 - l1)
    w1 = 1.0 / (1.0 + e)
    w2 = e / (1.0 + e)
    r_ref[...] = jnp.where(lane == 0.0, i1, jnp.where(lane == 1.0, i2,
                           jnp.where(lane == 2.0, w1, jnp.where(lane == 3.0, w2, 0.0))))


def _router(h, gain, w_pad, b_pad, tm):
    n, d = h.shape
    tm = min(tm, n)
    return pl.pallas_call(
        _router_kernel,
        grid=(n // tm,),
        in_specs=[pl.BlockSpec((tm, d), lambda i: (i, 0)),
                  pl.BlockSpec((1, d), lambda i: (0, 0)),
                  pl.BlockSpec(w_pad.shape, lambda i: (0, 0)),
                  pl.BlockSpec(b_pad.shape, lambda i: (0, 0))],
        out_specs=[pl.BlockSpec((tm, d), lambda i: (i, 0)),
                   pl.BlockSpec((tm, ROUTER_LANES), lambda i: (i, 0))],
        out_shape=[jax.ShapeDtypeStruct((n, d), BF16),
                   jax.ShapeDtypeStruct((n, ROUTER_LANES), F32)],
        compiler_params=_params(("parallel",), 32),
        name="router",
    )(h, gain, w_pad, b_pad)


def _gather_copy(idx_ref, x_hbm, o_ref, sem, base, r):
    return pltpu.make_async_copy(x_hbm.at[pl.ds(idx_ref[base + r], 1)], o_ref.at[pl.ds(r, 1)], sem)


def _gather_kernel(idx_ref, x_hbm, o_ref, sem):
    rows = o_ref.shape[0]
    base = pl.program_id(0) * rows

    def start(r, c):
        _gather_copy(idx_ref, x_hbm, o_ref, sem, base, r).start()
        return c

    def wait(r, c):
        _gather_copy(idx_ref, x_hbm, o_ref, sem, base, r).wait()
        return c

    lax.fori_loop(0, rows, start, 0)
    lax.fori_loop(0, rows, wait, 0)


def _gather_rows(idx, x, tg):
    p = idx.shape[0]
    w = x.shape[1]
    tg = min(tg, p)
    grid_spec = pltpu.PrefetchScalarGridSpec(
        num_scalar_prefetch=1,
        grid=(p // tg,),
        in_specs=[pl.BlockSpec(memory_space=pl.ANY)],
        out_specs=pl.BlockSpec((tg, w), lambda i, idx_ref: (i, 0)),
        scratch_shapes=[pltpu.SemaphoreType.DMA(())],
    )
    return pl.pallas_call(
        _gather_kernel,
        grid_spec=grid_spec,
        out_shape=jax.ShapeDtypeStruct((p, w), x.dtype),
        compiler_params=_params(("arbitrary",), 32),
        name="gather_rows",
    )(idx, x)


def _combine_kernel(h_ref, y0_ref, y1_ref, r_ref, o_ref):
    r = r_ref[...]
    w0 = r[:, 2:3]
    w1 = r[:, 3:4]
    o_ref[...] = h_ref[...] + w0 * y0_ref[...].astype(F32) + w1 * y1_ref[...].astype(F32)


def _combine(h, yg, route, tm):
    n, d = h.shape
    tm = min(tm, n)
    nb = n // tm
    return pl.pallas_call(
        _combine_kernel,
        grid=(nb,),
        in_specs=[pl.BlockSpec((tm, d), lambda i: (i, 0)),
                  pl.BlockSpec((tm, d), lambda i: (i, 0)),
                  pl.BlockSpec((tm, d), lambda i: (i + nb, 0)),
                  pl.BlockSpec((tm, ROUTER_LANES), lambda i: (i, 0))],
        out_specs=pl.BlockSpec((tm, d), lambda i: (i, 0)),
        out_shape=jax.ShapeDtypeStruct((n, d), F32),
        compiler_params=_params(("parallel",), 32),
        name="combine",
    )(h, yg, yg, route)


def _to_words(x):
    n, d = x.shape
    return lax.bitcast_convert_type(x.reshape(n, d // 2, 2), jnp.uint32)


def _from_words(x):
    n, w = x.shape
    return lax.bitcast_convert_type(x, BF16).reshape(n, 2 * w)


def _moe(h, gain, w_router, b_router, w_gate, w_up, w_down, tm, tf):
    n, d = h.shape
    e = w_gate.shape[0]
    w_pad = jnp.zeros((d, ROUTER_LANES), F32).at[:, :e].set(w_router.astype(F32))
    b_pad = jnp.full((1, ROUTER_LANES), NEG_INF, F32).at[0, :e].set(b_router.astype(F32))
    xn, route = _router(h, gain, w_pad, b_pad, 512)

    tm = min(tm, n)
    experts = jnp.concatenate([route[:, 0], route[:, 1]]).astype(jnp.int32)
    onehot = (experts[:, None] == jnp.arange(e, dtype=jnp.int32)[None, :]).astype(jnp.int32)
    rank = jnp.take_along_axis(jnp.cumsum(onehot, axis=0) - onehot, experts[:, None], axis=1)[:, 0]
    counts = jnp.sum(onehot, axis=0)
    padded = ((counts + tm - 1) // tm) * tm
    ends = jnp.cumsum(padded)
    starts = ends - padded
    pos = starts[experts] + rank
    cap = 2 * n + e * tm
    tokens = jnp.tile(jnp.arange(n, dtype=jnp.int32), 2)
    src = jnp.zeros((cap,), jnp.int32).at[pos].set(tokens)
    tile_start = jnp.arange(cap // tm, dtype=jnp.int32) * tm
    tile_expert = jnp.minimum(jnp.searchsorted(ends, tile_start, side='right'), e - 1).astype(jnp.int32)
    n_used = (ends[-1:] // tm).astype(jnp.int32)

    xs = _from_words(_gather_rows(src, _to_words(xn), 512))
    ys = _ffn_expert(tile_expert, n_used, xs, w_gate, w_up, w_down, tm, tf)
    yg = _from_words(_gather_rows(pos.astype(jnp.int32), _to_words(ys), 512))
    return _combine(h, yg, route, 512)


def _row(v):
    return v.astype(F32).reshape(1, -1)


def _memory_block(h, mem, norm_x, norm_m, w_q, w_kv, q_gain, k_gain, w_o):
    k, v = _mem_kv(mem, _row(norm_m), w_kv.astype(BF16), _row(k_gain), MEM_HEADS)
    return _xattn(h, _row(norm_x), w_q.astype(BF16), _row(q_gain), k, v, w_o.astype(BF16),
                  MEM_HEADS, 512)


def _even_mixer(h, norm, hy_w_in, rw_mu, rw_w0, rw_w_up, rw_a0, rw_a_up, rw_g_up, rw_k_k, rw_k_a,
                rw_r_k, rw_ln_w, rw_ln_b, s5_lam_re, s5_lam_im, s5_log_dt, s5_b_re, s5_b_im,
                s5_c_re, s5_c_im, s5_d, s5_w_glu, s5_b_glu, hy_w_out):
    b, t, d = h.shape
    width = rw_w0.shape[0]
    n_lora = DECAY_LORA + ICLR_LORA + GATE_LORA
    rwkv_cols = 3 * width + n_lora
    n_r = 3 * width + LORA_PAD
    pad = LORA_PAD - n_lora
    w_in = jnp.concatenate([hy_w_in[:, :rwkv_cols], jnp.zeros((d, pad), hy_w_in.dtype),
                            hy_w_in[:, rwkv_cols:]], axis=1).astype(BF16)
    mu = jnp.concatenate([rw_mu.astype(F32), jnp.zeros((pad,), F32)]).reshape(1, -1)
    w_lora = jnp.zeros((LORA_PAD, 3 * width), F32)
    w_lora = w_lora.at[:DECAY_LORA, :width].set(rw_w_up.astype(F32))
    w_lora = w_lora.at[DECAY_LORA:DECAY_LORA + ICLR_LORA, width:2 * width].set(rw_a_up.astype(F32))
    w_lora = w_lora.at[DECAY_LORA + ICLR_LORA:n_lora, 2 * width:].set(rw_g_up.astype(F32))

    h2 = h.reshape(b * t, d)
    zr, u = _in_proj(h2, _row(norm), w_in, n_r, 512)
    parts = _rwkv_prep(zr.reshape(b, t, n_r), mu, w_lora.astype(BF16), _row(rw_w0), _row(rw_a0),
                       _row(rw_k_k), _row(rw_k_a), width, 512)
    ya = _rwkv_chunk(*parts, _row(rw_r_k), _row(rw_ln_w), _row(rw_ln_b), 2)

    bblk, cblk, lam_r, lam_i = _s5_tables(s5_lam_re, s5_lam_im, s5_log_dt, s5_b_re, s5_b_im,
                                          s5_c_re, s5_c_im)
    yb = _s5(u.reshape(b, t, -1), bblk, cblk, lam_r, lam_i, _row(s5_d))

    w_out = hy_w_out.astype(BF16)
    out = _mix_out(h2, ya.reshape(b * t, width), yb.reshape(b * t, -1), s5_w_glu.astype(BF16),
                   _row(s5_b_glu), w_out[:width], w_out[width:], 512)
    return out.reshape(b, t, d)


def _odd_mixer(h, norm, layer, w_qkv, q_gain, k_gain, lam_q1, lam_k1, lam_q2, lam_k2, sub_gain, w_o):
    b, t, d = h.shape
    heads = w_o.shape[0] // (2 * DIFF_HEAD)
    lambda_init = 0.8 - 0.6 * math.exp(-0.3 * layer)
    lam = (jnp.exp(jnp.sum(lam_q1.astype(F32) * lam_k1.astype(F32)))
           - jnp.exp(jnp.sum(lam_q2.astype(F32) * lam_k2.astype(F32))) + lambda_init).reshape(1)
    slopes = 2.0 ** (-8.0 * jnp.arange(1, heads + 1, dtype=F32) / heads)
    n_half = heads * 2 * DIFF_HEAD
    qk_gain = jnp.concatenate([jnp.tile(q_gain.astype(F32), n_half // DIFF_HEAD) * DIFF_HEAD ** -0.5,
                               jnp.tile(k_gain.astype(F32), n_half // DIFF_HEAD)]).reshape(1, -1)
    h2 = h.reshape(b * t, d)
    qk, v = _qkv(h2, _row(norm), w_qkv.astype(BF16), qk_gain, 512)
    o = _diff_attn(slopes, lam, qk.reshape(b, t, -1), v.reshape(b, t, -1), _row(sub_gain),
                   heads, 1.0 - lambda_init, 512)
    return _proj_residual(h2, o.reshape(b * t, -1), w_o.astype(BF16), 512).reshape(b, t, d)


def kernel(x, mem, norm_mix, norm_xattn, norm_mem, norm_ffn, xa_w_q, xa_w_kv, xa_q_gain, xa_k_gain, xa_w_o, hy_w_in, rw_mu, rw_w0, rw_w_up, rw_a0, rw_a_up, rw_g_up, rw_k_k, rw_k_a, rw_r_k, rw_ln_w, rw_ln_b, s5_lam_re, s5_lam_im, s5_log_dt, s5_b_re, s5_b_im, s5_c_re, s5_c_im, s5_d, s5_w_glu, s5_b_glu, hy_w_out, ff_w_gate, ff_w_up, ff_w_down, da_w_qkv, da_q_gain, da_k_gain, da_lam_q1, da_lam_k1, da_lam_q2, da_lam_k2, da_sub_gain, da_w_o, moe_w_router, moe_b_router, moe_w_gate, moe_w_up, moe_w_down):
    depth = norm_mix.shape[0]
    b, t, d = x.shape
    h = x
    for l in range(depth):
        i = l // 2
        if l % 2 == 0:
            h = _even_mixer(h, norm_mix[l], hy_w_in[i], rw_mu[i], rw_w0[i], rw_w_up[i], rw_a0[i],
                            rw_a_up[i], rw_g_up[i], rw_k_k[i], rw_k_a[i], rw_r_k[i], rw_ln_w[i],
                            rw_ln_b[i], s5_lam_re[i], s5_lam_im[i], s5_log_dt[i], s5_b_re[i],
                            s5_b_im[i], s5_c_re[i], s5_c_im[i], s5_d[i], s5_w_glu[i], s5_b_glu[i],
                            hy_w_out[i])
        else:
            h = _odd_mixer(h, norm_mix[l], l, da_w_qkv[i], da_q_gain[i], da_k_gain[i], da_lam_q1[i],
                           da_lam_k1[i], da_lam_q2[i], da_lam_k2[i], da_sub_gain[i], da_w_o[i])
        h = _memory_block(h, mem, norm_xattn[l], norm_mem[l], xa_w_q[l], xa_w_kv[l], xa_q_gain[l],
                          xa_k_gain[l], xa_w_o[l])
        h2 = h.reshape(b * t, d)
        if l % 2 == 0:
            h2 = _ffn_dense(h2, _row(norm_ffn[l]), ff_w_gate[i].astype(BF16), ff_w_up[i].astype(BF16),
                            ff_w_down[i].astype(BF16), 512, 1408)
        else:
            h2 = _moe(h2, _row(norm_ffn[l]), moe_w_router[i], moe_b_router[i],
                      moe_w_gate[i].astype(BF16), moe_w_up[i].astype(BF16), moe_w_down[i].astype(BF16),
                      1024, 512)
        h = h2.reshape(b, t, d)
    return h
```

```python
import functools
import math

import jax
import jax.numpy as jnp
from jax import lax
from jax.experimental import pallas as pl
from jax.experimental.pallas import tpu as pltpu

F32 = jnp.float32
BF16 = jnp.bfloat16

NORM_EPS = 1e-6
NEG_INF = -1e30
ATTN_CHUNK = 64

RWKV_HEAD = 64
RWKV_CHUNK = 64
DECAY_LORA = 64
ICLR_LORA = 64
GATE_LORA = 160
LORA_PAD = 384
RWKV_GN_EPS = 1e-5 * RWKV_HEAD

S5_GROUP = 16
S5_STATE = 64
S5_TILE_GROUPS = 8

DIFF_HEAD = 64
MEM_HEADS = 4
N_EXPERTS = 8
LANES = 128
ROUTER_LANES = LANES

V7X_VMEM_BYTES = 64 * 1024 * 1024


def _params(semantics, vmem_mib):
    assert vmem_mib * 1024 * 1024 < V7X_VMEM_BYTES
    return pltpu.CompilerParams(dimension_semantics=semantics,
                                vmem_limit_bytes=vmem_mib * 1024 * 1024)


def _rms(x, gain):
    return x * lax.rsqrt(jnp.mean(x * x, axis=-1, keepdims=True) + NORM_EPS) * gain


def _dot(a, b):
    return jnp.dot(a, b, preferred_element_type=F32)


def _dot_nt(a, b):
    return lax.dot_general(a, b, (((1,), (1,)), ((), ())), preferred_element_type=F32)


def _dot_tn(a, b):
    return lax.dot_general(a, b, (((0,), (0,)), ((), ())), preferred_element_type=F32)


def _group_ones(width, group):
    r = lax.broadcasted_iota(jnp.int32, (width, width), 0) // group
    c = lax.broadcasted_iota(jnp.int32, (width, width), 1) // group
    return jnp.where(r == c, 1.0, 0.0).astype(BF16)


def _group_sum(x, ones):
    hi = x.astype(BF16)
    lo = (x - hi.astype(F32)).astype(BF16)
    return _dot(hi, ones) + _dot(lo, ones)


def _in_proj_kernel(x_ref, g_ref, w_ref, zr_ref, u_ref, *, n_r):
    xn = _rms(x_ref[...], g_ref[...]).astype(BF16)
    z = _dot(xn, w_ref[...])
    zr_ref[...] = z[:, :n_r]
    u_ref[...] = z[:, n_r:].astype(u_ref.dtype)


def _in_proj(x, gain, w, n_r, tm):
    n, d = x.shape
    c = w.shape[1]
    tm = min(tm, n)
    return pl.pallas_call(
        functools.partial(_in_proj_kernel, n_r=n_r),
        grid=(n // tm,),
        in_specs=[pl.BlockSpec((tm, d), lambda i: (i, 0)),
                  pl.BlockSpec((1, d), lambda i: (0, 0)),
                  pl.BlockSpec((d, c), lambda i: (0, 0))],
        out_specs=[pl.BlockSpec((tm, n_r), lambda i: (i, 0)),
                   pl.BlockSpec((tm, c - n_r), lambda i: (i, 0))],
        out_shape=[jax.ShapeDtypeStruct((n, n_r), F32),
                   jax.ShapeDtypeStruct((n, c - n_r), BF16)],
        compiler_params=_params(("parallel",), 48),
        name="in_proj",
    )(x, gain, w)


def _rwkv_prep_kernel(z_ref, mu_ref, wl_ref, w0_ref, a0_ref, kk_ref, ka_ref,
                      r_ref, lw_ref, k_ref, v_ref, kn_ref, a_ref, g_ref, carry_ref, *, width):
    i = pl.program_id(1)

    @pl.when(i == 0)
    def _():
        carry_ref[...] = jnp.zeros_like(carry_ref)

    z = z_ref[0]
    rows = lax.broadcasted_iota(jnp.int32, z.shape, 0)
    prev = jnp.where(rows == 0, carry_ref[...], pltpu.roll(z, 1, axis=0))
    carry_ref[...] = z[z.shape[0] - 1:, :]
    zs = z + (prev - z) * mu_ref[...]

    w = width
    r = zs[:, :w]
    k = zs[:, w:2 * w]
    v = zs[:, 2 * w:3 * w]
    lo = zs[:, 3 * w:]
    lane = lax.broadcasted_iota(jnp.int32, lo.shape, 1)
    lo = jnp.where(lane < DECAY_LORA, jnp.tanh(lo),
                   jnp.where(lane < DECAY_LORA + ICLR_LORA, lo, jax.nn.sigmoid(lo)))
    m = _dot(lo.astype(BF16), wl_ref[...])
    w_log = -jax.nn.softplus(-(w0_ref[...] + m[:, :w])) - 0.5
    a = jax.nn.sigmoid(a0_ref[...] + m[:, w:2 * w])
    kn = k * kk_ref[...]
    ss = _group_sum(kn * kn, _group_ones(w, RWKV_HEAD))
    kn = kn * lax.rsqrt(jnp.maximum(ss, 1e-24))

    r_ref[0] = r
    lw_ref[0] = -jnp.exp(w_log)
    k_ref[0] = k * (1.0 + (a - 1.0) * ka_ref[...])
    v_ref[0] = v
    kn_ref[0] = kn
    a_ref[0] = a
    g_ref[0] = m[:, 2 * w:]


def _rwkv_prep(zr, mu, wl, w0, a0, k_k, k_a, width, tm):
    b, t, c = zr.shape
    tm = min(tm, t)
    row = lambda n: pl.BlockSpec((1, n), lambda bi, i: (0, 0))
    out = pl.BlockSpec((1, tm, width), lambda bi, i: (bi, i, 0))
    return pl.pallas_call(
        functools.partial(_rwkv_prep_kernel, width=width),
        grid=(b, t // tm),
        in_specs=[pl.BlockSpec((1, tm, c), lambda bi, i: (bi, i, 0)),
                  row(c),
                  pl.BlockSpec(wl.shape, lambda bi, i: (0, 0)),
                  row(width), row(width), row(width), row(width)],
        out_specs=[out] * 7,
        out_shape=[jax.ShapeDtypeStruct((b, t, width), F32)] * 7,
        scratch_shapes=[pltpu.VMEM((1, c), F32)],
        compiler_params=_params(("parallel", "arbitrary"), 48),
        name="rwkv_prep",
    )(zr, mu, wl, w0, a0, k_k, k_a)


def _rwkv_chunk_kernel(r_ref, lw_ref, k_ref, v_ref, kn_ref, a_ref, g_ref,
                       rk_ref, lnw_ref, lnb_ref, o_ref, st_ref, *, heads, chunks, L):
    @pl.when(pl.program_id(2) == 0)
    def _():
        st_ref[...] = jnp.zeros_like(st_ref)

    n = RWKV_HEAD
    ri = lax.broadcasted_iota(jnp.int32, (L, L), 0)
    ci = lax.broadcasted_iota(jnp.int32, (L, L), 1)
    tri = jnp.where(ci <= ri, 1.0, 0.0).astype(BF16)
    strict = ci < ri
    incl = ci <= ri
    eye = jnp.where(ci == ri, 1.0, 0.0).astype(F32)
    rk_all, lnw_all, lnb_all = rk_ref[...], lnw_ref[...], lnb_ref[...]
    inv_steps = max(1, int(math.ceil(math.log2(L))) - 1)

    items = [(c, h) for c in range(chunks) for h in range(heads)]
    fs = []
    for c, h in items:
        rows = slice(c * L, (c + 1) * L)
        s = slice(h * n, (h + 1) * n)
        r, k, v = r_ref[0, rows, s], k_ref[0, rows, s], v_ref[0, rows, s]
        kn, a, lw = kn_ref[0, rows, s], a_ref[0, rows, s], lw_ref[0, rows, s]
        hi = lw.astype(BF16)
        lo = (lw - hi.astype(F32)).astype(BF16)
        fs.append(dict(r=r, k=k, v=v, lw=lw, alpha=-kn, beta=kn * a, v_b=v.astype(BF16),
                       cum=_dot(tri, hi) + _dot(tri, lo)))
    for f in fs:
        cum, lw = f["cum"], f["lw"]
        cum_ex = cum - lw
        c_mid = cum[L // 2 - 1:L // 2, :]
        c_end = cum[L - 1:L, :]
        e_inv = jnp.exp(c_mid - cum)
        e_end = jnp.exp(c_end - cum)
        lhs = jnp.concatenate([f["alpha"] * jnp.exp(cum_ex - c_mid), f["r"] * jnp.exp(cum - c_mid)],
                              axis=0).astype(BF16)
        f["m_b"] = _dot_nt(lhs, (f["beta"] * e_inv).astype(BF16))
        f["m_k"] = _dot_nt(lhs, (f["k"] * e_inv).astype(BF16))
        f["abs_lhs"] = jnp.concatenate([f["alpha"] * jnp.exp(cum_ex), f["r"] * jnp.exp(cum)],
                                       axis=0).astype(BF16)
        f["b_end"] = (f["beta"] * e_end).astype(BF16)
        f["k_end"] = (f["k"] * e_end).astype(BF16)
        f["decay"] = jnp.exp(c_end)
    for f in fs:
        m_ab = jnp.where(strict, f["m_b"][:L], 0.0)
        f["m_rb"] = jnp.where(incl, f["m_b"][L:], 0.0).astype(BF16)
        f["ak_v"] = _dot(jnp.where(strict, f["m_k"][:L], 0.0).astype(BF16), f["v_b"])
        f["rk_v"] = _dot(jnp.where(incl, f["m_k"][L:], 0.0).astype(BF16), f["v_b"])
        f["inv"] = eye + m_ab
        f["p"] = m_ab
    for _ in range(inv_steps):
        for f in fs:
            pb = f["p"].astype(BF16)
            f["p"] = _dot(pb, pb)
        for f in fs:
            f["inv"] = f["inv"] + _dot(f["inv"].astype(BF16), f["p"].astype(BF16))
    for f in fs:
        f["inv"] = f["inv"].astype(BF16)

    st = [st_ref[h] for h in range(heads)]
    for c in range(chunks):
        cf = fs[c * heads:(c + 1) * heads]
        from_state = [_dot_nt(f["abs_lhs"], st[h].astype(BF16)) for h, f in enumerate(cf)]
        u_b = [_dot(f["inv"], (from_state[h][:L] + f["ak_v"]).astype(BF16)).astype(BF16)
               for h, f in enumerate(cf)]
        st = [st[h] * f["decay"] + _dot_tn(u_b[h], f["b_end"]) + _dot_tn(f["v_b"], f["k_end"])
              for h, f in enumerate(cf)]
        for h, f in enumerate(cf):
            rows = slice(c * L, (c + 1) * L)
            s = slice(h * n, (h + 1) * n)
            o = from_state[h][L:] + _dot(f["m_rb"], u_b[h]) + f["rk_v"]
            mean = jnp.mean(o, axis=-1, keepdims=True)
            d = o - mean
            var = jnp.mean(d * d, axis=-1, keepdims=True)
            o = d * lax.rsqrt(var + RWKV_GN_EPS) * lnw_all[:, s] + lnb_all[:, s]
            bonus = jnp.sum(f["r"] * f["k"] * rk_all[:, s], axis=-1, keepdims=True) * f["v"]
            o_ref[0, rows, s] = ((o + bonus) * g_ref[0, rows, s]).astype(o_ref.dtype)
    for h in range(heads):
        st_ref[h] = st[h]


def _rwkv_chunk(r, lw, k, v, kn, a, g, r_k, ln_w, ln_b, heads_per_step, chunks_per_step):
    b, t, width = r.shape
    L = min(RWKV_CHUNK, t)
    chunks = min(chunks_per_step, t // L)
    lanes = heads_per_step * RWKV_HEAD
    seq = pl.BlockSpec((1, chunks * L, lanes), lambda bi, hi, ci: (bi, ci, hi))
    row = pl.BlockSpec((1, lanes), lambda bi, hi, ci: (0, hi))
    return pl.pallas_call(
        functools.partial(_rwkv_chunk_kernel, heads=heads_per_step, chunks=chunks, L=L),
        grid=(b, width // lanes, t // (chunks * L)),
        in_specs=[seq] * 7 + [row] * 3,
        out_specs=seq,
        out_shape=jax.ShapeDtypeStruct((b, t, width), BF16),
        scratch_shapes=[pltpu.VMEM((heads_per_step, RWKV_HEAD, RWKV_HEAD), F32)],
        compiler_params=_params(("parallel", "parallel", "arbitrary"), 32),
        name="rwkv_chunk",
    )(r, lw, k, v, kn, a, g, r_k, ln_w, ln_b)


S5_ROWS = 8


def _gelu_tanh(x):
    return 0.5 * x * (1.0 + jnp.tanh(math.sqrt(2.0 / math.pi) * (x + 0.044715 * (x * x * x))))


def _s5_kernel(u_ref, bb_ref, cb_ref, lr_ref, li_ref, d_ref, y_ref, x_ref, *, unroll, y_rows):
    half = lr_ref.shape[-1]
    t = u_ref.shape[1]
    nb = half // LANES
    lr = lr_ref[0]
    li = li_ref[0]

    for cblk in range(2 * nb):
        x_ref[cblk] = _dot(u_ref[0], bb_ref[0, :, cblk * LANES:(cblk + 1) * LANES])

    pows = [(lr, li)]
    for _ in range(S5_ROWS - 1):
        qr, qi = pows[-1]
        pows.append((qr * lr - qi * li, qr * li + qi * lr))
    pw_r = jnp.concatenate([q[0] for q in pows], axis=0)
    pw_i = jnp.concatenate([q[1] for q in pows], axis=0)
    row = lax.broadcasted_iota(jnp.int32, (S5_ROWS, half), 0)
    shifts = []
    k = 1
    while k < S5_ROWS:
        shifts.append((k, jnp.where(row >= k, pows[k - 1][0], 0.0), jnp.where(row >= k, pows[k - 1][1], 0.0)))
        k *= 2

    def tile_scan(ti, carry):
        rows = pl.ds(pl.multiple_of(ti * S5_ROWS, S5_ROWS), S5_ROWS)
        out = []
        for cblk in range(nb):
            lanes = slice(cblk * LANES, (cblk + 1) * LANES)
            xr = x_ref[cblk, rows, :]
            xi = x_ref[nb + cblk, rows, :]
            for k, ar, ai in shifts:
                sr = pltpu.roll(xr, k, axis=0)
                si = pltpu.roll(xi, k, axis=0)
                xr, xi = (xr + (ar[:, lanes] * sr - ai[:, lanes] * si),
                          xi + (ar[:, lanes] * si + ai[:, lanes] * sr))
            cr, ci = carry[cblk]
            xr, xi = (xr + (pw_r[:, lanes] * cr - pw_i[:, lanes] * ci),
                      xi + (pw_r[:, lanes] * ci + pw_i[:, lanes] * cr))
            x_ref[cblk, rows, :] = xr
            x_ref[nb + cblk, rows, :] = xi
            out.append((jnp.broadcast_to(xr[S5_ROWS - 1:, :], (S5_ROWS, LANES)),
                        jnp.broadcast_to(xi[S5_ROWS - 1:, :], (S5_ROWS, LANES))))
        return tuple(out)

    zero = jnp.zeros((S5_ROWS, LANES), F32)
    lax.fori_loop(0, t // S5_ROWS, tile_scan, tuple((zero, zero) for _ in range(nb)), unroll=unroll)

    d = d_ref[...]
    cb = cb_ref[0]
    for r0 in range(0, t, y_rows):
        rows = slice(r0, r0 + y_rows)
        x = jnp.concatenate([x_ref[cblk, rows, :] for cblk in range(2 * nb)], axis=-1).astype(BF16)
        y = _dot(x, cb) + d * u_ref[0, rows, :].astype(F32)
        y_ref[0, rows, :] = _gelu_tanh(y).astype(y_ref.dtype)


def _s5(u, bblk, cblk, lam_r, lam_i, d_skip):
    b, t, width = u.shape
    lanes = S5_TILE_GROUPS * S5_GROUP
    half = S5_TILE_GROUPS * S5_STATE
    tile = lambda shape: pl.BlockSpec((1,) + shape, lambda bi, ci: (ci, 0, 0))
    return pl.pallas_call(
        functools.partial(_s5_kernel, unroll=min(4, t // S5_ROWS), y_rows=min(512, t)),
        grid=(b, width // lanes),
        in_specs=[pl.BlockSpec((1, t, lanes), lambda bi, ci: (bi, 0, ci)),
                  tile((lanes, 2 * half)), tile((2 * half, lanes)),
                  tile((1, half)), tile((1, half)),
                  pl.BlockSpec((1, lanes), lambda bi, ci: (0, ci))],
        out_specs=pl.BlockSpec((1, t, lanes), lambda bi, ci: (bi, 0, ci)),
        out_shape=jax.ShapeDtypeStruct((b, t, width), BF16),
        scratch_shapes=[pltpu.VMEM((2 * half // LANES, t, LANES), F32)],
        compiler_params=_params(("parallel", "parallel"), 48),
        name="s5",
    )(u, bblk, cblk, lam_r, lam_i, d_skip)


def _s5_tables(lam_re, lam_im, log_dt, b_re, b_im, c_re, c_im):
    g, p = lam_re.shape
    tiles = g // S5_TILE_GROUPS
    lre, lim = lam_re.astype(F32), lam_im.astype(F32)
    dt = jnp.exp(log_dt.astype(F32))[:, None]
    mag = jnp.exp(lre * dt)
    bar_re = mag * jnp.cos(lim * dt)
    bar_im = mag * jnp.sin(lim * dt)
    den = lre * lre + lim * lim
    coef_re = ((bar_re - 1.0) * lre + bar_im * lim) / den
    coef_im = (bar_im * lre - (bar_re - 1.0) * lim) / den
    bre, bim = b_re.astype(F32), b_im.astype(F32)
    bbar_re = coef_re[..., None] * bre - coef_im[..., None] * bim
    bbar_im = coef_re[..., None] * bim + coef_im[..., None] * bre
    eye = jnp.eye(S5_TILE_GROUPS, dtype=F32)

    def pack_b(m):
        m = m.reshape(tiles, S5_TILE_GROUPS, p, S5_GROUP)
        return jnp.einsum('tgpi,gh->tgihp', m, eye).reshape(tiles, S5_TILE_GROUPS * S5_GROUP,
                                                            S5_TILE_GROUPS * p)

    def pack_c(m):
        m = m.reshape(tiles, S5_TILE_GROUPS, S5_GROUP, p)
        return jnp.einsum('tgip,gh->tgphi', m, eye).reshape(tiles, S5_TILE_GROUPS * p,
                                                            S5_TILE_GROUPS * S5_GROUP)

    bblk = jnp.concatenate([pack_b(bbar_re), pack_b(bbar_im)], axis=-1).astype(BF16)
    cblk = jnp.concatenate([pack_c(c_re.astype(F32)), -pack_c(c_im.astype(F32))], axis=1).astype(BF16)
    lam_r = bar_re.reshape(tiles, 1, S5_TILE_GROUPS * p)
    lam_i = bar_im.reshape(tiles, 1, S5_TILE_GROUPS * p)
    return bblk, cblk, lam_r, lam_i


def _mix_out_kernel(h_ref, ya_ref, yb_ref, wglu_ref, bglu_ref, woa_ref, wob_ref, o_ref):
    yb = yb_ref[...]
    gate = jax.nn.sigmoid(_dot(yb, wglu_ref[...]) + bglu_ref[...])
    yb = (yb.astype(F32) * gate).astype(BF16)
    o_ref[...] = h_ref[...] + _dot(ya_ref[...], woa_ref[...]) + _dot(yb, wob_ref[...])


def _mix_out(h, ya, yb, w_glu, b_glu, wo_a, wo_b, tm):
    n, d = h.shape
    tm = min(tm, n)
    wa, wb = ya.shape[1], yb.shape[1]
    full = lambda a: pl.BlockSpec(a.shape, lambda i: (0, 0))
    return pl.pallas_call(
        _mix_out_kernel,
        grid=(n // tm,),
        in_specs=[pl.BlockSpec((tm, d), lambda i: (i, 0)),
                  pl.BlockSpec((tm, wa), lambda i: (i, 0)),
                  pl.BlockSpec((tm, wb), lambda i: (i, 0)),
                  full(w_glu), full(b_glu), full(wo_a), full(wo_b)],
        out_specs=pl.BlockSpec((tm, d), lambda i: (i, 0)),
        out_shape=jax.ShapeDtypeStruct((n, d), F32),
        compiler_params=_params(("parallel",), 32),
        name="mix_out",
    )(h, ya, yb, w_glu, b_glu, wo_a, wo_b)


def _mem_kv_kernel(m_ref, g_ref, w_ref, kg_ref, k_ref, v_ref, *, heads):
    mn = _rms(m_ref[0], g_ref[...]).astype(BF16)
    kv = _dot(mn, w_ref[...])
    d = kv.shape[1] // 2
    hd = d // heads
    for h in range(heads):
        s = slice(h * hd, (h + 1) * hd)
        k_ref[0, :, s] = _rms(kv[:, s], kg_ref[...]).astype(k_ref.dtype)
    v_ref[0] = kv[:, d:].astype(v_ref.dtype)


def _mem_kv(mem, gain, w_kv, k_gain, heads):
    b, m, d = mem.shape
    blk = pl.BlockSpec((1, m, d), lambda bi: (bi, 0, 0))
    return pl.pallas_call(
        functools.partial(_mem_kv_kernel, heads=heads),
        grid=(b,),
        in_specs=[blk, pl.BlockSpec((1, d), lambda bi: (0, 0)),
                  pl.BlockSpec(w_kv.shape, lambda bi: (0, 0)),
                  pl.BlockSpec(k_gain.shape, lambda bi: (0, 0))],
        out_specs=[blk, blk],
        out_shape=[jax.ShapeDtypeStruct((b, m, d), BF16)] * 2,
        compiler_params=_params(("parallel",), 32),
        name="mem_kv",
    )(mem, gain, w_kv, k_gain)


def _xattn_kernel(h_ref, g_ref, wq_ref, qg_ref, k_ref, v_ref, wo_ref, o_ref, *, heads):
    h = h_ref[0]
    hn = _rms(h, g_ref[...]).astype(BF16)
    q = _dot(hn, wq_ref[...])
    hd = q.shape[1] // heads
    k = k_ref[0]
    v = v_ref[0]
    outs = []
    for i in range(heads):
        s = slice(i * hd, (i + 1) * hd)
        qh = (_rms(q[:, s], qg_ref[...]) * hd ** -0.5).astype(BF16)
        sc = _dot_nt(qh, k[:, s])
        sc = sc - jnp.max(sc, axis=-1, keepdims=True)
        p = jnp.exp(sc)
        p = p / jnp.sum(p, axis=-1, keepdims=True)
        outs.append(_dot(p.astype(BF16), v[:, s]).astype(BF16))
    o = jnp.concatenate(outs, axis=-1)
    o_ref[0] = h + _dot(o, wo_ref[...])


def _xattn(h, gain, w_q, q_gain, k, v, w_o, heads, tm):
    b, t, d = h.shape
    m = k.shape[1]
    tm = min(tm, t)
    full = lambda a: pl.BlockSpec(a.shape, lambda bi, i: (0,) * a.ndim)
    tok = pl.BlockSpec((1, tm, d), lambda bi, i: (bi, i, 0))
    mem = pl.BlockSpec((1, m, d), lambda bi, i: (bi, 0, 0))
    return pl.pallas_call(
        functools.partial(_xattn_kernel, heads=heads),
        grid=(b, t // tm),
        in_specs=[tok, full(gain), full(w_q), full(q_gain), mem, mem, full(w_o)],
        out_specs=tok,
        out_shape=jax.ShapeDtypeStruct((b, t, d), F32),
        compiler_params=_params(("parallel", "parallel"), 40),
        name="xattn",
    )(h, gain, w_q, q_gain, k, v, w_o)


def _ffn_dense_kernel(h_ref, g_ref, wg_ref, wu_ref, wd_ref, o_ref, xn_ref, acc_ref):
    j = pl.program_id(1)

    @pl.when(j == 0)
    def _():
        xn_ref[...] = _rms(h_ref[...], g_ref[...]).astype(BF16)
        acc_ref[...] = jnp.zeros_like(acc_ref)

    xn = xn_ref[...]
    act = (jax.nn.silu(_dot(xn, wg_ref[...])) * _dot(xn, wu_ref[...])).astype(BF16)
    acc_ref[...] += _dot(act, wd_ref[...])

    @pl.when(j == pl.num_programs(1) - 1)
    def _():
        o_ref[...] = h_ref[...] + acc_ref[...]


def _ffn_dense(h, gain, w_gate, w_up, w_down, tm, tf):
    n, d = h.shape
    f = w_gate.shape[1]
    tm, tf = min(tm, n), min(tf, f)
    return pl.pallas_call(
        _ffn_dense_kernel,
        grid=(n // tm, f // tf),
        in_specs=[pl.BlockSpec((tm, d), lambda i, j: (i, 0)),
                  pl.BlockSpec((1, d), lambda i, j: (0, 0)),
                  pl.BlockSpec((d, tf), lambda i, j: (0, j)),
                  pl.BlockSpec((d, tf), lambda i, j: (0, j)),
                  pl.BlockSpec((tf, d), lambda i, j: (j, 0))],
        out_specs=pl.BlockSpec((tm, d), lambda i, j: (i, 0)),
        out_shape=jax.ShapeDtypeStruct((n, d), F32),
        scratch_shapes=[pltpu.VMEM((tm, d), BF16), pltpu.VMEM((tm, d), F32)],
        compiler_params=_params(("parallel", "arbitrary"), 48),
        name="ffn_dense",
    )(h, gain, w_gate, w_up, w_down)


def _pack_words(x):
    bits = lax.bitcast_convert_type(x.astype(BF16).astype(F32), jnp.uint32)
    half = x.shape[1] // 2
    return (bits[:, :half] >> 16) | (bits[:, half:] & jnp.uint32(0xFFFF0000))


def _unpack_words(w):
    lo = lax.bitcast_convert_type(w << 16, F32)
    hi = lax.bitcast_convert_type(w & jnp.uint32(0xFFFF0000), F32)
    return jnp.concatenate([lo, hi], axis=-1)


def _ffn_expert_kernel(te_ref, nu_ref, x_ref, wg_ref, wu_ref, wd_ref, o_ref, xb_ref, acc_ref):
    i = pl.program_id(0)
    j = pl.program_id(1)

    @pl.when(i < nu_ref[0])
    def _():
        @pl.when(j == 0)
        def _():
            xb_ref[...] = _unpack_words(x_ref[...]).astype(BF16)
            acc_ref[...] = jnp.zeros_like(acc_ref)

        x = xb_ref[...]
        act = (jax.nn.silu(_dot(x, wg_ref[0])) * _dot(x, wu_ref[0])).astype(BF16)
        acc_ref[...] += _dot(act, wd_ref[0])

    @pl.when(j == pl.num_programs(1) - 1)
    def _():
        o_ref[...] = _pack_words(acc_ref[...])


def _ffn_expert(tile_expert, n_used, xs, w_gate, w_up, w_down, tm, tf):
    p, words = xs.shape
    d = 2 * words
    f = w_gate.shape[2]
    tf = min(tf, f)
    nj = f // tf

    def col(i, j, te, nu):
        return (te[i], 0, jnp.where(i < nu[0], j, nj - 1))

    def rowb(i, j, te, nu):
        return (te[i], jnp.where(i < nu[0], j, nj - 1), 0)

    grid_spec = pltpu.PrefetchScalarGridSpec(
        num_scalar_prefetch=2,
        grid=(p // tm, nj),
        in_specs=[pl.BlockSpec((tm, words), lambda i, j, te, nu: (i, 0)),
                  pl.BlockSpec((1, d, tf), col),
                  pl.BlockSpec((1, d, tf), col),
                  pl.BlockSpec((1, tf, d), rowb)],
        out_specs=pl.BlockSpec((tm, words), lambda i, j, te, nu: (i, 0)),
        scratch_shapes=[pltpu.VMEM((tm, d), BF16), pltpu.VMEM((tm, d), F32)],
    )
    return pl.pallas_call(
        _ffn_expert_kernel,
        grid_spec=grid_spec,
        out_shape=jax.ShapeDtypeStruct((p, words), jnp.uint32),
        compiler_params=_params(("arbitrary", "arbitrary"), 48),
        name="ffn_expert",
    )(tile_expert, n_used, xs, w_gate, w_up, w_down)


def _qkv_kernel(x_ref, g_ref, w_ref, qkg_ref, qk_ref, v_ref, *, n_qk):
    xn = _rms(x_ref[...], g_ref[...]).astype(BF16)
    z = _dot(xn, w_ref[...])
    blk = 256
    ones = _group_ones(blk, DIFF_HEAD)
    for c in range(n_qk // blk):
        s = slice(c * blk, (c + 1) * blk)
        zc = z[:, s]
        ms = _group_sum(zc * zc, ones) * (1.0 / DIFF_HEAD)
        qk_ref[:, s] = (zc * lax.rsqrt(ms + NORM_EPS) * qkg_ref[:, s]).astype(qk_ref.dtype)
    v_ref[...] = z[:, n_qk:].astype(v_ref.dtype)


def _qkv(x, gain, w, qk_gain, tm):
    n, d = x.shape
    c = w.shape[1]
    n_qk = qk_gain.shape[1]
    tm = min(tm, n)
    return pl.pallas_call(
        functools.partial(_qkv_kernel, n_qk=n_qk),
        grid=(n // tm,),
        in_specs=[pl.BlockSpec((tm, d), lambda i: (i, 0)),
                  pl.BlockSpec((1, d), lambda i: (0, 0)),
                  pl.BlockSpec((d, c), lambda i: (0, 0)),
                  pl.BlockSpec((1, n_qk), lambda i: (0, 0))],
        out_specs=[pl.BlockSpec((tm, n_qk), lambda i: (i, 0)),
                   pl.BlockSpec((tm, c - n_qk), lambda i: (i, 0))],
        out_shape=[jax.ShapeDtypeStruct((n, n_qk), BF16),
                   jax.ShapeDtypeStruct((n, c - n_qk), BF16)],
        compiler_params=_params(("parallel",), 48),
        name="qkv",
    )(x, gain, w, qk_gain)


def _diff_attn_kernel(ti_ref, tj_ref, slope_ref, lam_ref, q_ref, kc_ref, kn_ref, v_ref, sg_ref, o_ref,
                      qs_ref, s0_ref, s1_ref, p_ref, m_ref, l_ref, acc_ref, dg_ref, *, out_scale, rb):
    h = pl.program_id(1)
    step = pl.program_id(2)
    i = ti_ref[step]
    j = tj_ref[step]
    tq = q_ref.shape[1]
    tk = kc_ref.shape[1]
    nl = tk // LANES
    slope = slope_ref[h]
    slot = lax.rem(j, 2)

    @pl.when(j == 0)
    def _():
        q = q_ref[0]
        lane = lax.broadcasted_iota(jnp.int32, q.shape, 1)
        zero = jnp.zeros_like(q)
        qs_ref[0:tq, :] = jnp.where(lane < DIFF_HEAD, q, zero)
        qs_ref[tq:2 * tq, :] = jnp.where(lane < DIFF_HEAD, zero, q)
        m_ref[...] = jnp.full_like(m_ref, NEG_INF)
        l_ref[...] = jnp.zeros_like(l_ref)
        acc_ref[...] = jnp.zeros_like(acc_ref)
        s0_ref[...] = _dot_nt(qs_ref[...], kc_ref[0])

    @pl.when(step == 0)
    def _():
        r = lax.broadcasted_iota(jnp.int32, (tq, tk), 0)
        c = lax.broadcasted_iota(jnp.int32, (tq, tk), 1)
        ahead = jnp.maximum(c - r, 0).astype(F32) * (-2.0 * slope)
        dg_ref[...] = jnp.where(c // ATTN_CHUNK <= r // ATTN_CHUNK, ahead, NEG_INF)

    col = lax.broadcasted_iota(jnp.int32, (1, tk), 1)
    key_bias = slope * (col + (j - i) * tq).astype(F32)

    def update(diag, cur_ref):
        for r0 in range(0, 2 * tq, rb):
            rows = slice(r0, r0 + rb)
            s = cur_ref[rows, :] + key_bias
            if diag:
                d0 = r0 % tq
                s = s + dg_ref[d0:d0 + rb, :]
            blk = s[:, 0:LANES]
            for c in range(1, nl):
                blk = jnp.maximum(blk, s[:, c * LANES:(c + 1) * LANES])
            m_prev = m_ref[rows, :]
            m_new = jnp.maximum(m_prev, jnp.max(blk, axis=-1, keepdims=True))
            corr = jnp.exp2(m_prev - m_new)
            p = jnp.exp2(s - pltpu.repeat(m_new, nl, axis=1))
            part = p[:, 0:LANES]
            for c in range(1, nl):
                part = part + p[:, c * LANES:(c + 1) * LANES]
            l_ref[rows, :] = corr * l_ref[rows, :] + part
            acc_ref[rows, :] = corr * acc_ref[rows, :]
            m_ref[rows, :] = m_new
            p_ref[rows, :] = p.astype(BF16)
        v = v_ref[0]
        for half in range(2):
            rows = slice(half * tq, (half + 1) * tq)
            acc_ref[rows, :] += _dot(p_ref[rows, :], v)

    for parity, (cur_ref, nxt_ref) in enumerate(((s0_ref, s1_ref), (s1_ref, s0_ref))):
        @pl.when(jnp.logical_and(j < i, slot == parity))
        def _(cur_ref=cur_ref, nxt_ref=nxt_ref):
            nxt_ref[...] = _dot_nt(qs_ref[...], kn_ref[0])
            update(False, cur_ref)

        @pl.when(jnp.logical_and(j == i, slot == parity))
        def _(cur_ref=cur_ref):
            update(True, cur_ref)

    @pl.when(j == i)
    def _():
        o = acc_ref[...] / jnp.sum(l_ref[...], axis=-1, keepdims=True)
        o = o[:tq] - lam_ref[0] * o[tq:]
        o = o * lax.rsqrt(jnp.mean(o * o, axis=-1, keepdims=True) + NORM_EPS) * sg_ref[...]
        o_ref[0] = (o * out_scale).astype(o_ref.dtype)


def _diff_attn(slopes, lam, qk, v, sub_gain, heads, out_scale, tq, rb):
    b, t, _ = v.shape
    hw = 2 * DIFF_HEAD
    tq = min(tq, t)
    rb = min(rb, tq)
    nt = t // tq
    pairs = [(i, j) for i in range(nt) for j in range(i + 1)]
    ti = jnp.asarray([p[0] for p in pairs], jnp.int32)
    tj = jnp.asarray([p[1] for p in pairs], jnp.int32)
    smem = pl.BlockSpec(memory_space=pltpu.SMEM)
    grid_spec = pltpu.PrefetchScalarGridSpec(
        num_scalar_prefetch=2,
        grid=(b, heads, len(pairs)),
        in_specs=[smem, smem,
                  pl.BlockSpec((1, tq, hw), lambda bi, h, s, ti, tj: (bi, ti[s], h)),
                  pl.BlockSpec((1, tq, hw), lambda bi, h, s, ti, tj: (bi, tj[s], heads + h)),
                  pl.BlockSpec((1, tq, hw),
                               lambda bi, h, s, ti, tj: (bi, jnp.minimum(tj[s] + 1, ti[s]), heads + h)),
                  pl.BlockSpec((1, tq, hw), lambda bi, h, s, ti, tj: (bi, tj[s], h)),
                  pl.BlockSpec((1, hw), lambda bi, h, s, ti, tj: (0, 0))],
        out_specs=pl.BlockSpec((1, tq, hw), lambda bi, h, s, ti, tj: (bi, ti[s], h)),
        scratch_shapes=[pltpu.VMEM((2 * tq, hw), BF16),
                        pltpu.VMEM((2 * tq, tq), F32),
                        pltpu.VMEM((2 * tq, tq), F32),
                        pltpu.VMEM((2 * tq, tq), BF16),
                        pltpu.VMEM((2 * tq, LANES), F32),
                        pltpu.VMEM((2 * tq, LANES), F32),
                        pltpu.VMEM((2 * tq, hw), F32),
                        pltpu.VMEM((tq, tq), F32)],
    )
    return pl.pallas_call(
        functools.partial(_diff_attn_kernel, out_scale=out_scale, rb=rb),
        grid_spec=grid_spec,
        out_shape=jax.ShapeDtypeStruct((b, t, heads * hw), BF16),
        compiler_params=_params(("arbitrary", "arbitrary", "arbitrary"), 32),
        name="diff_attn",
    )(ti, tj, slopes, lam, qk, qk, qk, v, sub_gain)


def _proj_residual_kernel(h_ref, a_ref, w_ref, o_ref):
    o_ref[...] = h_ref[...] + _dot(a_ref[...], w_ref[...])


def _proj_residual(h, a, w, tm):
    n, d = h.shape
    tm = min(tm, n)
    return pl.pallas_call(
        _proj_residual_kernel,
        grid=(n // tm,),
        in_specs=[pl.BlockSpec((tm, d), lambda i: (i, 0)),
                  pl.BlockSpec((tm, a.shape[1]), lambda i: (i, 0)),
                  pl.BlockSpec(w.shape, lambda i: (0, 0))],
        out_specs=pl.BlockSpec((tm, d), lambda i: (i, 0)),
        out_shape=jax.ShapeDtypeStruct((n, d), F32),
        compiler_params=_params(("parallel",), 32),
        name="proj_residual",
    )(h, a, w)


def _router_kernel(h_ref, g_ref, w_ref, b_ref, xn_ref, r_ref):
    xn = _rms(h_ref[...], g_ref[...])
    xn_ref[...] = _pack_words(xn)
    logits = jnp.dot(xn, w_ref[...], preferred_element_type=F32,
                     precision=lax.Precision.HIGHEST) + b_ref[...]
    lane = lax.broadcasted_iota(jnp.int32, logits.shape, 1).astype(F32)
    big = float(ROUTER_LANES)
    l1 = jnp.max(logits, axis=-1, keepdims=True)
    i1 = jnp.min(jnp.where(logits == l1, lane, big), axis=-1, keepdims=True)
    rest = jnp.where(lane == i1, NEG_INF, logits)
    l2 = jnp.max(rest, axis=-1, keepdims=True)
    i2 = jnp.min(jnp.where(rest == l2, lane, big), axis=-1, keepdims=True)
    e = jnp.exp(l2 - l1)
    w1 = 1.0 / (1.0 + e)
    w2 = e / (1.0 + e)
    r_ref[...] = jnp.where(lane == 0.0, i1, jnp.where(lane == 1.0, i2,
                           jnp.where(lane == 2.0, w1, jnp.where(lane == 3.0, w2, 0.0))))


def _router(h, gain, w_pad, b_pad, tm):
    n, d = h.shape
    tm = min(tm, n)
    return pl.pallas_call(
        _router_kernel,
        grid=(n // tm,),
        in_specs=[pl.BlockSpec((tm, d), lambda i: (i, 0)),
                  pl.BlockSpec((1, d), lambda i: (0, 0)),
                  pl.BlockSpec(w_pad.shape, lambda i: (0, 0)),
                  pl.BlockSpec(b_pad.shape, lambda i: (0, 0))],
        out_specs=[pl.BlockSpec((tm, d // 2), lambda i: (i, 0)),
                   pl.BlockSpec((tm, ROUTER_LANES), lambda i: (i, 0))],
        out_shape=[jax.ShapeDtypeStruct((n, d // 2), jnp.uint32),
                   jax.ShapeDtypeStruct((n, ROUTER_LANES), F32)],
        compiler_params=_params(("parallel",), 32),
        name="router",
    )(h, gain, w_pad, b_pad)


def _gather_copy(idx_ref, x_hbm, o_ref, sem, base, r):
    return pltpu.make_async_copy(x_hbm.at[pl.ds(idx_ref[base + r], 1)], o_ref.at[pl.ds(r, 1)], sem)


def _gather_kernel(idx_ref, x_hbm, o_ref, sem):
    rows = o_ref.shape[0]
    base = pl.program_id(0) * rows

    def start(r, c):
        _gather_copy(idx_ref, x_hbm, o_ref, sem, base, r).start()
        return c

    def wait(r, c):
        _gather_copy(idx_ref, x_hbm, o_ref, sem, base, r).wait()
        return c

    lax.fori_loop(0, rows, start, 0)
    lax.fori_loop(0, rows, wait, 0)


def _gather_rows(idx, x, tg):
    p = idx.shape[0]
    w = x.shape[1]
    tg = min(tg, p)
    grid_spec = pltpu.PrefetchScalarGridSpec(
        num_scalar_prefetch=1,
        grid=(p // tg,),
        in_specs=[pl.BlockSpec(memory_space=pl.ANY)],
        out_specs=pl.BlockSpec((tg, w), lambda i, idx_ref: (i, 0)),
        scratch_shapes=[pltpu.SemaphoreType.DMA(())],
    )
    return pl.pallas_call(
        _gather_kernel,
        grid_spec=grid_spec,
        out_shape=jax.ShapeDtypeStruct((p, w), x.dtype),
        compiler_params=_params(("arbitrary",), 32),
        name="gather_rows",
    )(idx, x)


def _combine_kernel(h_ref, y0_ref, y1_ref, r_ref, o_ref):
    r = r_ref[...]
    w0 = r[:, 2:3]
    w1 = r[:, 3:4]
    o_ref[...] = h_ref[...] + w0 * _unpack_words(y0_ref[...]) + w1 * _unpack_words(y1_ref[...])


def _combine(h, yg, route, tm):
    n, d = h.shape
    tm = min(tm, n)
    nb = n // tm
    return pl.pallas_call(
        _combine_kernel,
        grid=(nb,),
        in_specs=[pl.BlockSpec((tm, d), lambda i: (i, 0)),
                  pl.BlockSpec((tm, d // 2), lambda i: (i, 0)),
                  pl.BlockSpec((tm, d // 2), lambda i: (i + nb, 0)),
                  pl.BlockSpec((tm, ROUTER_LANES), lambda i: (i, 0))],
        out_specs=pl.BlockSpec((tm, d), lambda i: (i, 0)),
        out_shape=jax.ShapeDtypeStruct((n, d), F32),
        compiler_params=_params(("parallel",), 32),
        name="combine",
    )(h, yg, yg, route)


def _moe(h, gain, w_router, b_router, w_gate, w_up, w_down, tm, tf):
    n, d = h.shape
    e = w_gate.shape[0]
    w_pad = jnp.zeros((d, ROUTER_LANES), F32).at[:, :e].set(w_router.astype(F32))
    b_pad = jnp.full((1, ROUTER_LANES), NEG_INF, F32).at[0, :e].set(b_router.astype(F32))
    xn, route = _router(h, gain, w_pad, b_pad, 512)

    tm = min(tm, n)
    experts = jnp.concatenate([route[:, 0], route[:, 1]]).astype(jnp.int32)
    onehot = (experts[:, None] == jnp.arange(e, dtype=jnp.int32)[None, :]).astype(jnp.int32)
    counts = jnp.sum(onehot, axis=0)
    padded = ((counts + tm - 1) // tm) * tm
    ends = jnp.cumsum(padded)
    starts = ends - padded
    pos = jnp.sum(onehot * (jnp.cumsum(onehot, axis=0) - onehot + starts[None, :]), axis=1)
    cap = 2 * n + e * tm
    tokens = jnp.tile(jnp.arange(n, dtype=jnp.int32), 2)
    src = jnp.zeros((cap,), jnp.int32).at[pos].set(tokens)
    tile_start = jnp.arange(cap // tm, dtype=jnp.int32) * tm
    tile_expert = jnp.minimum(jnp.searchsorted(ends, tile_start, side='right'), e - 1).astype(jnp.int32)
    n_used = (ends[-1:] // tm).astype(jnp.int32)

    xs = _gather_rows(src, xn, 512)
    ys = _ffn_expert(tile_expert, n_used, xs, w_gate, w_up, w_down, tm, tf)
    yg = _gather_rows(pos.astype(jnp.int32), ys, 512)
    return _combine(h, yg, route, 512)


def _row(v):
    return v.astype(F32).reshape(1, -1)


def _memory_block(h, mem, norm_x, norm_m, w_q, w_kv, q_gain, k_gain, w_o):
    k, v = _mem_kv(mem, _row(norm_m), w_kv.astype(BF16), _row(k_gain), MEM_HEADS)
    return _xattn(h, _row(norm_x), w_q.astype(BF16), _row(q_gain), k, v, w_o.astype(BF16),
                  MEM_HEADS, 512)


def _even_mixer(h, norm, hy_w_in, rw_mu, rw_w0, rw_w_up, rw_a0, rw_a_up, rw_g_up, rw_k_k, rw_k_a,
                rw_r_k, rw_ln_w, rw_ln_b, s5_lam_re, s5_lam_im, s5_log_dt, s5_b_re, s5_b_im,
                s5_c_re, s5_c_im, s5_d, s5_w_glu, s5_b_glu, hy_w_out):
    b, t, d = h.shape
    width = rw_w0.shape[0]
    n_lora = DECAY_LORA + ICLR_LORA + GATE_LORA
    rwkv_cols = 3 * width + n_lora
    n_r = 3 * width + LORA_PAD
    pad = LORA_PAD - n_lora
    w_in = jnp.concatenate([hy_w_in[:, :rwkv_cols], jnp.zeros((d, pad), hy_w_in.dtype),
                            hy_w_in[:, rwkv_cols:]], axis=1).astype(BF16)
    mu = jnp.concatenate([rw_mu.astype(F32), jnp.zeros((pad,), F32)]).reshape(1, -1)
    w_lora = jnp.zeros((LORA_PAD, 3 * width), F32)
    w_lora = w_lora.at[:DECAY_LORA, :width].set(rw_w_up.astype(F32))
    w_lora = w_lora.at[DECAY_LORA:DECAY_LORA + ICLR_LORA, width:2 * width].set(rw_a_up.astype(F32))
    w_lora = w_lora.at[DECAY_LORA + ICLR_LORA:n_lora, 2 * width:].set(rw_g_up.astype(F32))

    h2 = h.reshape(b * t, d)
    zr, u = _in_proj(h2, _row(norm), w_in, n_r, 512)
    parts = _rwkv_prep(zr.reshape(b, t, n_r), mu, w_lora.astype(BF16), _row(rw_w0), _row(rw_a0),
                       _row(rw_k_k), _row(rw_k_a), width, 512)
    ya = _rwkv_chunk(*parts, _row(rw_r_k), _row(rw_ln_w), _row(rw_ln_b), width // RWKV_HEAD, 2)

    bblk, cblk, lam_r, lam_i = _s5_tables(s5_lam_re, s5_lam_im, s5_log_dt, s5_b_re, s5_b_im,
                                          s5_c_re, s5_c_im)
    yb = _s5(u.reshape(b, t, -1), bblk, cblk, lam_r, lam_i, _row(s5_d))

    w_out = hy_w_out.astype(BF16)
    out = _mix_out(h2, ya.reshape(b * t, width), yb.reshape(b * t, -1), s5_w_glu.astype(BF16),
                   _row(s5_b_glu), w_out[:width], w_out[width:], 512)
    return out.reshape(b, t, d)


def _odd_mixer(h, norm, layer, w_qkv, q_gain, k_gain, lam_q1, lam_k1, lam_q2, lam_k2, sub_gain, w_o):
    b, t, d = h.shape
    heads = w_o.shape[0] // (2 * DIFF_HEAD)
    lambda_init = 0.8 - 0.6 * math.exp(-0.3 * layer)
    lam = (jnp.exp(jnp.sum(lam_q1.astype(F32) * lam_k1.astype(F32)))
           - jnp.exp(jnp.sum(lam_q2.astype(F32) * lam_k2.astype(F32))) + lambda_init).reshape(1)
    log2e = math.log2(math.e)
    slopes = 2.0 ** (-8.0 * jnp.arange(1, heads + 1, dtype=F32) / heads) * log2e
    n_half = heads * 2 * DIFF_HEAD
    qk_gain = jnp.concatenate([jnp.tile(q_gain.astype(F32), n_half // DIFF_HEAD) * (DIFF_HEAD ** -0.5 * log2e),
                               jnp.tile(k_gain.astype(F32), n_half // DIFF_HEAD)]).reshape(1, -1)
    h2 = h.reshape(b * t, d)
    qk, v = _qkv(h2, _row(norm), w_qkv.astype(BF16), qk_gain, 512)
    o = _diff_attn(slopes, lam, qk.reshape(b, t, -1), v.reshape(b, t, -1), _row(sub_gain),
                   heads, 1.0 - lambda_init, 512, 128)
    return _proj_residual(h2, o.reshape(b * t, -1), w_o.astype(BF16), 512).reshape(b, t, d)


def kernel(x, mem, norm_mix, norm_xattn, norm_mem, norm_ffn, xa_w_q, xa_w_kv, xa_q_gain, xa_k_gain, xa_w_o, hy_w_in, rw_mu, rw_w0, rw_w_up, rw_a0, rw_a_up, rw_g_up, rw_k_k, rw_k_a, rw_r_k, rw_ln_w, rw_ln_b, s5_lam_re, s5_lam_im, s5_log_dt, s5_b_re, s5_b_im, s5_c_re, s5_c_im, s5_d, s5_w_glu, s5_b_glu, hy_w_out, ff_w_gate, ff_w_up, ff_w_down, da_w_qkv, da_q_gain, da_k_gain, da_lam_q1, da_lam_k1, da_lam_q2, da_lam_k2, da_sub_gain, da_w_o, moe_w_router, moe_b_router, moe_w_gate, moe_w_up, moe_w_down):
    depth = norm_mix.shape[0]
    b, t, d = x.shape
    h = x
    for l in range(depth):
        i = l // 2
        if l % 2 == 0:
            h = _even_mixer(h, norm_mix[l], hy_w_in[i], rw_mu[i], rw_w0[i], rw_w_up[i], rw_a0[i],
                            rw_a_up[i], rw_g_up[i], rw_k_k[i], rw_k_a[i], rw_r_k[i], rw_ln_w[i],
                            rw_ln_b[i], s5_lam_re[i], s5_lam_im[i], s5_log_dt[i], s5_b_re[i],
                            s5_b_im[i], s5_c_re[i], s5_c_im[i], s5_d[i], s5_w_glu[i], s5_b_glu[i],
                            hy_w_out[i])
        else:
            h = _odd_mixer(h, norm_mix[l], l, da_w_qkv[i], da_q_gain[i], da_k_gain[i], da_lam_q1[i],
                           da_lam_k1[i], da_lam_q2[i], da_lam_k2[i], da_sub_gain[i], da_w_o[i])
        h = _memory_block(h, mem, norm_xattn[l], norm_mem[l], xa_w_q[l], xa_w_kv[l], xa_q_gain[l],
                          xa_k_gain[l], xa_w_o[l])
        h2 = h.reshape(b * t, d)
        if l % 2 == 0:
            h2 = _ffn_dense(h2, _row(norm_ffn[l]), ff_w_gate[i].astype(BF16), ff_w_up[i].astype(BF16),
                            ff_w_down[i].astype(BF16), 512, 1408)
        else:
            h2 = _moe(h2, _row(norm_ffn[l]), moe_w_router[i], moe_b_router[i],
                      moe_w_gate[i].astype(BF16), moe_w_up[i].astype(BF16), moe_w_down[i].astype(BF16),
                      1024, 512)
        h = h2.reshape(b, t, d)
    return h
```

```python
import functools
import math

import jax
import jax.numpy as jnp
from jax import lax
from jax.experimental import pallas as pl
from jax.experimental.pallas import tpu as pltpu

F32 = jnp.float32
BF16 = jnp.bfloat16

NORM_EPS = 1e-6
NEG_INF = -1e30
ATTN_CHUNK = 64

RWKV_HEAD = 64
RWKV_CHUNK = 64
DECAY_LORA = 64
ICLR_LORA = 64
GATE_LORA = 160
LORA_PAD = 384
RWKV_GN_EPS = 1e-5 * RWKV_HEAD

S5_GROUP = 16
S5_STATE = 64
S5_TILE_GROUPS = 8

DIFF_HEAD = 64
MEM_HEADS = 4
N_EXPERTS = 8
LANES = 128
ROUTER_LANES = LANES

V7X_VMEM_BYTES = 64 * 1024 * 1024


def _params(semantics, vmem_mib):
    assert vmem_mib * 1024 * 1024 < V7X_VMEM_BYTES
    return pltpu.CompilerParams(dimension_semantics=semantics,
                                vmem_limit_bytes=vmem_mib * 1024 * 1024)


def _rms(x, gain):
    return x * lax.rsqrt(jnp.mean(x * x, axis=-1, keepdims=True) + NORM_EPS) * gain


def _dot(a, b):
    return jnp.dot(a, b, preferred_element_type=F32)


def _dot_nt(a, b):
    return lax.dot_general(a, b, (((1,), (1,)), ((), ())), preferred_element_type=F32)


def _dot_tn(a, b):
    return lax.dot_general(a, b, (((0,), (0,)), ((), ())), preferred_element_type=F32)


def _group_ones(width, group):
    r = lax.broadcasted_iota(jnp.int32, (width, width), 0) // group
    c = lax.broadcasted_iota(jnp.int32, (width, width), 1) // group
    return jnp.where(r == c, 1.0, 0.0).astype(BF16)


def _group_sum(x, ones):
    hi = x.astype(BF16)
    lo = (x - hi.astype(F32)).astype(BF16)
    return _dot(hi, ones) + _dot(lo, ones)


def _in_proj_kernel(x_ref, g_ref, w_ref, zr_ref, u_ref, *, n_r):
    xn = _rms(x_ref[...], g_ref[...]).astype(BF16)
    z = _dot(xn, w_ref[...])
    zr_ref[...] = z[:, :n_r]
    u_ref[...] = z[:, n_r:].astype(u_ref.dtype)


def _in_proj(x, gain, w, n_r, tm):
    n, d = x.shape
    c = w.shape[1]
    tm = min(tm, n)
    return pl.pallas_call(
        functools.partial(_in_proj_kernel, n_r=n_r),
        grid=(n // tm,),
        in_specs=[pl.BlockSpec((tm, d), lambda i: (i, 0)),
                  pl.BlockSpec((1, d), lambda i: (0, 0)),
                  pl.BlockSpec((d, c), lambda i: (0, 0))],
        out_specs=[pl.BlockSpec((tm, n_r), lambda i: (i, 0)),
                   pl.BlockSpec((tm, c - n_r), lambda i: (i, 0))],
        out_shape=[jax.ShapeDtypeStruct((n, n_r), F32),
                   jax.ShapeDtypeStruct((n, c - n_r), BF16)],
        compiler_params=_params(("parallel",), 48),
        name="in_proj",
    )(x, gain, w)


def _rwkv_prep_kernel(z_ref, mu_ref, wl_ref, w0_ref, a0_ref, kk_ref, ka_ref,
                      r_ref, lw_ref, k_ref, v_ref, kn_ref, a_ref, g_ref, carry_ref, *, width):
    i = pl.program_id(1)

    @pl.when(i == 0)
    def _():
        carry_ref[...] = jnp.zeros_like(carry_ref)

    z = z_ref[0]
    rows = lax.broadcasted_iota(jnp.int32, z.shape, 0)
    prev = jnp.where(rows == 0, carry_ref[...], pltpu.roll(z, 1, axis=0))
    carry_ref[...] = z[z.shape[0] - 1:, :]
    zs = z + (prev - z) * mu_ref[...]

    w = width
    r = zs[:, :w]
    k = zs[:, w:2 * w]
    v = zs[:, 2 * w:3 * w]
    lo = zs[:, 3 * w:]
    lane = lax.broadcasted_iota(jnp.int32, lo.shape, 1)
    lo = jnp.where(lane < DECAY_LORA, jnp.tanh(lo),
                   jnp.where(lane < DECAY_LORA + ICLR_LORA, lo, jax.nn.sigmoid(lo)))
    m = _dot(lo.astype(BF16), wl_ref[...])
    w_log = -jax.nn.softplus(-(w0_ref[...] + m[:, :w])) - 0.5
    a = jax.nn.sigmoid(a0_ref[...] + m[:, w:2 * w])
    kn = k * kk_ref[...]
    ss = _group_sum(kn * kn, _group_ones(w, RWKV_HEAD))
    kn = kn * lax.rsqrt(jnp.maximum(ss, 1e-24))

    r_ref[0] = r
    lw_ref[0] = -jnp.exp(w_log)
    k_ref[0] = k * (1.0 + (a - 1.0) * ka_ref[...])
    v_ref[0] = v
    kn_ref[0] = kn
    a_ref[0] = a
    g_ref[0] = m[:, 2 * w:]


def _rwkv_prep(zr, mu, wl, w0, a0, k_k, k_a, width, tm):
    b, t, c = zr.shape
    tm = min(tm, t)
    row = lambda n: pl.BlockSpec((1, n), lambda bi, i: (0, 0))
    out = pl.BlockSpec((1, tm, width), lambda bi, i: (bi, i, 0))
    return pl.pallas_call(
        functools.partial(_rwkv_prep_kernel, width=width),
        grid=(b, t // tm),
        in_specs=[pl.BlockSpec((1, tm, c), lambda bi, i: (bi, i, 0)),
                  row(c),
                  pl.BlockSpec(wl.shape, lambda bi, i: (0, 0)),
                  row(width), row(width), row(width), row(width)],
        out_specs=[out] * 7,
        out_shape=[jax.ShapeDtypeStruct((b, t, width), F32)] * 7,
        scratch_shapes=[pltpu.VMEM((1, c), F32)],
        compiler_params=_params(("parallel", "arbitrary"), 48),
        name="rwkv_prep",
    )(zr, mu, wl, w0, a0, k_k, k_a)


def _rwkv_chunk_kernel(r_ref, lw_ref, k_ref, v_ref, kn_ref, a_ref, g_ref,
                       rk_ref, lnw_ref, lnb_ref, o_ref, st_ref, *, heads, chunks, L):
    @pl.when(pl.program_id(2) == 0)
    def _():
        st_ref[...] = jnp.zeros_like(st_ref)

    n = RWKV_HEAD
    ri = lax.broadcasted_iota(jnp.int32, (L, L), 0)
    ci = lax.broadcasted_iota(jnp.int32, (L, L), 1)
    tri = jnp.where(ci <= ri, 1.0, 0.0).astype(BF16)
    strict = ci < ri
    incl = ci <= ri
    eye = jnp.where(ci == ri, 1.0, 0.0).astype(F32)
    rk_all, lnw_all, lnb_all = rk_ref[...], lnw_ref[...], lnb_ref[...]
    inv_steps = max(1, int(math.ceil(math.log2(L))) - 1)

    items = [(c, h) for c in range(chunks) for h in range(heads)]
    fs = []
    for c, h in items:
        rows = slice(c * L, (c + 1) * L)
        s = slice(h * n, (h + 1) * n)
        r, k, v = r_ref[0, rows, s], k_ref[0, rows, s], v_ref[0, rows, s]
        kn, a, lw = kn_ref[0, rows, s], a_ref[0, rows, s], lw_ref[0, rows, s]
        hi = lw.astype(BF16)
        lo = (lw - hi.astype(F32)).astype(BF16)
        fs.append(dict(r=r, k=k, v=v, lw=lw, alpha=-kn, beta=kn * a, v_b=v.astype(BF16),
                       cum=_dot(tri, hi) + _dot(tri, lo)))
    for f in fs:
        cum, lw = f["cum"], f["lw"]
        cum_ex = cum - lw
        c_mid = cum[L // 2 - 1:L // 2, :]
        c_end = cum[L - 1:L, :]
        e_inv = jnp.exp(c_mid - cum)
        e_end = jnp.exp(c_end - cum)
        lhs = jnp.concatenate([f["alpha"] * jnp.exp(cum_ex - c_mid), f["r"] * jnp.exp(cum - c_mid)],
                              axis=0).astype(BF16)
        f["m_b"] = _dot_nt(lhs, (f["beta"] * e_inv).astype(BF16))
        f["m_k"] = _dot_nt(lhs, (f["k"] * e_inv).astype(BF16))
        f["abs_lhs"] = jnp.concatenate([f["alpha"] * jnp.exp(cum_ex), f["r"] * jnp.exp(cum)],
                                       axis=0).astype(BF16)
        f["b_end"] = (f["beta"] * e_end).astype(BF16)
        f["k_end"] = (f["k"] * e_end).astype(BF16)
        f["decay"] = jnp.exp(c_end)
    for f in fs:
        m_ab = jnp.where(strict, f["m_b"][:L], 0.0)
        f["m_rb"] = jnp.where(incl, f["m_b"][L:], 0.0).astype(BF16)
        f["ak_v"] = _dot(jnp.where(strict, f["m_k"][:L], 0.0).astype(BF16), f["v_b"])
        f["rk_v"] = _dot(jnp.where(incl, f["m_k"][L:], 0.0).astype(BF16), f["v_b"])
        f["inv"] = eye + m_ab
        f["p"] = m_ab
    for _ in range(inv_steps):
        for f in fs:
            pb = f["p"].astype(BF16)
            f["p"] = _dot(pb, pb)
        for f in fs:
            f["inv"] = f["inv"] + _dot(f["inv"].astype(BF16), f["p"].astype(BF16))
    for f in fs:
        f["inv"] = f["inv"].astype(BF16)

    st = [st_ref[h] for h in range(heads)]
    for c in range(chunks):
        cf = fs[c * heads:(c + 1) * heads]
        from_state = [_dot_nt(f["abs_lhs"], st[h].astype(BF16)) for h, f in enumerate(cf)]
        u_b = [_dot(f["inv"], (from_state[h][:L] + f["ak_v"]).astype(BF16)).astype(BF16)
               for h, f in enumerate(cf)]
        st = [st[h] * f["decay"] + _dot_tn(u_b[h], f["b_end"]) + _dot_tn(f["v_b"], f["k_end"])
              for h, f in enumerate(cf)]
        for h, f in enumerate(cf):
            rows = slice(c * L, (c + 1) * L)
            s = slice(h * n, (h + 1) * n)
            o = from_state[h][L:] + _dot(f["m_rb"], u_b[h]) + f["rk_v"]
            mean = jnp.mean(o, axis=-1, keepdims=True)
            d = o - mean
            var = jnp.mean(d * d, axis=-1, keepdims=True)
            o = d * lax.rsqrt(var + RWKV_GN_EPS) * lnw_all[:, s] + lnb_all[:, s]
            bonus = jnp.sum(f["r"] * f["k"] * rk_all[:, s], axis=-1, keepdims=True) * f["v"]
            o_ref[0, rows, s] = ((o + bonus) * g_ref[0, rows, s]).astype(o_ref.dtype)
    for h in range(heads):
        st_ref[h] = st[h]


def _rwkv_chunk(r, lw, k, v, kn, a, g, r_k, ln_w, ln_b, heads_per_step, chunks_per_step):
    b, t, width = r.shape
    L = min(RWKV_CHUNK, t)
    chunks = min(chunks_per_step, t // L)
    lanes = heads_per_step * RWKV_HEAD
    seq = pl.BlockSpec((1, chunks * L, lanes), lambda bi, hi, ci: (bi, ci, hi))
    row = pl.BlockSpec((1, lanes), lambda bi, hi, ci: (0, hi))
    return pl.pallas_call(
        functools.partial(_rwkv_chunk_kernel, heads=heads_per_step, chunks=chunks, L=L),
        grid=(b, width // lanes, t // (chunks * L)),
        in_specs=[seq] * 7 + [row] * 3,
        out_specs=seq,
        out_shape=jax.ShapeDtypeStruct((b, t, width), BF16),
        scratch_shapes=[pltpu.VMEM((heads_per_step, RWKV_HEAD, RWKV_HEAD), F32)],
        compiler_params=_params(("parallel", "parallel", "arbitrary"), 32),
        name="rwkv_chunk",
    )(r, lw, k, v, kn, a, g, r_k, ln_w, ln_b)


S5_ROWS = 8


def _gelu_tanh(x):
    return 0.5 * x * (1.0 + jnp.tanh(math.sqrt(2.0 / math.pi) * (x + 0.044715 * (x * x * x))))


def _s5_kernel(u_ref, bb_ref, cb_ref, lr_ref, li_ref, d_ref, y_ref, x_ref, *, unroll, y_rows):
    half = lr_ref.shape[-1]
    t = u_ref.shape[1]
    nb = half // LANES
    lr = lr_ref[0]
    li = li_ref[0]

    for cblk in range(2 * nb):
        x_ref[cblk] = _dot(u_ref[0], bb_ref[0, :, cblk * LANES:(cblk + 1) * LANES])

    pows = [(lr, li)]
    for _ in range(S5_ROWS - 1):
        qr, qi = pows[-1]
        pows.append((qr * lr - qi * li, qr * li + qi * lr))
    pw_r = jnp.concatenate([q[0] for q in pows], axis=0)
    pw_i = jnp.concatenate([q[1] for q in pows], axis=0)
    row = lax.broadcasted_iota(jnp.int32, (S5_ROWS, half), 0)
    shifts = []
    k = 1
    while k < S5_ROWS:
        shifts.append((k, jnp.where(row >= k, pows[k - 1][0], 0.0), jnp.where(row >= k, pows[k - 1][1], 0.0)))
        k *= 2

    def tile_scan(ti, carry):
        rows = pl.ds(pl.multiple_of(ti * S5_ROWS, S5_ROWS), S5_ROWS)
        out = []
        for cblk in range(nb):
            lanes = slice(cblk * LANES, (cblk + 1) * LANES)
            xr = x_ref[cblk, rows, :]
            xi = x_ref[nb + cblk, rows, :]
            for k, ar, ai in shifts:
                sr = pltpu.roll(xr, k, axis=0)
                si = pltpu.roll(xi, k, axis=0)
                xr, xi = (xr + (ar[:, lanes] * sr - ai[:, lanes] * si),
                          xi + (ar[:, lanes] * si + ai[:, lanes] * sr))
            cr, ci = carry[cblk]
            xr, xi = (xr + (pw_r[:, lanes] * cr - pw_i[:, lanes] * ci),
                      xi + (pw_r[:, lanes] * ci + pw_i[:, lanes] * cr))
            x_ref[cblk, rows, :] = xr
            x_ref[nb + cblk, rows, :] = xi
            out.append((jnp.broadcast_to(xr[S5_ROWS - 1:, :], (S5_ROWS, LANES)),
                        jnp.broadcast_to(xi[S5_ROWS - 1:, :], (S5_ROWS, LANES))))
        return tuple(out)

    zero = jnp.zeros((S5_ROWS, LANES), F32)
    lax.fori_loop(0, t // S5_ROWS, tile_scan, tuple((zero, zero) for _ in range(nb)), unroll=unroll)

    d = d_ref[...]
    cb = cb_ref[0]
    for r0 in range(0, t, y_rows):
        rows = slice(r0, r0 + y_rows)
        x = jnp.concatenate([x_ref[cblk, rows, :] for cblk in range(2 * nb)], axis=-1).astype(BF16)
        y = _dot(x, cb) + d * u_ref[0, rows, :].astype(F32)
        y_ref[0, rows, :] = _gelu_tanh(y).astype(y_ref.dtype)


def _s5(u, bblk, cblk, lam_r, lam_i, d_skip):
    b, t, width = u.shape
    lanes = S5_TILE_GROUPS * S5_GROUP
    half = S5_TILE_GROUPS * S5_STATE
    tile = lambda shape: pl.BlockSpec((1,) + shape, lambda bi, ci: (ci, 0, 0))
    return pl.pallas_call(
        functools.partial(_s5_kernel, unroll=min(4, t // S5_ROWS), y_rows=min(512, t)),
        grid=(b, width // lanes),
        in_specs=[pl.BlockSpec((1, t, lanes), lambda bi, ci: (bi, 0, ci)),
                  tile((lanes, 2 * half)), tile((2 * half, lanes)),
                  tile((1, half)), tile((1, half)),
                  pl.BlockSpec((1, lanes), lambda bi, ci: (0, ci))],
        out_specs=pl.BlockSpec((1, t, lanes), lambda bi, ci: (bi, 0, ci)),
        out_shape=jax.ShapeDtypeStruct((b, t, width), BF16),
        scratch_shapes=[pltpu.VMEM((2 * half // LANES, t, LANES), F32)],
        compiler_params=_params(("parallel", "parallel"), 48),
        name="s5",
    )(u, bblk, cblk, lam_r, lam_i, d_skip)


def _s5_tables(lam_re, lam_im, log_dt, b_re, b_im, c_re, c_im):
    g, p = lam_re.shape
    tiles = g // S5_TILE_GROUPS
    lre, lim = lam_re.astype(F32), lam_im.astype(F32)
    dt = jnp.exp(log_dt.astype(F32))[:, None]
    mag = jnp.exp(lre * dt)
    bar_re = mag * jnp.cos(lim * dt)
    bar_im = mag * jnp.sin(lim * dt)
    den = lre * lre + lim * lim
    coef_re = ((bar_re - 1.0) * lre + bar_im * lim) / den
    coef_im = (bar_im * lre - (bar_re - 1.0) * lim) / den
    bre, bim = b_re.astype(F32), b_im.astype(F32)
    bbar_re = coef_re[..., None] * bre - coef_im[..., None] * bim
    bbar_im = coef_re[..., None] * bim + coef_im[..., None] * bre
    eye = jnp.eye(S5_TILE_GROUPS, dtype=F32)

    def pack_b(m):
        m = m.reshape(tiles, S5_TILE_GROUPS, p, S5_GROUP)
        return jnp.einsum('tgpi,gh->tgihp', m, eye).reshape(tiles, S5_TILE_GROUPS * S5_GROUP,
                                                            S5_TILE_GROUPS * p)

    def pack_c(m):
        m = m.reshape(tiles, S5_TILE_GROUPS, S5_GROUP, p)
        return jnp.einsum('tgip,gh->tgphi', m, eye).reshape(tiles, S5_TILE_GROUPS * p,
                                                            S5_TILE_GROUPS * S5_GROUP)

    bblk = jnp.concatenate([pack_b(bbar_re), pack_b(bbar_im)], axis=-1).astype(BF16)
    cblk = jnp.concatenate([pack_c(c_re.astype(F32)), -pack_c(c_im.astype(F32))], axis=1).astype(BF16)
    lam_r = bar_re.reshape(tiles, 1, S5_TILE_GROUPS * p)
    lam_i = bar_im.reshape(tiles, 1, S5_TILE_GROUPS * p)
    return bblk, cblk, lam_r, lam_i


def _mix_out_kernel(h_ref, ya_ref, yb_ref, wglu_ref, bglu_ref, woa_ref, wob_ref, o_ref):
    yb = yb_ref[...]
    gate = jax.nn.sigmoid(_dot(yb, wglu_ref[...]) + bglu_ref[...])
    yb = (yb.astype(F32) * gate).astype(BF16)
    o_ref[...] = h_ref[...] + _dot(ya_ref[...], woa_ref[...]) + _dot(yb, wob_ref[...])


def _mix_out(h, ya, yb, w_glu, b_glu, wo_a, wo_b, tm):
    n, d = h.shape
    tm = min(tm, n)
    wa, wb = ya.shape[1], yb.shape[1]
    full = lambda a: pl.BlockSpec(a.shape, lambda i: (0, 0))
    return pl.pallas_call(
        _mix_out_kernel,
        grid=(n // tm,),
        in_specs=[pl.BlockSpec((tm, d), lambda i: (i, 0)),
                  pl.BlockSpec((tm, wa), lambda i: (i, 0)),
                  pl.BlockSpec((tm, wb), lambda i: (i, 0)),
                  full(w_glu), full(b_glu), full(wo_a), full(wo_b)],
        out_specs=pl.BlockSpec((tm, d), lambda i: (i, 0)),
        out_shape=jax.ShapeDtypeStruct((n, d), F32),
        compiler_params=_params(("parallel",), 32),
        name="mix_out",
    )(h, ya, yb, w_glu, b_glu, wo_a, wo_b)


def _mem_kv_kernel(m_ref, g_ref, w_ref, kg_ref, k_ref, v_ref, *, heads):
    mn = _rms(m_ref[0], g_ref[...]).astype(BF16)
    kv = _dot(mn, w_ref[...])
    d = kv.shape[1] // 2
    hd = d // heads
    for h in range(heads):
        s = slice(h * hd, (h + 1) * hd)
        k_ref[0, :, s] = _rms(kv[:, s], kg_ref[...]).astype(k_ref.dtype)
    v_ref[0] = kv[:, d:].astype(v_ref.dtype)


def _mem_kv(mem, gain, w_kv, k_gain, heads):
    b, m, d = mem.shape
    blk = pl.BlockSpec((1, m, d), lambda bi: (bi, 0, 0))
    return pl.pallas_call(
        functools.partial(_mem_kv_kernel, heads=heads),
        grid=(b,),
        in_specs=[blk, pl.BlockSpec((1, d), lambda bi: (0, 0)),
                  pl.BlockSpec(w_kv.shape, lambda bi: (0, 0)),
                  pl.BlockSpec(k_gain.shape, lambda bi: (0, 0))],
        out_specs=[blk, blk],
        out_shape=[jax.ShapeDtypeStruct((b, m, d), BF16)] * 2,
        compiler_params=_params(("parallel",), 32),
        name="mem_kv",
    )(mem, gain, w_kv, k_gain)


def _xattn_kernel(h_ref, g_ref, wq_ref, qg_ref, k_ref, v_ref, wo_ref, o_ref, *, heads):
    h = h_ref[0]
    hn = _rms(h, g_ref[...]).astype(BF16)
    q = _dot(hn, wq_ref[...])
    hd = q.shape[1] // heads
    k = k_ref[0]
    v = v_ref[0]
    outs = []
    for i in range(heads):
        s = slice(i * hd, (i + 1) * hd)
        qh = (_rms(q[:, s], qg_ref[...]) * hd ** -0.5).astype(BF16)
        sc = _dot_nt(qh, k[:, s])
        sc = sc - jnp.max(sc, axis=-1, keepdims=True)
        p = jnp.exp(sc)
        p = p / jnp.sum(p, axis=-1, keepdims=True)
        outs.append(_dot(p.astype(BF16), v[:, s]).astype(BF16))
    o = jnp.concatenate(outs, axis=-1)
    o_ref[0] = h + _dot(o, wo_ref[...])


def _xattn(h, gain, w_q, q_gain, k, v, w_o, heads, tm):
    b, t, d = h.shape
    m = k.shape[1]
    tm = min(tm, t)
    full = lambda a: pl.BlockSpec(a.shape, lambda bi, i: (0,) * a.ndim)
    tok = pl.BlockSpec((1, tm, d), lambda bi, i: (bi, i, 0))
    mem = pl.BlockSpec((1, m, d), lambda bi, i: (bi, 0, 0))
    return pl.pallas_call(
        functools.partial(_xattn_kernel, heads=heads),
        grid=(b, t // tm),
        in_specs=[tok, full(gain), full(w_q), full(q_gain), mem, mem, full(w_o)],
        out_specs=tok,
        out_shape=jax.ShapeDtypeStruct((b, t, d), F32),
        compiler_params=_params(("parallel", "parallel"), 40),
        name="xattn",
    )(h, gain, w_q, q_gain, k, v, w_o)


def _ffn_dense_kernel(h_ref, g_ref, wg_ref, wu_ref, wd_ref, o_ref, xn_ref, acc_ref):
    j = pl.program_id(1)

    @pl.when(j == 0)
    def _():
        xn_ref[...] = _rms(h_ref[...], g_ref[...]).astype(BF16)
        acc_ref[...] = jnp.zeros_like(acc_ref)

    xn = xn_ref[...]
    act = (jax.nn.silu(_dot(xn, wg_ref[...])) * _dot(xn, wu_ref[...])).astype(BF16)
    acc_ref[...] += _dot(act, wd_ref[...])

    @pl.when(j == pl.num_programs(1) - 1)
    def _():
        o_ref[...] = h_ref[...] + acc_ref[...]


def _ffn_dense(h, gain, w_gate, w_up, w_down, tm, tf):
    n, d = h.shape
    f = w_gate.shape[1]
    tm, tf = min(tm, n), min(tf, f)
    return pl.pallas_call(
        _ffn_dense_kernel,
        grid=(n // tm, f // tf),
        in_specs=[pl.BlockSpec((tm, d), lambda i, j: (i, 0)),
                  pl.BlockSpec((1, d), lambda i, j: (0, 0)),
                  pl.BlockSpec((d, tf), lambda i, j: (0, j)),
                  pl.BlockSpec((d, tf), lambda i, j: (0, j)),
                  pl.BlockSpec((tf, d), lambda i, j: (j, 0))],
        out_specs=pl.BlockSpec((tm, d), lambda i, j: (i, 0)),
        out_shape=jax.ShapeDtypeStruct((n, d), F32),
        scratch_shapes=[pltpu.VMEM((tm, d), BF16), pltpu.VMEM((tm, d), F32)],
        compiler_params=_params(("parallel", "arbitrary"), 48),
        name="ffn_dense",
    )(h, gain, w_gate, w_up, w_down)


def _pack_words(x):
    bits = lax.bitcast_convert_type(x.astype(BF16).astype(F32), jnp.uint32)
    half = x.shape[1] // 2
    return (bits[:, :half] >> 16) | (bits[:, half:] & jnp.uint32(0xFFFF0000))


def _unpack_words(w):
    lo = lax.bitcast_convert_type(w << 16, F32)
    hi = lax.bitcast_convert_type(w & jnp.uint32(0xFFFF0000), F32)
    return jnp.concatenate([lo, hi], axis=-1)


def _ffn_expert_kernel(te_ref, nu_ref, x_ref, wg_ref, wu_ref, wd_ref, o_ref, xb_ref, acc_ref):
    i = pl.program_id(0)
    j = pl.program_id(1)

    @pl.when(i < nu_ref[0])
    def _():
        @pl.when(j == 0)
        def _():
            xb_ref[...] = _unpack_words(x_ref[...]).astype(BF16)
            acc_ref[...] = jnp.zeros_like(acc_ref)

        x = xb_ref[...]
        act = (jax.nn.silu(_dot(x, wg_ref[0].astype(BF16))) * _dot(x, wu_ref[0].astype(BF16))).astype(BF16)
        acc_ref[...] += _dot(act, wd_ref[0].astype(BF16))

    @pl.when(j == pl.num_programs(1) - 1)
    def _():
        o_ref[...] = _pack_words(acc_ref[...])


def _ffn_expert(tile_expert, n_used, xs, w_gate, w_up, w_down, tm, tf):
    p, words = xs.shape
    d = 2 * words
    f = w_gate.shape[2]
    tf = min(tf, f)
    nj = f // tf

    def col(i, j, te, nu):
        return (te[i], 0, jnp.where(i < nu[0], j, nj - 1))

    def rowb(i, j, te, nu):
        return (te[i], jnp.where(i < nu[0], j, nj - 1), 0)

    grid_spec = pltpu.PrefetchScalarGridSpec(
        num_scalar_prefetch=2,
        grid=(p // tm, nj),
        in_specs=[pl.BlockSpec((tm, words), lambda i, j, te, nu: (i, 0)),
                  pl.BlockSpec((1, d, tf), col),
                  pl.BlockSpec((1, d, tf), col),
                  pl.BlockSpec((1, tf, d), rowb)],
        out_specs=pl.BlockSpec((tm, words), lambda i, j, te, nu: (i, 0)),
        scratch_shapes=[pltpu.VMEM((tm, d), BF16), pltpu.VMEM((tm, d), F32)],
    )
    return pl.pallas_call(
        _ffn_expert_kernel,
        grid_spec=grid_spec,
        out_shape=jax.ShapeDtypeStruct((p, words), jnp.uint32),
        compiler_params=_params(("arbitrary", "arbitrary"), 48),
        name="ffn_expert",
    )(tile_expert, n_used, xs, w_gate, w_up, w_down)


def _qkv_kernel(x_ref, g_ref, w_ref, qkg_ref, qk_ref, v_ref, *, n_qk):
    xn = _rms(x_ref[...], g_ref[...]).astype(BF16)
    z = _dot(xn, w_ref[...])
    blk = 256
    ones = _group_ones(blk, DIFF_HEAD)
    for c in range(n_qk // blk):
        s = slice(c * blk, (c + 1) * blk)
        zc = z[:, s]
        ms = _group_sum(zc * zc, ones) * (1.0 / DIFF_HEAD)
        qk_ref[:, s] = (zc * lax.rsqrt(ms + NORM_EPS) * qkg_ref[:, s]).astype(qk_ref.dtype)
    v_ref[...] = z[:, n_qk:].astype(v_ref.dtype)


def _qkv(x, gain, w, qk_gain, tm):
    n, d = x.shape
    c = w.shape[1]
    n_qk = qk_gain.shape[1]
    tm = min(tm, n)
    return pl.pallas_call(
        functools.partial(_qkv_kernel, n_qk=n_qk),
        grid=(n // tm,),
        in_specs=[pl.BlockSpec((tm, d), lambda i: (i, 0)),
                  pl.BlockSpec((1, d), lambda i: (0, 0)),
                  pl.BlockSpec((d, c), lambda i: (0, 0)),
                  pl.BlockSpec((1, n_qk), lambda i: (0, 0))],
        out_specs=[pl.BlockSpec((tm, n_qk), lambda i: (i, 0)),
                   pl.BlockSpec((tm, c - n_qk), lambda i: (i, 0))],
        out_shape=[jax.ShapeDtypeStruct((n, n_qk), BF16),
                   jax.ShapeDtypeStruct((n, c - n_qk), BF16)],
        compiler_params=_params(("parallel",), 48),
        name="qkv",
    )(x, gain, w, qk_gain)


def _diff_attn_kernel(ti_ref, tj_ref, slope_ref, lam_ref, q_ref, kc_ref, kn_ref, v_ref, sg_ref, o_ref,
                      qs_ref, s0_ref, s1_ref, p_ref, m_ref, l_ref, acc_ref, dg_ref, *, out_scale, rb):
    h = pl.program_id(1)
    step = pl.program_id(2)
    i = ti_ref[step]
    j = tj_ref[step]
    tq = q_ref.shape[1]
    tk = kc_ref.shape[1]
    nl = tk // LANES
    slope = slope_ref[h]
    slot = lax.rem(j, 2)

    @pl.when(j == 0)
    def _():
        q = q_ref[0]
        lane = lax.broadcasted_iota(jnp.int32, q.shape, 1)
        zero = jnp.zeros_like(q)
        qs_ref[0:tq, :] = jnp.where(lane < DIFF_HEAD, q, zero)
        qs_ref[tq:2 * tq, :] = jnp.where(lane < DIFF_HEAD, zero, q)
        m_ref[...] = jnp.full_like(m_ref, NEG_INF)
        l_ref[...] = jnp.zeros_like(l_ref)
        acc_ref[...] = jnp.zeros_like(acc_ref)
        s0_ref[...] = _dot_nt(qs_ref[...], kc_ref[0])

    @pl.when(step == 0)
    def _():
        r = lax.broadcasted_iota(jnp.int32, (tq, tk), 0)
        c = lax.broadcasted_iota(jnp.int32, (tq, tk), 1)
        ahead = jnp.maximum(c - r, 0).astype(F32) * (-2.0 * slope)
        dg_ref[...] = jnp.where(c // ATTN_CHUNK <= r // ATTN_CHUNK, ahead, NEG_INF)

    col = lax.broadcasted_iota(jnp.int32, (1, tk), 1)
    key_bias = slope * (col + (j - i) * tq).astype(F32)

    def update(diag, cur_ref):
        for r0 in range(0, 2 * tq, rb):
            rows = slice(r0, r0 + rb)
            s = cur_ref[rows, :] + key_bias
            if diag:
                d0 = r0 % tq
                s = s + dg_ref[d0:d0 + rb, :]
            blk = s[:, 0:LANES]
            for c in range(1, nl):
                blk = jnp.maximum(blk, s[:, c * LANES:(c + 1) * LANES])
            m_prev = m_ref[rows, :]
            m_new = jnp.maximum(m_prev, jnp.max(blk, axis=-1, keepdims=True))
            corr = jnp.exp2(m_prev - m_new)
            p = jnp.exp2(s - pltpu.repeat(m_new, nl, axis=1))
            part = p[:, 0:LANES]
            for c in range(1, nl):
                part = part + p[:, c * LANES:(c + 1) * LANES]
            l_ref[rows, :] = corr * l_ref[rows, :] + part
            acc_ref[rows, :] = corr * acc_ref[rows, :]
            m_ref[rows, :] = m_new
            p_ref[rows, :] = p.astype(BF16)
        v = v_ref[0]
        for half in range(2):
            rows = slice(half * tq, (half + 1) * tq)
            acc_ref[rows, :] += _dot(p_ref[rows, :], v)

    for parity, (cur_ref, nxt_ref) in enumerate(((s0_ref, s1_ref), (s1_ref, s0_ref))):
        @pl.when(jnp.logical_and(j < i, slot == parity))
        def _(cur_ref=cur_ref, nxt_ref=nxt_ref):
            nxt_ref[...] = _dot_nt(qs_ref[...], kn_ref[0])
            update(False, cur_ref)

        @pl.when(jnp.logical_and(j == i, slot == parity))
        def _(cur_ref=cur_ref):
            update(True, cur_ref)

    @pl.when(j == i)
    def _():
        o = acc_ref[...] / jnp.sum(l_ref[...], axis=-1, keepdims=True)
        o = o[:tq] - lam_ref[0] * o[tq:]
        o = o * lax.rsqrt(jnp.mean(o * o, axis=-1, keepdims=True) + NORM_EPS) * sg_ref[...]
        o_ref[0] = (o * out_scale).astype(o_ref.dtype)


def _diff_attn(slopes, lam, qk, v, sub_gain, heads, out_scale, tq, rb):
    b, t, _ = v.shape
    hw = 2 * DIFF_HEAD
    tq = min(tq, t)
    rb = min(rb, tq)
    nt = t // tq
    pairs = [(i, j) for i in range(nt) for j in range(i + 1)]
    ti = jnp.asarray([p[0] for p in pairs], jnp.int32)
    tj = jnp.asarray([p[1] for p in pairs], jnp.int32)
    smem = pl.BlockSpec(memory_space=pltpu.SMEM)
    grid_spec = pltpu.PrefetchScalarGridSpec(
        num_scalar_prefetch=2,
        grid=(b, heads, len(pairs)),
        in_specs=[smem, smem,
                  pl.BlockSpec((1, tq, hw), lambda bi, h, s, ti, tj: (bi, ti[s], h)),
                  pl.BlockSpec((1, tq, hw), lambda bi, h, s, ti, tj: (bi, tj[s], heads + h)),
                  pl.BlockSpec((1, tq, hw),
                               lambda bi, h, s, ti, tj: (bi, jnp.minimum(tj[s] + 1, ti[s]), heads + h)),
                  pl.BlockSpec((1, tq, hw), lambda bi, h, s, ti, tj: (bi, tj[s], h)),
                  pl.BlockSpec((1, hw), lambda bi, h, s, ti, tj: (0, 0))],
        out_specs=pl.BlockSpec((1, tq, hw), lambda bi, h, s, ti, tj: (bi, ti[s], h)),
        scratch_shapes=[pltpu.VMEM((2 * tq, hw), BF16),
                        pltpu.VMEM((2 * tq, tq), F32),
                        pltpu.VMEM((2 * tq, tq), F32),
                        pltpu.VMEM((2 * tq, tq), BF16),
                        pltpu.VMEM((2 * tq, LANES), F32),
                        pltpu.VMEM((2 * tq, LANES), F32),
                        pltpu.VMEM((2 * tq, hw), F32),
                        pltpu.VMEM((tq, tq), F32)],
    )
    return pl.pallas_call(
        functools.partial(_diff_attn_kernel, out_scale=out_scale, rb=rb),
        grid_spec=grid_spec,
        out_shape=jax.ShapeDtypeStruct((b, t, heads * hw), BF16),
        compiler_params=_params(("arbitrary", "arbitrary", "arbitrary"), 48),
        name="diff_attn",
    )(ti, tj, slopes, lam, qk, qk, qk, v, sub_gain)


def _proj_residual_kernel(h_ref, a_ref, w_ref, o_ref):
    o_ref[...] = h_ref[...] + _dot(a_ref[...], w_ref[...])


def _proj_residual(h, a, w, tm):
    n, d = h.shape
    tm = min(tm, n)
    return pl.pallas_call(
        _proj_residual_kernel,
        grid=(n // tm,),
        in_specs=[pl.BlockSpec((tm, d), lambda i: (i, 0)),
                  pl.BlockSpec((tm, a.shape[1]), lambda i: (i, 0)),
                  pl.BlockSpec(w.shape, lambda i: (0, 0))],
        out_specs=pl.BlockSpec((tm, d), lambda i: (i, 0)),
        out_shape=jax.ShapeDtypeStruct((n, d), F32),
        compiler_params=_params(("parallel",), 32),
        name="proj_residual",
    )(h, a, w)


def _router_kernel(h_ref, g_ref, w_ref, b_ref, xn_ref, r_ref):
    xn = _rms(h_ref[...], g_ref[...])
    xn_ref[...] = _pack_words(xn)
    logits = jnp.dot(xn, w_ref[...], preferred_element_type=F32,
                     precision=lax.Precision.HIGHEST) + b_ref[...]
    lane = lax.broadcasted_iota(jnp.int32, logits.shape, 1).astype(F32)
    big = float(ROUTER_LANES)
    l1 = jnp.max(logits, axis=-1, keepdims=True)
    i1 = jnp.min(jnp.where(logits == l1, lane, big), axis=-1, keepdims=True)
    rest = jnp.where(lane == i1, NEG_INF, logits)
    l2 = jnp.max(rest, axis=-1, keepdims=True)
    i2 = jnp.min(jnp.where(rest == l2, lane, big), axis=-1, keepdims=True)
    e = jnp.exp(l2 - l1)
    w1 = 1.0 / (1.0 + e)
    w2 = e / (1.0 + e)
    r_ref[...] = jnp.where(lane == 0.0, i1, jnp.where(lane == 1.0, i2,
                           jnp.where(lane == 2.0, w1, jnp.where(lane == 3.0, w2, 0.0))))


def _router(h, gain, w_pad, b_pad, tm):
    n, d = h.shape
    tm = min(tm, n)
    return pl.pallas_call(
        _router_kernel,
        grid=(n // tm,),
        in_specs=[pl.BlockSpec((tm, d), lambda i: (i, 0)),
                  pl.BlockSpec((1, d), lambda i: (0, 0)),
                  pl.BlockSpec(w_pad.shape, lambda i: (0, 0)),
                  pl.BlockSpec(b_pad.shape, lambda i: (0, 0))],
        out_specs=[pl.BlockSpec((tm, d // 2), lambda i: (i, 0)),
                   pl.BlockSpec((tm, ROUTER_LANES), lambda i: (i, 0))],
        out_shape=[jax.ShapeDtypeStruct((n, d // 2), jnp.uint32),
                   jax.ShapeDtypeStruct((n, ROUTER_LANES), F32)],
        compiler_params=_params(("parallel",), 32),
        name="router",
    )(h, gain, w_pad, b_pad)


GATHER_UNROLL = 8


def _gather_copy(idx_ref, x_hbm, o_ref, sem, base, r):
    return pltpu.make_async_copy(x_hbm.at[pl.ds(idx_ref[base + r], 1)], o_ref.at[pl.ds(r, 1)], sem)


def _gather_kernel(idx_ref, x_hbm, o_ref, sem):
    rows = o_ref.shape[0]
    base = pl.program_id(0) * rows

    def start(r, c):
        _gather_copy(idx_ref, x_hbm, o_ref, sem, base, r).start()
        return c

    lax.fori_loop(0, rows, start, 0, unroll=GATHER_UNROLL)
    pltpu.make_async_copy(x_hbm.at[pl.ds(0, rows)], o_ref, sem).wait()


def _gather_rows(idx, x, tg):
    p = idx.shape[0]
    w = x.shape[1]
    tg = min(tg, p)
    assert p % tg == 0 and tg % GATHER_UNROLL == 0
    grid_spec = pltpu.PrefetchScalarGridSpec(
        num_scalar_prefetch=1,
        grid=(p // tg,),
        in_specs=[pl.BlockSpec(memory_space=pl.ANY)],
        out_specs=pl.BlockSpec((tg, w), lambda i, idx_ref: (i, 0)),
        scratch_shapes=[pltpu.SemaphoreType.DMA(())],
    )
    return pl.pallas_call(
        _gather_kernel,
        grid_spec=grid_spec,
        out_shape=jax.ShapeDtypeStruct((p, w), x.dtype),
        compiler_params=_params(("arbitrary",), 32),
        name="gather_rows",
    )(idx, x)


def _combine_kernel(h_ref, y0_ref, y1_ref, r_ref, o_ref):
    r = r_ref[...]
    w0 = r[:, 2:3]
    w1 = r[:, 3:4]
    o_ref[...] = h_ref[...] + w0 * _unpack_words(y0_ref[...]) + w1 * _unpack_words(y1_ref[...])


def _combine(h, yg, route, tm):
    n, d = h.shape
    tm = min(tm, n)
    nb = n // tm
    return pl.pallas_call(
        _combine_kernel,
        grid=(nb,),
        in_specs=[pl.BlockSpec((tm, d), lambda i: (i, 0)),
                  pl.BlockSpec((tm, d // 2), lambda i: (i, 0)),
                  pl.BlockSpec((tm, d // 2), lambda i: (i + nb, 0)),
                  pl.BlockSpec((tm, ROUTER_LANES), lambda i: (i, 0))],
        out_specs=pl.BlockSpec((tm, d), lambda i: (i, 0)),
        out_shape=jax.ShapeDtypeStruct((n, d), F32),
        compiler_params=_params(("parallel",), 32),
        name="combine",
    )(h, yg, yg, route)


def _moe(h, gain, w_router, b_router, w_gate, w_up, w_down, tm, tf):
    n, d = h.shape
    e = w_gate.shape[0]
    w_pad = jnp.zeros((d, ROUTER_LANES), F32).at[:, :e].set(w_router.astype(F32))
    b_pad = jnp.full((1, ROUTER_LANES), NEG_INF, F32).at[0, :e].set(b_router.astype(F32))
    xn, route = _router(h, gain, w_pad, b_pad, 512)

    tm = min(tm, n)
    experts = jnp.concatenate([route[:, 0], route[:, 1]]).astype(jnp.int32)
    onehot = (experts[:, None] == jnp.arange(e, dtype=jnp.int32)[None, :]).astype(jnp.int32)
    counts = jnp.sum(onehot, axis=0)
    padded = ((counts + tm - 1) // tm) * tm
    ends = jnp.cumsum(padded)
    starts = ends - padded
    pos = jnp.sum(onehot * (jnp.cumsum(onehot, axis=0) - onehot + starts[None, :]), axis=1)
    cap = 2 * n + e * tm
    tokens = jnp.tile(jnp.arange(n, dtype=jnp.int32), 2)
    src = jnp.zeros((cap,), jnp.int32).at[pos].set(tokens)
    tile_start = jnp.arange(cap // tm, dtype=jnp.int32) * tm
    tile_expert = jnp.minimum(jnp.sum((tile_start[:, None] >= ends[None, :]).astype(jnp.int32), axis=1),
                              e - 1)
    n_used = (ends[-1:] // tm).astype(jnp.int32)

    xs = _gather_rows(src, xn, 2048)
    ys = _ffn_expert(tile_expert, n_used, xs, w_gate, w_up, w_down, tm, tf)
    yg = _gather_rows(pos.astype(jnp.int32), ys, 2048)
    return _combine(h, yg, route, 512)


def _row(v):
    return v.astype(F32).reshape(1, -1)


def _memory_block(h, mem, norm_x, norm_m, w_q, w_kv, q_gain, k_gain, w_o):
    k, v = _mem_kv(mem, _row(norm_m), w_kv.astype(BF16), _row(k_gain), MEM_HEADS)
    return _xattn(h, _row(norm_x), w_q.astype(BF16), _row(q_gain), k, v, w_o.astype(BF16),
                  MEM_HEADS, 512)


def _even_mixer(h, norm, hy_w_in, rw_mu, rw_w0, rw_w_up, rw_a0, rw_a_up, rw_g_up, rw_k_k, rw_k_a,
                rw_r_k, rw_ln_w, rw_ln_b, s5_lam_re, s5_lam_im, s5_log_dt, s5_b_re, s5_b_im,
                s5_c_re, s5_c_im, s5_d, s5_w_glu, s5_b_glu, hy_w_out):
    b, t, d = h.shape
    width = rw_w0.shape[0]
    n_lora = DECAY_LORA + ICLR_LORA + GATE_LORA
    rwkv_cols = 3 * width + n_lora
    n_r = 3 * width + LORA_PAD
    pad = LORA_PAD - n_lora
    w_in = jnp.concatenate([hy_w_in[:, :rwkv_cols], jnp.zeros((d, pad), hy_w_in.dtype),
                            hy_w_in[:, rwkv_cols:]], axis=1).astype(BF16)
    mu = jnp.concatenate([rw_mu.astype(F32), jnp.zeros((pad,), F32)]).reshape(1, -1)
    w_lora = jnp.zeros((LORA_PAD, 3 * width), F32)
    w_lora = w_lora.at[:DECAY_LORA, :width].set(rw_w_up.astype(F32))
    w_lora = w_lora.at[DECAY_LORA:DECAY_LORA + ICLR_LORA, width:2 * width].set(rw_a_up.astype(F32))
    w_lora = w_lora.at[DECAY_LORA + ICLR_LORA:n_lora, 2 * width:].set(rw_g_up.astype(F32))

    h2 = h.reshape(b * t, d)
    zr, u = _in_proj(h2, _row(norm), w_in, n_r, 512)
    parts = _rwkv_prep(zr.reshape(b, t, n_r), mu, w_lora.astype(BF16), _row(rw_w0), _row(rw_a0),
                       _row(rw_k_k), _row(rw_k_a), width, 512)
    ya = _rwkv_chunk(*parts, _row(rw_r_k), _row(rw_ln_w), _row(rw_ln_b), width // RWKV_HEAD, 2)

    bblk, cblk, lam_r, lam_i = _s5_tables(s5_lam_re, s5_lam_im, s5_log_dt, s5_b_re, s5_b_im,
                                          s5_c_re, s5_c_im)
    yb = _s5(u.reshape(b, t, -1), bblk, cblk, lam_r, lam_i, _row(s5_d))

    w_out = hy_w_out.astype(BF16)
    out = _mix_out(h2, ya.reshape(b * t, width), yb.reshape(b * t, -1), s5_w_glu.astype(BF16),
                   _row(s5_b_glu), w_out[:width], w_out[width:], 512)
    return out.reshape(b, t, d)


def _odd_mixer(h, norm, layer, w_qkv, q_gain, k_gain, lam_q1, lam_k1, lam_q2, lam_k2, sub_gain, w_o):
    b, t, d = h.shape
    heads = w_o.shape[0] // (2 * DIFF_HEAD)
    lambda_init = 0.8 - 0.6 * math.exp(-0.3 * layer)
    lam = (jnp.exp(jnp.sum(lam_q1.astype(F32) * lam_k1.astype(F32)))
           - jnp.exp(jnp.sum(lam_q2.astype(F32) * lam_k2.astype(F32))) + lambda_init).reshape(1)
    log2e = math.log2(math.e)
    slopes = 2.0 ** (-8.0 * jnp.arange(1, heads + 1, dtype=F32) / heads) * log2e
    n_half = heads * 2 * DIFF_HEAD
    qk_gain = jnp.concatenate([jnp.tile(q_gain.astype(F32), n_half // DIFF_HEAD) * (DIFF_HEAD ** -0.5 * log2e),
                               jnp.tile(k_gain.astype(F32), n_half // DIFF_HEAD)]).reshape(1, -1)
    h2 = h.reshape(b * t, d)
    qk, v = _qkv(h2, _row(norm), w_qkv.astype(BF16), qk_gain, 512)
    o = _diff_attn(slopes, lam, qk.reshape(b, t, -1), v.reshape(b, t, -1), _row(sub_gain),
                   heads, 1.0 - lambda_init, 1024, 128)
    return _proj_residual(h2, o.reshape(b * t, -1), w_o.astype(BF16), 512).reshape(b, t, d)


def kernel(x, mem, norm_mix, norm_xattn, norm_mem, norm_ffn, xa_w_q, xa_w_kv, xa_q_gain, xa_k_gain, xa_w_o, hy_w_in, rw_mu, rw_w0, rw_w_up, rw_a0, rw_a_up, rw_g_up, rw_k_k, rw_k_a, rw_r_k, rw_ln_w, rw_ln_b, s5_lam_re, s5_lam_im, s5_log_dt, s5_b_re, s5_b_im, s5_c_re, s5_c_im, s5_d, s5_w_glu, s5_b_glu, hy_w_out, ff_w_gate, ff_w_up, ff_w_down, da_w_qkv, da_q_gain, da_k_gain, da_lam_q1, da_lam_k1, da_lam_q2, da_lam_k2, da_sub_gain, da_w_o, moe_w_router, moe_b_router, moe_w_gate, moe_w_up, moe_w_down):
    depth = norm_mix.shape[0]
    b, t, d = x.shape
    h = x
    for l in range(depth):
        i = l // 2
        if l % 2 == 0:
            h = _even_mixer(h, norm_mix[l], hy_w_in[i], rw_mu[i], rw_w0[i], rw_w_up[i], rw_a0[i],
                            rw_a_up[i], rw_g_up[i], rw_k_k[i], rw_k_a[i], rw_r_k[i], rw_ln_w[i],
                            rw_ln_b[i], s5_lam_re[i], s5_lam_im[i], s5_log_dt[i], s5_b_re[i],
                            s5_b_im[i], s5_c_re[i], s5_c_im[i], s5_d[i], s5_w_glu[i], s5_b_glu[i],
                            hy_w_out[i])
        else:
            h = _odd_mixer(h, norm_mix[l], l, da_w_qkv[i], da_q_gain[i], da_k_gain[i], da_lam_q1[i],
                           da_lam_k1[i], da_lam_q2[i], da_lam_k2[i], da_sub_gain[i], da_w_o[i])
        h = _memory_block(h, mem, norm_xattn[l], norm_mem[l], xa_w_q[l], xa_w_kv[l], xa_q_gain[l],
                          xa_k_gain[l], xa_w_o[l])
        h2 = h.reshape(b * t, d)
        if l % 2 == 0:
            h2 = _ffn_dense(h2, _row(norm_ffn[l]), ff_w_gate[i].astype(BF16), ff_w_up[i].astype(BF16),
                            ff_w_down[i].astype(BF16), 512, 1408)
        else:
            h2 = _moe(h2, _row(norm_ffn[l]), moe_w_router[i], moe_b_router[i],
                      moe_w_gate[i], moe_w_up[i], moe_w_down[i], 1024, 512)
        h = h2.reshape(b, t, d)
    return h
```

```python
import functools
import math

import jax
import jax.numpy as jnp
from jax import lax
from jax.experimental import pallas as pl
from jax.experimental.pallas import tpu as pltpu

F32 = jnp.float32
BF16 = jnp.bfloat16

NORM_EPS = 1e-6
NEG_INF = -1e30
ATTN_CHUNK = 64

RWKV_HEAD = 64
RWKV_CHUNK = 64
DECAY_LORA = 64
ICLR_LORA = 64
GATE_LORA = 160
LORA_PAD = 384
RWKV_GN_EPS = 1e-5 * RWKV_HEAD

S5_GROUP = 16
S5_STATE = 64
S5_TILE_GROUPS = 8

DIFF_HEAD = 64
MEM_HEADS = 4
N_EXPERTS = 8
LANES = 128
ROUTER_LANES = LANES

V7X_VMEM_BYTES = 64 * 1024 * 1024


def _params(semantics, vmem_mib):
    assert vmem_mib * 1024 * 1024 < V7X_VMEM_BYTES
    return pltpu.CompilerParams(dimension_semantics=semantics,
                                vmem_limit_bytes=vmem_mib * 1024 * 1024)


def _rms(x, gain):
    return x * lax.rsqrt(jnp.mean(x * x, axis=-1, keepdims=True) + NORM_EPS) * gain


def _dot(a, b):
    return jnp.dot(a, b, preferred_element_type=F32)


def _dot_nt(a, b):
    return lax.dot_general(a, b, (((1,), (1,)), ((), ())), preferred_element_type=F32)


def _dot_tn(a, b):
    return lax.dot_general(a, b, (((0,), (0,)), ((), ())), preferred_element_type=F32)


def _group_ones(width, group):
    r = lax.broadcasted_iota(jnp.int32, (width, width), 0) // group
    c = lax.broadcasted_iota(jnp.int32, (width, width), 1) // group
    return jnp.where(r == c, 1.0, 0.0).astype(BF16)


def _group_sum(x, ones):
    hi = x.astype(BF16)
    lo = (x - hi.astype(F32)).astype(BF16)
    return _dot(hi, ones) + _dot(lo, ones)


def _in_proj_kernel(x_ref, g_ref, w_ref, zr_ref, u_ref, *, n_r):
    xn = _rms(x_ref[...], g_ref[...]).astype(BF16)
    z = _dot(xn, w_ref[...])
    zr_ref[...] = z[:, :n_r]
    u_ref[...] = z[:, n_r:].astype(u_ref.dtype)


def _in_proj(x, gain, w, n_r, tm):
    n, d = x.shape
    c = w.shape[1]
    tm = min(tm, n)
    return pl.pallas_call(
        functools.partial(_in_proj_kernel, n_r=n_r),
        grid=(n // tm,),
        in_specs=[pl.BlockSpec((tm, d), lambda i: (i, 0)),
                  pl.BlockSpec((1, d), lambda i: (0, 0)),
                  pl.BlockSpec((d, c), lambda i: (0, 0))],
        out_specs=[pl.BlockSpec((tm, n_r), lambda i: (i, 0)),
                   pl.BlockSpec((tm, c - n_r), lambda i: (i, 0))],
        out_shape=[jax.ShapeDtypeStruct((n, n_r), F32),
                   jax.ShapeDtypeStruct((n, c - n_r), BF16)],
        compiler_params=_params(("parallel",), 48),
        name="in_proj",
    )(x, gain, w)


def _rwkv_prep_kernel(z_ref, mu_ref, wl_ref, w0_ref, a0_ref, kk_ref, ka_ref,
                      r_ref, lw_ref, k_ref, v_ref, kn_ref, a_ref, g_ref, carry_ref, *, width):
    i = pl.program_id(1)

    @pl.when(i == 0)
    def _():
        carry_ref[...] = jnp.zeros_like(carry_ref)

    z = z_ref[0]
    rows = lax.broadcasted_iota(jnp.int32, z.shape, 0)
    prev = jnp.where(rows == 0, carry_ref[...], pltpu.roll(z, 1, axis=0))
    carry_ref[...] = z[z.shape[0] - 1:, :]
    zs = z + (prev - z) * mu_ref[...]

    w = width
    r = zs[:, :w]
    k = zs[:, w:2 * w]
    v = zs[:, 2 * w:3 * w]
    lo = zs[:, 3 * w:]
    lane = lax.broadcasted_iota(jnp.int32, lo.shape, 1)
    lo = jnp.where(lane < DECAY_LORA, jnp.tanh(lo),
                   jnp.where(lane < DECAY_LORA + ICLR_LORA, lo, jax.nn.sigmoid(lo)))
    m = _dot(lo.astype(BF16), wl_ref[...])
    w_log = -jax.nn.softplus(-(w0_ref[...] + m[:, :w])) - 0.5
    a = jax.nn.sigmoid(a0_ref[...] + m[:, w:2 * w])
    kn = k * kk_ref[...]
    ss = _group_sum(kn * kn, _group_ones(w, RWKV_HEAD))
    kn = kn * lax.rsqrt(jnp.maximum(ss, 1e-24))

    r_ref[0] = r
    lw_ref[0] = -jnp.exp(w_log)
    k_ref[0] = k * (1.0 + (a - 1.0) * ka_ref[...])
    v_ref[0] = v
    kn_ref[0] = kn
    a_ref[0] = a
    g_ref[0] = m[:, 2 * w:]


def _rwkv_prep(zr, mu, wl, w0, a0, k_k, k_a, width, tm):
    b, t, c = zr.shape
    tm = min(tm, t)
    row = lambda n: pl.BlockSpec((1, n), lambda bi, i: (0, 0))
    out = pl.BlockSpec((1, tm, width), lambda bi, i: (bi, i, 0))
    return pl.pallas_call(
        functools.partial(_rwkv_prep_kernel, width=width),
        grid=(b, t // tm),
        in_specs=[pl.BlockSpec((1, tm, c), lambda bi, i: (bi, i, 0)),
                  row(c),
                  pl.BlockSpec(wl.shape, lambda bi, i: (0, 0)),
                  row(width), row(width), row(width), row(width)],
        out_specs=[out] * 7,
        out_shape=[jax.ShapeDtypeStruct((b, t, width), F32)] * 7,
        scratch_shapes=[pltpu.VMEM((1, c), F32)],
        compiler_params=_params(("parallel", "arbitrary"), 48),
        name="rwkv_prep",
    )(zr, mu, wl, w0, a0, k_k, k_a)


def _rwkv_chunk_kernel(r_ref, lw_ref, k_ref, v_ref, kn_ref, a_ref, g_ref,
                       rk_ref, lnw_ref, lnb_ref, o_ref, st_ref, *, heads, chunks, L):
    @pl.when(pl.program_id(2) == 0)
    def _():
        st_ref[...] = jnp.zeros_like(st_ref)

    n = RWKV_HEAD
    ri = lax.broadcasted_iota(jnp.int32, (L, L), 0)
    ci = lax.broadcasted_iota(jnp.int32, (L, L), 1)
    tri = jnp.where(ci <= ri, 1.0, 0.0).astype(BF16)
    strict = ci < ri
    incl = ci <= ri
    eye = jnp.where(ci == ri, 1.0, 0.0).astype(F32)
    rk_all, lnw_all, lnb_all = rk_ref[...], lnw_ref[...], lnb_ref[...]
    inv_steps = max(1, int(math.ceil(math.log2(L))) - 1)

    items = [(c, h) for c in range(chunks) for h in range(heads)]
    fs = []
    for c, h in items:
        rows = slice(c * L, (c + 1) * L)
        s = slice(h * n, (h + 1) * n)
        r, k, v = r_ref[0, rows, s], k_ref[0, rows, s], v_ref[0, rows, s]
        kn, a, lw = kn_ref[0, rows, s], a_ref[0, rows, s], lw_ref[0, rows, s]
        hi = lw.astype(BF16)
        lo = (lw - hi.astype(F32)).astype(BF16)
        fs.append(dict(r=r, k=k, v=v, lw=lw, alpha=-kn, beta=kn * a, v_b=v.astype(BF16),
                       cum=_dot(tri, hi) + _dot(tri, lo)))
    for f in fs:
        cum, lw = f["cum"], f["lw"]
        cum_ex = cum - lw
        c_mid = cum[L // 2 - 1:L // 2, :]
        c_end = cum[L - 1:L, :]
        e_inv = jnp.exp(c_mid - cum)
        e_end = jnp.exp(c_end - cum)
        lhs = jnp.concatenate([f["alpha"] * jnp.exp(cum_ex - c_mid), f["r"] * jnp.exp(cum - c_mid)],
                              axis=0).astype(BF16)
        f["m_b"] = _dot_nt(lhs, (f["beta"] * e_inv).astype(BF16))
        f["m_k"] = _dot_nt(lhs, (f["k"] * e_inv).astype(BF16))
        f["abs_lhs"] = jnp.concatenate([f["alpha"] * jnp.exp(cum_ex), f["r"] * jnp.exp(cum)],
                                       axis=0).astype(BF16)
        f["b_end"] = (f["beta"] * e_end).astype(BF16)
        f["k_end"] = (f["k"] * e_end).astype(BF16)
        f["decay"] = jnp.exp(c_end)
    for f in fs:
        m_ab = jnp.where(strict, f["m_b"][:L], 0.0)
        f["m_rb"] = jnp.where(incl, f["m_b"][L:], 0.0).astype(BF16)
        f["ak_v"] = _dot(jnp.where(strict, f["m_k"][:L], 0.0).astype(BF16), f["v_b"])
        f["rk_v"] = _dot(jnp.where(incl, f["m_k"][L:], 0.0).astype(BF16), f["v_b"])
        f["inv"] = eye + m_ab
        f["p"] = m_ab
    for _ in range(inv_steps):
        for f in fs:
            pb = f["p"].astype(BF16)
            f["p"] = _dot(pb, pb)
        for f in fs:
            f["inv"] = f["inv"] + _dot(f["inv"].astype(BF16), f["p"].astype(BF16))
    for f in fs:
        f["inv"] = f["inv"].astype(BF16)

    st = [st_ref[h] for h in range(heads)]
    for c in range(chunks):
        cf = fs[c * heads:(c + 1) * heads]
        from_state = [_dot_nt(f["abs_lhs"], st[h].astype(BF16)) for h, f in enumerate(cf)]
        u_b = [_dot(f["inv"], (from_state[h][:L] + f["ak_v"]).astype(BF16)).astype(BF16)
               for h, f in enumerate(cf)]
        st = [st[h] * f["decay"] + _dot_tn(u_b[h], f["b_end"]) + _dot_tn(f["v_b"], f["k_end"])
              for h, f in enumerate(cf)]
        for h, f in enumerate(cf):
            rows = slice(c * L, (c + 1) * L)
            s = slice(h * n, (h + 1) * n)
            o = from_state[h][L:] + _dot(f["m_rb"], u_b[h]) + f["rk_v"]
            mean = jnp.mean(o, axis=-1, keepdims=True)
            d = o - mean
            var = jnp.mean(d * d, axis=-1, keepdims=True)
            o = d * lax.rsqrt(var + RWKV_GN_EPS) * lnw_all[:, s] + lnb_all[:, s]
            bonus = jnp.sum(f["r"] * f["k"] * rk_all[:, s], axis=-1, keepdims=True) * f["v"]
            o_ref[0, rows, s] = ((o + bonus) * g_ref[0, rows, s]).astype(o_ref.dtype)
    for h in range(heads):
        st_ref[h] = st[h]


def _rwkv_chunk(r, lw, k, v, kn, a, g, r_k, ln_w, ln_b, heads_per_step, chunks_per_step):
    b, t, width = r.shape
    L = min(RWKV_CHUNK, t)
    chunks = min(chunks_per_step, t // L)
    lanes = heads_per_step * RWKV_HEAD
    seq = pl.BlockSpec((1, chunks * L, lanes), lambda bi, hi, ci: (bi, ci, hi))
    row = pl.BlockSpec((1, lanes), lambda bi, hi, ci: (0, hi))
    return pl.pallas_call(
        functools.partial(_rwkv_chunk_kernel, heads=heads_per_step, chunks=chunks, L=L),
        grid=(b, width // lanes, t // (chunks * L)),
        in_specs=[seq] * 7 + [row] * 3,
        out_specs=seq,
        out_shape=jax.ShapeDtypeStruct((b, t, width), BF16),
        scratch_shapes=[pltpu.VMEM((heads_per_step, RWKV_HEAD, RWKV_HEAD), F32)],
        compiler_params=_params(("parallel", "parallel", "arbitrary"), 32),
        name="rwkv_chunk",
    )(r, lw, k, v, kn, a, g, r_k, ln_w, ln_b)


S5_ROWS = 8


def _gelu_tanh(x):
    return 0.5 * x * (1.0 + jnp.tanh(math.sqrt(2.0 / math.pi) * (x + 0.044715 * (x * x * x))))


def _s5_kernel(u_ref, bb_ref, cb_ref, lr_ref, li_ref, d_ref, y_ref, x_ref, *, unroll, y_rows):
    half = lr_ref.shape[-1]
    t = u_ref.shape[1]
    nb = half // LANES
    lr = lr_ref[0]
    li = li_ref[0]

    for cblk in range(2 * nb):
        x_ref[cblk] = _dot(u_ref[0], bb_ref[0, :, cblk * LANES:(cblk + 1) * LANES])

    pows = [(lr, li)]
    for _ in range(S5_ROWS - 1):
        qr, qi = pows[-1]
        pows.append((qr * lr - qi * li, qr * li + qi * lr))
    pw_r = jnp.concatenate([q[0] for q in pows], axis=0)
    pw_i = jnp.concatenate([q[1] for q in pows], axis=0)
    row = lax.broadcasted_iota(jnp.int32, (S5_ROWS, half), 0)
    shifts = []
    k = 1
    while k < S5_ROWS:
        shifts.append((k, jnp.where(row >= k, pows[k - 1][0], 0.0), jnp.where(row >= k, pows[k - 1][1], 0.0)))
        k *= 2

    def tile_scan(ti, carry):
        rows = pl.ds(pl.multiple_of(ti * S5_ROWS, S5_ROWS), S5_ROWS)
        out = []
        for cblk in range(nb):
            lanes = slice(cblk * LANES, (cblk + 1) * LANES)
            xr = x_ref[cblk, rows, :]
            xi = x_ref[nb + cblk, rows, :]
            for k, ar, ai in shifts:
                sr = pltpu.roll(xr, k, axis=0)
                si = pltpu.roll(xi, k, axis=0)
                xr, xi = (xr + (ar[:, lanes] * sr - ai[:, lanes] * si),
                          xi + (ar[:, lanes] * si + ai[:, lanes] * sr))
            cr, ci = carry[cblk]
            xr, xi = (xr + (pw_r[:, lanes] * cr - pw_i[:, lanes] * ci),
                      xi + (pw_r[:, lanes] * ci + pw_i[:, lanes] * cr))
            x_ref[cblk, rows, :] = xr
            x_ref[nb + cblk, rows, :] = xi
            out.append((jnp.broadcast_to(xr[S5_ROWS - 1:, :], (S5_ROWS, LANES)),
                        jnp.broadcast_to(xi[S5_ROWS - 1:, :], (S5_ROWS, LANES))))
        return tuple(out)

    zero = jnp.zeros((S5_ROWS, LANES), F32)
    lax.fori_loop(0, t // S5_ROWS, tile_scan, tuple((zero, zero) for _ in range(nb)), unroll=unroll)

    d = d_ref[...]
    cb = cb_ref[0]
    for r0 in range(0, t, y_rows):
        rows = slice(r0, r0 + y_rows)
        x = jnp.concatenate([x_ref[cblk, rows, :] for cblk in range(2 * nb)], axis=-1).astype(BF16)
        y = _dot(x, cb) + d * u_ref[0, rows, :].astype(F32)
        y_ref[0, rows, :] = _gelu_tanh(y).astype(y_ref.dtype)


def _s5(u, bblk, cblk, lam_r, lam_i, d_skip):
    b, t, width = u.shape
    lanes = S5_TILE_GROUPS * S5_GROUP
    half = S5_TILE_GROUPS * S5_STATE
    tile = lambda shape: pl.BlockSpec((1,) + shape, lambda bi, ci: (ci, 0, 0))
    return pl.pallas_call(
        functools.partial(_s5_kernel, unroll=min(4, t // S5_ROWS), y_rows=min(512, t)),
        grid=(b, width // lanes),
        in_specs=[pl.BlockSpec((1, t, lanes), lambda bi, ci: (bi, 0, ci)),
                  tile((lanes, 2 * half)), tile((2 * half, lanes)),
                  tile((1, half)), tile((1, half)),
                  pl.BlockSpec((1, lanes), lambda bi, ci: (0, ci))],
        out_specs=pl.BlockSpec((1, t, lanes), lambda bi, ci: (bi, 0, ci)),
        out_shape=jax.ShapeDtypeStruct((b, t, width), BF16),
        scratch_shapes=[pltpu.VMEM((2 * half // LANES, t, LANES), F32)],
        compiler_params=_params(("parallel", "parallel"), 48),
        name="s5",
    )(u, bblk, cblk, lam_r, lam_i, d_skip)


def _s5_tables(lam_re, lam_im, log_dt, b_re, b_im, c_re, c_im):
    g, p = lam_re.shape
    tiles = g // S5_TILE_GROUPS
    lre, lim = lam_re.astype(F32), lam_im.astype(F32)
    dt = jnp.exp(log_dt.astype(F32))[:, None]
    mag = jnp.exp(lre * dt)
    bar_re = mag * jnp.cos(lim * dt)
    bar_im = mag * jnp.sin(lim * dt)
    den = lre * lre + lim * lim
    coef_re = ((bar_re - 1.0) * lre + bar_im * lim) / den
    coef_im = (bar_im * lre - (bar_re - 1.0) * lim) / den
    bre, bim = b_re.astype(F32), b_im.astype(F32)
    bbar_re = coef_re[..., None] * bre - coef_im[..., None] * bim
    bbar_im = coef_re[..., None] * bim + coef_im[..., None] * bre
    eye = jnp.eye(S5_TILE_GROUPS, dtype=F32)

    def pack_b(m):
        m = m.reshape(tiles, S5_TILE_GROUPS, p, S5_GROUP)
        return jnp.einsum('tgpi,gh->tgihp', m, eye).reshape(tiles, S5_TILE_GROUPS * S5_GROUP,
                                                            S5_TILE_GROUPS * p)

    def pack_c(m):
        m = m.reshape(tiles, S5_TILE_GROUPS, S5_GROUP, p)
        return jnp.einsum('tgip,gh->tgphi', m, eye).reshape(tiles, S5_TILE_GROUPS * p,
                                                            S5_TILE_GROUPS * S5_GROUP)

    bblk = jnp.concatenate([pack_b(bbar_re), pack_b(bbar_im)], axis=-1).astype(BF16)
    cblk = jnp.concatenate([pack_c(c_re.astype(F32)), -pack_c(c_im.astype(F32))], axis=1).astype(BF16)
    lam_r = bar_re.reshape(tiles, 1, S5_TILE_GROUPS * p)
    lam_i = bar_im.reshape(tiles, 1, S5_TILE_GROUPS * p)
    return bblk, cblk, lam_r, lam_i


def _mix_out_kernel(h_ref, ya_ref, yb_ref, wglu_ref, bglu_ref, woa_ref, wob_ref, o_ref):
    yb = yb_ref[...]
    gate = jax.nn.sigmoid(_dot(yb, wglu_ref[...]) + bglu_ref[...])
    yb = (yb.astype(F32) * gate).astype(BF16)
    o_ref[...] = h_ref[...] + _dot(ya_ref[...], woa_ref[...]) + _dot(yb, wob_ref[...])


def _mix_out(h, ya, yb, w_glu, b_glu, wo_a, wo_b, tm):
    n, d = h.shape
    tm = min(tm, n)
    wa, wb = ya.shape[1], yb.shape[1]
    full = lambda a: pl.BlockSpec(a.shape, lambda i: (0, 0))
    return pl.pallas_call(
        _mix_out_kernel,
        grid=(n // tm,),
        in_specs=[pl.BlockSpec((tm, d), lambda i: (i, 0)),
                  pl.BlockSpec((tm, wa), lambda i: (i, 0)),
                  pl.BlockSpec((tm, wb), lambda i: (i, 0)),
                  full(w_glu), full(b_glu), full(wo_a), full(wo_b)],
        out_specs=pl.BlockSpec((tm, d), lambda i: (i, 0)),
        out_shape=jax.ShapeDtypeStruct((n, d), F32),
        compiler_params=_params(("parallel",), 32),
        name="mix_out",
    )(h, ya, yb, w_glu, b_glu, wo_a, wo_b)


def _mem_kv_kernel(m_ref, g_ref, w_ref, kg_ref, k_ref, v_ref, *, heads):
    mn = _rms(m_ref[0], g_ref[...]).astype(BF16)
    kv = _dot(mn, w_ref[...])
    d = kv.shape[1] // 2
    hd = d // heads
    for h in range(heads):
        s = slice(h * hd, (h + 1) * hd)
        k_ref[0, :, s] = _rms(kv[:, s], kg_ref[...]).astype(k_ref.dtype)
    v_ref[0] = kv[:, d:].astype(v_ref.dtype)


def _mem_kv(mem, gain, w_kv, k_gain, heads):
    b, m, d = mem.shape
    blk = pl.BlockSpec((1, m, d), lambda bi: (bi, 0, 0))
    return pl.pallas_call(
        functools.partial(_mem_kv_kernel, heads=heads),
        grid=(b,),
        in_specs=[blk, pl.BlockSpec((1, d), lambda bi: (0, 0)),
                  pl.BlockSpec(w_kv.shape, lambda bi: (0, 0)),
                  pl.BlockSpec(k_gain.shape, lambda bi: (0, 0))],
        out_specs=[blk, blk],
        out_shape=[jax.ShapeDtypeStruct((b, m, d), BF16)] * 2,
        compiler_params=_params(("parallel",), 32),
        name="mem_kv",
    )(mem, gain, w_kv, k_gain)


def _xattn_kernel(h_ref, g_ref, wq_ref, qg_ref, k_ref, v_ref, wo_ref, o_ref, *, heads):
    h = h_ref[0]
    hn = _rms(h, g_ref[...]).astype(BF16)
    q = _dot(hn, wq_ref[...])
    hd = q.shape[1] // heads
    k = k_ref[0]
    v = v_ref[0]
    outs = []
    for i in range(heads):
        s = slice(i * hd, (i + 1) * hd)
        qh = (_rms(q[:, s], qg_ref[...]) * hd ** -0.5).astype(BF16)
        sc = _dot_nt(qh, k[:, s])
        sc = sc - jnp.max(sc, axis=-1, keepdims=True)
        p = jnp.exp(sc)
        p = p / jnp.sum(p, axis=-1, keepdims=True)
        outs.append(_dot(p.astype(BF16), v[:, s]).astype(BF16))
    o = jnp.concatenate(outs, axis=-1)
    o_ref[0] = h + _dot(o, wo_ref[...])


def _xattn(h, gain, w_q, q_gain, k, v, w_o, heads, tm):
    b, t, d = h.shape
    m = k.shape[1]
    tm = min(tm, t)
    full = lambda a: pl.BlockSpec(a.shape, lambda bi, i: (0,) * a.ndim)
    tok = pl.BlockSpec((1, tm, d), lambda bi, i: (bi, i, 0))
    mem = pl.BlockSpec((1, m, d), lambda bi, i: (bi, 0, 0))
    return pl.pallas_call(
        functools.partial(_xattn_kernel, heads=heads),
        grid=(b, t // tm),
        in_specs=[tok, full(gain), full(w_q), full(q_gain), mem, mem, full(w_o)],
        out_specs=tok,
        out_shape=jax.ShapeDtypeStruct((b, t, d), F32),
        compiler_params=_params(("parallel", "parallel"), 40),
        name="xattn",
    )(h, gain, w_q, q_gain, k, v, w_o)


def _ffn_dense_kernel(h_ref, g_ref, wg_ref, wu_ref, wd_ref, o_ref, xn_ref, acc_ref):
    j = pl.program_id(1)

    @pl.when(j == 0)
    def _():
        xn_ref[...] = _rms(h_ref[...], g_ref[...]).astype(BF16)
        acc_ref[...] = jnp.zeros_like(acc_ref)

    xn = xn_ref[...]
    act = (jax.nn.silu(_dot(xn, wg_ref[...])) * _dot(xn, wu_ref[...])).astype(BF16)
    acc_ref[...] += _dot(act, wd_ref[...])

    @pl.when(j == pl.num_programs(1) - 1)
    def _():
        o_ref[...] = h_ref[...] + acc_ref[...]


def _ffn_dense(h, gain, w_gate, w_up, w_down, tm, tf):
    n, d = h.shape
    f = w_gate.shape[1]
    tm, tf = min(tm, n), min(tf, f)
    return pl.pallas_call(
        _ffn_dense_kernel,
        grid=(n // tm, f // tf),
        in_specs=[pl.BlockSpec((tm, d), lambda i, j: (i, 0)),
                  pl.BlockSpec((1, d), lambda i, j: (0, 0)),
                  pl.BlockSpec((d, tf), lambda i, j: (0, j)),
                  pl.BlockSpec((d, tf), lambda i, j: (0, j)),
                  pl.BlockSpec((tf, d), lambda i, j: (j, 0))],
        out_specs=pl.BlockSpec((tm, d), lambda i, j: (i, 0)),
        out_shape=jax.ShapeDtypeStruct((n, d), F32),
        scratch_shapes=[pltpu.VMEM((tm, d), BF16), pltpu.VMEM((tm, d), F32)],
        compiler_params=_params(("parallel", "arbitrary"), 48),
        name="ffn_dense",
    )(h, gain, w_gate, w_up, w_down)


def _pack_words(x):
    bits = lax.bitcast_convert_type(x.astype(BF16).astype(F32), jnp.uint32)
    half = x.shape[1] // 2
    return (bits[:, :half] >> 16) | (bits[:, half:] & jnp.uint32(0xFFFF0000))


def _unpack_words(w):
    lo = lax.bitcast_convert_type(w << 16, F32)
    hi = lax.bitcast_convert_type(w & jnp.uint32(0xFFFF0000), F32)
    return jnp.concatenate([lo, hi], axis=-1)


def _ffn_expert_kernel(te_ref, nu_ref, x_ref, wg_ref, wu_ref, wd_ref, o_ref, xb_ref, acc_ref):
    i = pl.program_id(0)
    j = pl.program_id(1)

    @pl.when(i < nu_ref[0])
    def _():
        @pl.when(j == 0)
        def _():
            xb_ref[...] = _unpack_words(x_ref[...]).astype(BF16)
            acc_ref[...] = jnp.zeros_like(acc_ref)

        x = xb_ref[...]
        act = (jax.nn.silu(_dot(x, wg_ref[0].astype(BF16))) * _dot(x, wu_ref[0].astype(BF16))).astype(BF16)
        acc_ref[...] += _dot(act, wd_ref[0].astype(BF16))

    @pl.when(j == pl.num_programs(1) - 1)
    def _():
        o_ref[...] = _pack_words(acc_ref[...])


def _ffn_expert(tile_expert, n_used, xs, w_gate, w_up, w_down, tm, tf):
    p, words = xs.shape
    d = 2 * words
    f = w_gate.shape[2]
    tf = min(tf, f)
    nj = f // tf

    def col(i, j, te, nu):
        return (te[i], 0, jnp.where(i < nu[0], j, nj - 1))

    def rowb(i, j, te, nu):
        return (te[i], jnp.where(i < nu[0], j, nj - 1), 0)

    grid_spec = pltpu.PrefetchScalarGridSpec(
        num_scalar_prefetch=2,
        grid=(p // tm, nj),
        in_specs=[pl.BlockSpec((tm, words), lambda i, j, te, nu: (i, 0)),
                  pl.BlockSpec((1, d, tf), col),
                  pl.BlockSpec((1, d, tf), col),
                  pl.BlockSpec((1, tf, d), rowb)],
        out_specs=pl.BlockSpec((tm, words), lambda i, j, te, nu: (i, 0)),
        scratch_shapes=[pltpu.VMEM((tm, d), BF16), pltpu.VMEM((tm, d), F32)],
    )
    return pl.pallas_call(
        _ffn_expert_kernel,
        grid_spec=grid_spec,
        out_shape=jax.ShapeDtypeStruct((p, words), jnp.uint32),
        compiler_params=_params(("arbitrary", "arbitrary"), 48),
        name="ffn_expert",
    )(tile_expert, n_used, xs, w_gate, w_up, w_down)


def _qkv_kernel(x_ref, g_ref, w_ref, qkg_ref, qk_ref, v_ref, *, n_qk):
    xn = _rms(x_ref[...], g_ref[...]).astype(BF16)
    z = _dot(xn, w_ref[...])
    blk = 256
    ones = _group_ones(blk, DIFF_HEAD)
    for c in range(n_qk // blk):
        s = slice(c * blk, (c + 1) * blk)
        zc = z[:, s]
        ms = _group_sum(zc * zc, ones) * (1.0 / DIFF_HEAD)
        qk_ref[:, s] = (zc * lax.rsqrt(ms + NORM_EPS) * qkg_ref[:, s]).astype(qk_ref.dtype)
    v_ref[...] = z[:, n_qk:].astype(v_ref.dtype)


def _qkv(x, gain, w, qk_gain, tm):
    n, d = x.shape
    c = w.shape[1]
    n_qk = qk_gain.shape[1]
    tm = min(tm, n)
    return pl.pallas_call(
        functools.partial(_qkv_kernel, n_qk=n_qk),
        grid=(n // tm,),
        in_specs=[pl.BlockSpec((tm, d), lambda i: (i, 0)),
                  pl.BlockSpec((1, d), lambda i: (0, 0)),
                  pl.BlockSpec((d, c), lambda i: (0, 0)),
                  pl.BlockSpec((1, n_qk), lambda i: (0, 0))],
        out_specs=[pl.BlockSpec((tm, n_qk), lambda i: (i, 0)),
                   pl.BlockSpec((tm, c - n_qk), lambda i: (i, 0))],
        out_shape=[jax.ShapeDtypeStruct((n, n_qk), BF16),
                   jax.ShapeDtypeStruct((n, c - n_qk), BF16)],
        compiler_params=_params(("parallel",), 48),
        name="qkv",
    )(x, gain, w, qk_gain)


def _diff_attn_kernel(ti_ref, tj_ref, slope_ref, lam_ref, q_ref, kc_ref, kn_ref, v_ref, sg_ref, o_ref,
                      qs_ref, s0_ref, s1_ref, p_ref, m_ref, l_ref, acc_ref, dg_ref, *, out_scale, rb):
    h = pl.program_id(1)
    step = pl.program_id(2)
    i = ti_ref[step]
    j = tj_ref[step]
    tq = q_ref.shape[1]
    tk = kc_ref.shape[1]
    nl = tk // LANES
    slope = slope_ref[h]
    slot = lax.rem(j, 2)

    @pl.when(j == 0)
    def _():
        q = q_ref[0]
        lane = lax.broadcasted_iota(jnp.int32, q.shape, 1)
        zero = jnp.zeros_like(q)
        qs_ref[0:tq, :] = jnp.where(lane < DIFF_HEAD, q, zero)
        qs_ref[tq:2 * tq, :] = jnp.where(lane < DIFF_HEAD, zero, q)
        m_ref[...] = jnp.full_like(m_ref, NEG_INF)
        l_ref[...] = jnp.zeros_like(l_ref)
        acc_ref[...] = jnp.zeros_like(acc_ref)
        s0_ref[...] = _dot_nt(qs_ref[...], kc_ref[0])

    @pl.when(step == 0)
    def _():
        r = lax.broadcasted_iota(jnp.int32, (tq, tk), 0)
        c = lax.broadcasted_iota(jnp.int32, (tq, tk), 1)
        ahead = jnp.maximum(c - r, 0).astype(F32) * (-2.0 * slope)
        dg_ref[...] = jnp.where(c // ATTN_CHUNK <= r // ATTN_CHUNK, ahead, NEG_INF)

    col = lax.broadcasted_iota(jnp.int32, (1, tk), 1)
    key_bias = slope * (col + (j - i) * tq).astype(F32)

    def update(diag, cur_ref):
        for r0 in range(0, 2 * tq, rb):
            rows = slice(r0, r0 + rb)
            s = cur_ref[rows, :] + key_bias
            if diag:
                d0 = r0 % tq
                s = s + dg_ref[d0:d0 + rb, :]
            blk = s[:, 0:LANES]
            for c in range(1, nl):
                blk = jnp.maximum(blk, s[:, c * LANES:(c + 1) * LANES])
            m_prev = m_ref[rows, :]
            m_new = jnp.maximum(m_prev, jnp.max(blk, axis=-1, keepdims=True))
            corr = jnp.exp2(m_prev - m_new)
            p = jnp.exp2(s - jnp.concatenate([m_new] * nl, axis=1))
            part = p[:, 0:LANES]
            for c in range(1, nl):
                part = part + p[:, c * LANES:(c + 1) * LANES]
            l_ref[rows, :] = corr * l_ref[rows, :] + part
            acc_ref[rows, :] = corr * acc_ref[rows, :]
            m_ref[rows, :] = m_new
            p_ref[rows, :] = p.astype(BF16)
        v = v_ref[0]
        for half in range(2):
            rows = slice(half * tq, (half + 1) * tq)
            acc_ref[rows, :] += _dot(p_ref[rows, :], v)

    for parity, (cur_ref, nxt_ref) in enumerate(((s0_ref, s1_ref), (s1_ref, s0_ref))):
        @pl.when(jnp.logical_and(j < i, slot == parity))
        def _(cur_ref=cur_ref, nxt_ref=nxt_ref):
            nxt_ref[...] = _dot_nt(qs_ref[...], kn_ref[0])
            update(False, cur_ref)

        @pl.when(jnp.logical_and(j == i, slot == parity))
        def _(cur_ref=cur_ref):
            update(True, cur_ref)

    @pl.when(j == i)
    def _():
        o = acc_ref[...] / jnp.sum(l_ref[...], axis=-1, keepdims=True)
        o = o[:tq] - lam_ref[0] * o[tq:]
        o = o * lax.rsqrt(jnp.mean(o * o, axis=-1, keepdims=True) + NORM_EPS) * sg_ref[...]
        o_ref[0] = (o * out_scale).astype(o_ref.dtype)


def _diff_attn(slopes, lam, qk, v, sub_gain, heads, out_scale, tq, rb):
    b, t, _ = v.shape
    hw = 2 * DIFF_HEAD
    tq = min(tq, t)
    rb = min(rb, tq)
    nt = t // tq
    pairs = [(i, j) for i in range(nt) for j in range(i + 1)]
    ti = jnp.asarray([p[0] for p in pairs], jnp.int32)
    tj = jnp.asarray([p[1] for p in pairs], jnp.int32)
    smem = pl.BlockSpec(memory_space=pltpu.SMEM)
    grid_spec = pltpu.PrefetchScalarGridSpec(
        num_scalar_prefetch=2,
        grid=(b, heads, len(pairs)),
        in_specs=[smem, smem,
                  pl.BlockSpec((1, tq, hw), lambda bi, h, s, ti, tj: (bi, ti[s], h)),
                  pl.BlockSpec((1, tq, hw), lambda bi, h, s, ti, tj: (bi, tj[s], heads + h)),
                  pl.BlockSpec((1, tq, hw),
                               lambda bi, h, s, ti, tj: (bi, jnp.minimum(tj[s] + 1, ti[s]), heads + h)),
                  pl.BlockSpec((1, tq, hw), lambda bi, h, s, ti, tj: (bi, tj[s], h)),
                  pl.BlockSpec((1, hw), lambda bi, h, s, ti, tj: (0, 0))],
        out_specs=pl.BlockSpec((1, tq, hw), lambda bi, h, s, ti, tj: (bi, ti[s], h)),
        scratch_shapes=[pltpu.VMEM((2 * tq, hw), BF16),
                        pltpu.VMEM((2 * tq, tq), F32),
                        pltpu.VMEM((2 * tq, tq), F32),
                        pltpu.VMEM((2 * tq, tq), BF16),
                        pltpu.VMEM((2 * tq, LANES), F32),
                        pltpu.VMEM((2 * tq, LANES), F32),
                        pltpu.VMEM((2 * tq, hw), F32),
                        pltpu.VMEM((tq, tq), F32)],
    )
    return pl.pallas_call(
        functools.partial(_diff_attn_kernel, out_scale=out_scale, rb=rb),
        grid_spec=grid_spec,
        out_shape=jax.ShapeDtypeStruct((b, t, heads * hw), BF16),
        compiler_params=_params(("arbitrary", "arbitrary", "arbitrary"), 48),
        name="diff_attn",
    )(ti, tj, slopes, lam, qk, qk, qk, v, sub_gain)


def _proj_residual_kernel(h_ref, a_ref, w_ref, o_ref):
    o_ref[...] = h_ref[...] + _dot(a_ref[...], w_ref[...])


def _proj_residual(h, a, w, tm):
    n, d = h.shape
    tm = min(tm, n)
    return pl.pallas_call(
        _proj_residual_kernel,
        grid=(n // tm,),
        in_specs=[pl.BlockSpec((tm, d), lambda i: (i, 0)),
                  pl.BlockSpec((tm, a.shape[1]), lambda i: (i, 0)),
                  pl.BlockSpec(w.shape, lambda i: (0, 0))],
        out_specs=pl.BlockSpec((tm, d), lambda i: (i, 0)),
        out_shape=jax.ShapeDtypeStruct((n, d), F32),
        compiler_params=_params(("parallel",), 32),
        name="proj_residual",
    )(h, a, w)


def _router_kernel(h_ref, g_ref, w_ref, b_ref, xn_ref, r_ref):
    xn = _rms(h_ref[...], g_ref[...])
    xn_ref[...] = _pack_words(xn)
    logits = jnp.dot(xn, w_ref[...], preferred_element_type=F32,
                     precision=lax.Precision.HIGHEST) + b_ref[...]
    lane = lax.broadcasted_iota(jnp.int32, logits.shape, 1).astype(F32)
    big = float(ROUTER_LANES)
    l1 = jnp.max(logits, axis=-1, keepdims=True)
    i1 = jnp.min(jnp.where(logits == l1, lane, big), axis=-1, keepdims=True)
    rest = jnp.where(lane == i1, NEG_INF, logits)
    l2 = jnp.max(rest, axis=-1, keepdims=True)
    i2 = jnp.min(jnp.where(rest == l2, lane, big), axis=-1, keepdims=True)
    e = jnp.exp(l2 - l1)
    w1 = 1.0 / (1.0 + e)
    w2 = e / (1.0 + e)
    r_ref[...] = jnp.where(lane == 0.0, i1, jnp.where(lane == 1.0, i2,
                           jnp.where(lane == 2.0, w1, jnp.where(lane == 3.0, w2, 0.0))))


def _router(h, gain, w_pad, b_pad, tm):
    n, d = h.shape
    tm = min(tm, n)
    return pl.pallas_call(
        _router_kernel,
        grid=(n // tm,),
        in_specs=[pl.BlockSpec((tm, d), lambda i: (i, 0)),
                  pl.BlockSpec((1, d), lambda i: (0, 0)),
                  pl.BlockSpec(w_pad.shape, lambda i: (0, 0)),
                  pl.BlockSpec(b_pad.shape, lambda i: (0, 0))],
        out_specs=[pl.BlockSpec((tm, d // 2), lambda i: (i, 0)),
                   pl.BlockSpec((tm, ROUTER_LANES), lambda i: (i, 0))],
        out_shape=[jax.ShapeDtypeStruct((n, d // 2), jnp.uint32),
                   jax.ShapeDtypeStruct((n, ROUTER_LANES), F32)],
        compiler_params=_params(("parallel",), 32),
        name="router",
    )(h, gain, w_pad, b_pad)


GATHER_UNROLL = 8


def _gather_copy(idx_ref, x_hbm, o_ref, sem, base, r):
    return pltpu.make_async_copy(x_hbm.at[pl.ds(idx_ref[base + r], 1)], o_ref.at[pl.ds(r, 1)], sem)


def _gather_kernel(idx_ref, x_hbm, o_ref, sem):
    rows = o_ref.shape[0]
    base = pl.program_id(0) * rows

    def start(r, c):
        _gather_copy(idx_ref, x_hbm, o_ref, sem, base, r).start()
        return c

    lax.fori_loop(0, rows, start, 0, unroll=GATHER_UNROLL)
    pltpu.make_async_copy(x_hbm.at[pl.ds(0, rows)], o_ref, sem).wait()


def _gather_rows(idx, x, tg):
    p = idx.shape[0]
    w = x.shape[1]
    tg = min(tg, p)
    assert p % tg == 0 and tg % GATHER_UNROLL == 0
    grid_spec = pltpu.PrefetchScalarGridSpec(
        num_scalar_prefetch=1,
        grid=(p // tg,),
        in_specs=[pl.BlockSpec(memory_space=pl.ANY)],
        out_specs=pl.BlockSpec((tg, w), lambda i, idx_ref: (i, 0)),
        scratch_shapes=[pltpu.SemaphoreType.DMA(())],
    )
    return pl.pallas_call(
        _gather_kernel,
        grid_spec=grid_spec,
        out_shape=jax.ShapeDtypeStruct((p, w), x.dtype),
        compiler_params=_params(("arbitrary",), 32),
        name="gather_rows",
    )(idx, x)


def _scatter_kernel(pos_ref, x_ref, init_hbm, o_hbm, sem):
    del init_hbm
    rows = x_ref.shape[0]
    base = pl.program_id(0) * rows

    def start(r, c):
        pltpu.make_async_copy(x_ref.at[pl.ds(r, 1)], o_hbm.at[pl.ds(pos_ref[base + r], 1)], sem).start()
        return c

    lax.fori_loop(0, rows, start, 0, unroll=GATHER_UNROLL)
    pltpu.make_async_copy(x_ref, o_hbm.at[pl.ds(0, rows)], sem).wait()


def _scatter_rows(pos, x, cap, tg):
    a = pos.shape[0]
    n, w = x.shape
    tg = min(tg, n)
    assert n % tg == 0 and a % n == 0 and tg % GATHER_UNROLL == 0
    nblk = n // tg
    grid_spec = pltpu.PrefetchScalarGridSpec(
        num_scalar_prefetch=1,
        grid=(a // tg,),
        in_specs=[pl.BlockSpec((tg, w), lambda i, pos_ref: (lax.rem(i, nblk), 0)),
                  pl.BlockSpec(memory_space=pl.ANY)],
        out_specs=pl.BlockSpec(memory_space=pl.ANY),
        scratch_shapes=[pltpu.SemaphoreType.DMA(())],
    )
    return pl.pallas_call(
        _scatter_kernel,
        grid_spec=grid_spec,
        out_shape=jax.ShapeDtypeStruct((cap, w), x.dtype),
        input_output_aliases={2: 0},
        compiler_params=_params(("arbitrary",), 32),
        name="scatter_rows",
    )(pos, x, jnp.zeros((cap, w), x.dtype))


def _combine_kernel(h_ref, y0_ref, y1_ref, r_ref, o_ref):
    r = r_ref[...]
    w0 = r[:, 2:3]
    w1 = r[:, 3:4]
    o_ref[...] = h_ref[...] + w0 * _unpack_words(y0_ref[...]) + w1 * _unpack_words(y1_ref[...])


def _combine(h, yg, route, tm):
    n, d = h.shape
    tm = min(tm, n)
    nb = n // tm
    return pl.pallas_call(
        _combine_kernel,
        grid=(nb,),
        in_specs=[pl.BlockSpec((tm, d), lambda i: (i, 0)),
                  pl.BlockSpec((tm, d // 2), lambda i: (i, 0)),
                  pl.BlockSpec((tm, d // 2), lambda i: (i + nb, 0)),
                  pl.BlockSpec((tm, ROUTER_LANES), lambda i: (i, 0))],
        out_specs=pl.BlockSpec((tm, d), lambda i: (i, 0)),
        out_shape=jax.ShapeDtypeStruct((n, d), F32),
        compiler_params=_params(("parallel",), 32),
        name="combine",
    )(h, yg, yg, route)


def _moe(h, gain, w_router, b_router, w_gate, w_up, w_down, tm, tf):
    n, d = h.shape
    e = w_gate.shape[0]
    w_pad = jnp.zeros((d, ROUTER_LANES), F32).at[:, :e].set(w_router.astype(F32))
    b_pad = jnp.full((1, ROUTER_LANES), NEG_INF, F32).at[0, :e].set(b_router.astype(F32))
    xn, route = _router(h, gain, w_pad, b_pad, 512)

    tm = min(tm, n)
    experts = jnp.concatenate([route[:, 0], route[:, 1]]).astype(jnp.int32)
    onehot = (experts[:, None] == jnp.arange(e, dtype=jnp.int32)[None, :]).astype(jnp.int32)
    counts = jnp.sum(onehot, axis=0)
    padded = ((counts + tm - 1) // tm) * tm
    ends = jnp.cumsum(padded)
    starts = ends - padded
    pos = jnp.sum(onehot * (jnp.cumsum(onehot, axis=0) - onehot + starts[None, :]), axis=1)
    pos = pos.astype(jnp.int32)
    cap = 2 * n + e * tm
    tile_start = jnp.arange(cap // tm, dtype=jnp.int32) * tm
    tile_expert = jnp.minimum(jnp.sum((tile_start[:, None] >= ends[None, :]).astype(jnp.int32), axis=1),
                              e - 1)
    n_used = (ends[-1:] // tm).astype(jnp.int32)

    xs = _scatter_rows(pos, xn, cap, 2048)
    ys = _ffn_expert(tile_expert, n_used, xs, w_gate, w_up, w_down, tm, tf)
    yg = _gather_rows(pos, ys, 2048)
    return _combine(h, yg, route, 512)


def _row(v):
    return v.astype(F32).reshape(1, -1)


def _memory_block(h, mem, norm_x, norm_m, w_q, w_kv, q_gain, k_gain, w_o):
    k, v = _mem_kv(mem, _row(norm_m), w_kv.astype(BF16), _row(k_gain), MEM_HEADS)
    return _xattn(h, _row(norm_x), w_q.astype(BF16), _row(q_gain), k, v, w_o.astype(BF16),
                  MEM_HEADS, 512)


def _even_mixer(h, norm, hy_w_in, rw_mu, rw_w0, rw_w_up, rw_a0, rw_a_up, rw_g_up, rw_k_k, rw_k_a,
                rw_r_k, rw_ln_w, rw_ln_b, s5_lam_re, s5_lam_im, s5_log_dt, s5_b_re, s5_b_im,
                s5_c_re, s5_c_im, s5_d, s5_w_glu, s5_b_glu, hy_w_out):
    b, t, d = h.shape
    width = rw_w0.shape[0]
    n_lora = DECAY_LORA + ICLR_LORA + GATE_LORA
    rwkv_cols = 3 * width + n_lora
    n_r = 3 * width + LORA_PAD
    pad = LORA_PAD - n_lora
    w_in = jnp.concatenate([hy_w_in[:, :rwkv_cols], jnp.zeros((d, pad), hy_w_in.dtype),
                            hy_w_in[:, rwkv_cols:]], axis=1).astype(BF16)
    mu = jnp.concatenate([rw_mu.astype(F32), jnp.zeros((pad,), F32)]).reshape(1, -1)
    w_lora = jnp.zeros((LORA_PAD, 3 * width), F32)
    w_lora = w_lora.at[:DECAY_LORA, :width].set(rw_w_up.astype(F32))
    w_lora = w_lora.at[DECAY_LORA:DECAY_LORA + ICLR_LORA, width:2 * width].set(rw_a_up.astype(F32))
    w_lora = w_lora.at[DECAY_LORA + ICLR_LORA:n_lora, 2 * width:].set(rw_g_up.astype(F32))

    h2 = h.reshape(b * t, d)
    zr, u = _in_proj(h2, _row(norm), w_in, n_r, 512)
    parts = _rwkv_prep(zr.reshape(b, t, n_r), mu, w_lora.astype(BF16), _row(rw_w0), _row(rw_a0),
                       _row(rw_k_k), _row(rw_k_a), width, 512)
    ya = _rwkv_chunk(*parts, _row(rw_r_k), _row(rw_ln_w), _row(rw_ln_b), width // RWKV_HEAD, 2)

    bblk, cblk, lam_r, lam_i = _s5_tables(s5_lam_re, s5_lam_im, s5_log_dt, s5_b_re, s5_b_im,
                                          s5_c_re, s5_c_im)
    yb = _s5(u.reshape(b, t, -1), bblk, cblk, lam_r, lam_i, _row(s5_d))

    w_out = hy_w_out.astype(BF16)
    out = _mix_out(h2, ya.reshape(b * t, width), yb.reshape(b * t, -1), s5_w_glu.astype(BF16),
                   _row(s5_b_glu), w_out[:width], w_out[width:], 512)
    return out.reshape(b, t, d)


def _odd_mixer(h, norm, layer, w_qkv, q_gain, k_gain, lam_q1, lam_k1, lam_q2, lam_k2, sub_gain, w_o):
    b, t, d = h.shape
    heads = w_o.shape[0] // (2 * DIFF_HEAD)
    lambda_init = 0.8 - 0.6 * math.exp(-0.3 * layer)
    lam = (jnp.exp(jnp.sum(lam_q1.astype(F32) * lam_k1.astype(F32)))
           - jnp.exp(jnp.sum(lam_q2.astype(F32) * lam_k2.astype(F32))) + lambda_init).reshape(1)
    log2e = math.log2(math.e)
    slopes = 2.0 ** (-8.0 * jnp.arange(1, heads + 1, dtype=F32) / heads) * log2e
    n_half = heads * 2 * DIFF_HEAD
    qk_gain = jnp.concatenate([jnp.tile(q_gain.astype(F32), n_half // DIFF_HEAD) * (DIFF_HEAD ** -0.5 * log2e),
                               jnp.tile(k_gain.astype(F32), n_half // DIFF_HEAD)]).reshape(1, -1)
    h2 = h.reshape(b * t, d)
    qk, v = _qkv(h2, _row(norm), w_qkv.astype(BF16), qk_gain, 512)
    o = _diff_attn(slopes, lam, qk.reshape(b, t, -1), v.reshape(b, t, -1), _row(sub_gain),
                   heads, 1.0 - lambda_init, 1024, 128)
    return _proj_residual(h2, o.reshape(b * t, -1), w_o.astype(BF16), 512).reshape(b, t, d)


def kernel(x, mem, norm_mix, norm_xattn, norm_mem, norm_ffn, xa_w_q, xa_w_kv, xa_q_gain, xa_k_gain, xa_w_o, hy_w_in, rw_mu, rw_w0, rw_w_up, rw_a0, rw_a_up, rw_g_up, rw_k_k, rw_k_a, rw_r_k, rw_ln_w, rw_ln_b, s5_lam_re, s5_lam_im, s5_log_dt, s5_b_re, s5_b_im, s5_c_re, s5_c_im, s5_d, s5_w_glu, s5_b_glu, hy_w_out, ff_w_gate, ff_w_up, ff_w_down, da_w_qkv, da_q_gain, da_k_gain, da_lam_q1, da_lam_k1, da_lam_q2, da_lam_k2, da_sub_gain, da_w_o, moe_w_router, moe_b_router, moe_w_gate, moe_w_up, moe_w_down):
    depth = norm_mix.shape[0]
    b, t, d = x.shape
    h = x
    for l in range(depth):
        i = l // 2
        if l % 2 == 0:
            h = _even_mixer(h, norm_mix[l], hy_w_in[i], rw_mu[i], rw_w0[i], rw_w_up[i], rw_a0[i],
                            rw_a_up[i], rw_g_up[i], rw_k_k[i], rw_k_a[i], rw_r_k[i], rw_ln_w[i],
                            rw_ln_b[i], s5_lam_re[i], s5_lam_im[i], s5_log_dt[i], s5_b_re[i],
                            s5_b_im[i], s5_c_re[i], s5_c_im[i], s5_d[i], s5_w_glu[i], s5_b_glu[i],
                            hy_w_out[i])
        else:
            h = _odd_mixer(h, norm_mix[l], l, da_w_qkv[i], da_q_gain[i], da_k_gain[i], da_lam_q1[i],
                           da_lam_k1[i], da_lam_q2[i], da_lam_k2[i], da_sub_gain[i], da_w_o[i])
        h = _memory_block(h, mem, norm_xattn[l], norm_mem[l], xa_w_q[l], xa_w_kv[l], xa_q_gain[l],
                          xa_k_gain[l], xa_w_o[l])
        h2 = h.reshape(b * t, d)
        if l % 2 == 0:
            h2 = _ffn_dense(h2, _row(norm_ffn[l]), ff_w_gate[i].astype(BF16), ff_w_up[i].astype(BF16),
                            ff_w_down[i].astype(BF16), 512, 1408)
        else:
            h2 = _moe(h2, _row(norm_ffn[l]), moe_w_router[i], moe_b_router[i],
                      moe_w_gate[i], moe_w_up[i], moe_w_down[i], 1024, 512)
        h = h2.reshape(b, t, d)
    return h
```

```python
import functools
import math

import jax
import jax.numpy as jnp
from jax import lax
from jax.experimental import pallas as pl
from jax.experimental.pallas import tpu as pltpu

F32 = jnp.float32
BF16 = jnp.bfloat16

NORM_EPS = 1e-6
NEG_INF = -1e30
ATTN_CHUNK = 64

RWKV_HEAD = 64
RWKV_CHUNK = 64
DECAY_LORA = 64
ICLR_LORA = 64
GATE_LORA = 160
LORA_PAD = 384
RWKV_GN_EPS = 1e-5 * RWKV_HEAD

S5_GROUP = 16
S5_STATE = 64
S5_TILE_GROUPS = 8

DIFF_HEAD = 64
MEM_HEADS = 4
N_EXPERTS = 8
LANES = 128
ROUTER_LANES = LANES

V7X_VMEM_BYTES = 64 * 1024 * 1024


def _params(semantics, vmem_mib):
    assert vmem_mib * 1024 * 1024 < V7X_VMEM_BYTES
    return pltpu.CompilerParams(dimension_semantics=semantics,
                                vmem_limit_bytes=vmem_mib * 1024 * 1024)


def _rms(x, gain):
    return x * lax.rsqrt(jnp.mean(x * x, axis=-1, keepdims=True) + NORM_EPS) * gain


def _dot(a, b):
    return jnp.dot(a, b, preferred_element_type=F32)


def _dot_nt(a, b):
    return lax.dot_general(a, b, (((1,), (1,)), ((), ())), preferred_element_type=F32)


def _dot_tn(a, b):
    return lax.dot_general(a, b, (((0,), (0,)), ((), ())), preferred_element_type=F32)


def _group_ones(width, group):
    r = lax.broadcasted_iota(jnp.int32, (width, width), 0) // group
    c = lax.broadcasted_iota(jnp.int32, (width, width), 1) // group
    return jnp.where(r == c, 1.0, 0.0).astype(BF16)


def _group_sum(x, ones):
    hi = x.astype(BF16)
    lo = (x - hi.astype(F32)).astype(BF16)
    return _dot(hi, ones) + _dot(lo, ones)


def _rwkv_prep_kernel(x_ref, gain_ref, win_ref, mu_ref, wl_ref, w0_ref, a0_ref, kk_ref, ka_ref,
                      r_ref, lw_ref, k_ref, v_ref, kn_ref, a_ref, g_ref, u_ref, carry_ref, *, width):
    i = pl.program_id(1)

    @pl.when(i == 0)
    def _():
        carry_ref[...] = jnp.zeros_like(carry_ref)

    xn = _rms(x_ref[0], gain_ref[...]).astype(BF16)
    zall = _dot(xn, win_ref[...])
    n_r = mu_ref.shape[1]
    u_ref[0] = zall[:, n_r:].astype(u_ref.dtype)
    z = zall[:, :n_r]
    rows = lax.broadcasted_iota(jnp.int32, z.shape, 0)
    prev = jnp.where(rows == 0, carry_ref[...], pltpu.roll(z, 1, axis=0))
    carry_ref[...] = z[z.shape[0] - 1:, :]
    zs = z + (prev - z) * mu_ref[...]

    w = width
    r = zs[:, :w]
    k = zs[:, w:2 * w]
    v = zs[:, 2 * w:3 * w]
    lo = zs[:, 3 * w:]
    lane = lax.broadcasted_iota(jnp.int32, lo.shape, 1)
    lo = jnp.where(lane < DECAY_LORA, jnp.tanh(lo),
                   jnp.where(lane < DECAY_LORA + ICLR_LORA, lo, jax.nn.sigmoid(lo)))
    m = _dot(lo.astype(BF16), wl_ref[...])
    w_log = -jax.nn.softplus(-(w0_ref[...] + m[:, :w])) - 0.5
    a = jax.nn.sigmoid(a0_ref[...] + m[:, w:2 * w])
    kn = k * kk_ref[...]
    ss = _group_sum(kn * kn, _group_ones(w, RWKV_HEAD))
    kn = kn * lax.rsqrt(jnp.maximum(ss, 1e-24))

    r_ref[0] = r
    lw_ref[0] = -jnp.exp(w_log)
    k_ref[0] = k * (1.0 + (a - 1.0) * ka_ref[...])
    v_ref[0] = v
    kn_ref[0] = kn
    a_ref[0] = a
    g_ref[0] = m[:, 2 * w:]


def _rwkv_prep(x, gain, w_in, mu, wl, w0, a0, k_k, k_a, width, tm):
    b, t, d = x.shape
    c = mu.shape[1]
    n_u = w_in.shape[1] - c
    tm = min(tm, t)
    row = lambda n: pl.BlockSpec((1, n), lambda bi, i: (0, 0))
    full = lambda a: pl.BlockSpec(a.shape, lambda bi, i: (0, 0))
    out = lambda n: pl.BlockSpec((1, tm, n), lambda bi, i: (bi, i, 0))
    return pl.pallas_call(
        functools.partial(_rwkv_prep_kernel, width=width),
        grid=(b, t // tm),
        in_specs=[pl.BlockSpec((1, tm, d), lambda bi, i: (bi, i, 0)),
                  row(d), full(w_in), row(c), full(wl),
                  row(width), row(width), row(width), row(width)],
        out_specs=[out(width)] * 7 + [out(n_u)],
        out_shape=[jax.ShapeDtypeStruct((b, t, width), F32)] * 7 + [jax.ShapeDtypeStruct((b, t, n_u), BF16)],
        scratch_shapes=[pltpu.VMEM((1, c), F32)],
        compiler_params=_params(("parallel", "arbitrary"), 48),
        name="rwkv_prep",
    )(x, gain, w_in, mu, wl, w0, a0, k_k, k_a)


def _rwkv_chunk_kernel(r_ref, lw_ref, k_ref, v_ref, kn_ref, a_ref, g_ref,
                       rk_ref, lnw_ref, lnb_ref, o_ref, st_ref, *, heads, chunks, L):
    @pl.when(pl.program_id(2) == 0)
    def _():
        st_ref[...] = jnp.zeros_like(st_ref)

    n = RWKV_HEAD
    ri = lax.broadcasted_iota(jnp.int32, (L, L), 0)
    ci = lax.broadcasted_iota(jnp.int32, (L, L), 1)
    tri = jnp.where(ci <= ri, 1.0, 0.0).astype(BF16)
    strict = ci < ri
    incl = ci <= ri
    eye = jnp.where(ci == ri, 1.0, 0.0).astype(F32)
    rk_all, lnw_all, lnb_all = rk_ref[...], lnw_ref[...], lnb_ref[...]
    inv_steps = max(1, int(math.ceil(math.log2(L))) - 1)

    items = [(c, h) for c in range(chunks) for h in range(heads)]
    fs = []
    for c, h in items:
        rows = slice(c * L, (c + 1) * L)
        s = slice(h * n, (h + 1) * n)
        r, k, v = r_ref[0, rows, s], k_ref[0, rows, s], v_ref[0, rows, s]
        kn, a, lw = kn_ref[0, rows, s], a_ref[0, rows, s], lw_ref[0, rows, s]
        hi = lw.astype(BF16)
        lo = (lw - hi.astype(F32)).astype(BF16)
        fs.append(dict(r=r, k=k, v=v, lw=lw, alpha=-kn, beta=kn * a, v_b=v.astype(BF16),
                       cum=_dot(tri, hi) + _dot(tri, lo)))
    for f in fs:
        cum, lw = f["cum"], f["lw"]
        cum_ex = cum - lw
        c_mid = cum[L // 2 - 1:L // 2, :]
        c_end = cum[L - 1:L, :]
        e_inv = jnp.exp(c_mid - cum)
        e_end = jnp.exp(c_end - cum)
        lhs = jnp.concatenate([f["alpha"] * jnp.exp(cum_ex - c_mid), f["r"] * jnp.exp(cum - c_mid)],
                              axis=0).astype(BF16)
        f["m_b"] = _dot_nt(lhs, (f["beta"] * e_inv).astype(BF16))
        f["m_k"] = _dot_nt(lhs, (f["k"] * e_inv).astype(BF16))
        f["abs_lhs"] = jnp.concatenate([f["alpha"] * jnp.exp(cum_ex), f["r"] * jnp.exp(cum)],
                                       axis=0).astype(BF16)
        f["b_end"] = (f["beta"] * e_end).astype(BF16)
        f["k_end"] = (f["k"] * e_end).astype(BF16)
        f["decay"] = jnp.exp(c_end)
    for f in fs:
        m_ab = jnp.where(strict, f["m_b"][:L], 0.0)
        f["m_rb"] = jnp.where(incl, f["m_b"][L:], 0.0).astype(BF16)
        f["ak_v"] = _dot(jnp.where(strict, f["m_k"][:L], 0.0).astype(BF16), f["v_b"])
        f["rk_v"] = _dot(jnp.where(incl, f["m_k"][L:], 0.0).astype(BF16), f["v_b"])
        f["inv"] = eye + m_ab
        f["p"] = m_ab
    for _ in range(inv_steps):
        for f in fs:
            pb = f["p"].astype(BF16)
            f["p"] = _dot(pb, pb)
        for f in fs:
            f["inv"] = f["inv"] + _dot(f["inv"].astype(BF16), f["p"].astype(BF16))
    for f in fs:
        f["inv"] = f["inv"].astype(BF16)

    st = [st_ref[h] for h in range(heads)]
    for c in range(chunks):
        cf = fs[c * heads:(c + 1) * heads]
        from_state = [_dot_nt(f["abs_lhs"], st[h].astype(BF16)) for h, f in enumerate(cf)]
        u_b = [_dot(f["inv"], (from_state[h][:L] + f["ak_v"]).astype(BF16)).astype(BF16)
               for h, f in enumerate(cf)]
        st = [st[h] * f["decay"] + _dot_tn(u_b[h], f["b_end"]) + _dot_tn(f["v_b"], f["k_end"])
              for h, f in enumerate(cf)]
        for h, f in enumerate(cf):
            rows = slice(c * L, (c + 1) * L)
            s = slice(h * n, (h + 1) * n)
            o = from_state[h][L:] + _dot(f["m_rb"], u_b[h]) + f["rk_v"]
            mean = jnp.mean(o, axis=-1, keepdims=True)
            d = o - mean
            var = jnp.mean(d * d, axis=-1, keepdims=True)
            o = d * lax.rsqrt(var + RWKV_GN_EPS) * lnw_all[:, s] + lnb_all[:, s]
            bonus = jnp.sum(f["r"] * f["k"] * rk_all[:, s], axis=-1, keepdims=True) * f["v"]
            o_ref[0, rows, s] = ((o + bonus) * g_ref[0, rows, s]).astype(o_ref.dtype)
    for h in range(heads):
        st_ref[h] = st[h]


def _rwkv_chunk(r, lw, k, v, kn, a, g, r_k, ln_w, ln_b, heads_per_step, chunks_per_step):
    b, t, width = r.shape
    L = min(RWKV_CHUNK, t)
    chunks = min(chunks_per_step, t // L)
    lanes = heads_per_step * RWKV_HEAD
    seq = pl.BlockSpec((1, chunks * L, lanes), lambda bi, hi, ci: (bi, ci, hi))
    row = pl.BlockSpec((1, lanes), lambda bi, hi, ci: (0, hi))
    return pl.pallas_call(
        functools.partial(_rwkv_chunk_kernel, heads=heads_per_step, chunks=chunks, L=L),
        grid=(b, width // lanes, t // (chunks * L)),
        in_specs=[seq] * 7 + [row] * 3,
        out_specs=seq,
        out_shape=jax.ShapeDtypeStruct((b, t, width), BF16),
        scratch_shapes=[pltpu.VMEM((heads_per_step, RWKV_HEAD, RWKV_HEAD), F32)],
        compiler_params=_params(("parallel", "parallel", "arbitrary"), 32),
        name="rwkv_chunk",
    )(r, lw, k, v, kn, a, g, r_k, ln_w, ln_b)


S5_ROWS = 8


def _gelu_tanh(x):
    return 0.5 * x * (1.0 + jnp.tanh(math.sqrt(2.0 / math.pi) * (x + 0.044715 * (x * x * x))))


def _s5_kernel(u_ref, bb_ref, cb_ref, lr_ref, li_ref, d_ref, y_ref, x_ref, *, unroll, y_rows):
    half = lr_ref.shape[-1]
    t = u_ref.shape[1]
    nb = half // LANES
    lr = lr_ref[0]
    li = li_ref[0]

    for cblk in range(2 * nb):
        x_ref[cblk] = _dot(u_ref[0], bb_ref[0, :, cblk * LANES:(cblk + 1) * LANES])

    pows = [(lr, li)]
    for _ in range(S5_ROWS - 1):
        qr, qi = pows[-1]
        pows.append((qr * lr - qi * li, qr * li + qi * lr))
    pw_r = jnp.concatenate([q[0] for q in pows], axis=0)
    pw_i = jnp.concatenate([q[1] for q in pows], axis=0)
    row = lax.broadcasted_iota(jnp.int32, (S5_ROWS, half), 0)
    shifts = []
    k = 1
    while k < S5_ROWS:
        shifts.append((k, jnp.where(row >= k, pows[k - 1][0], 0.0), jnp.where(row >= k, pows[k - 1][1], 0.0)))
        k *= 2

    def tile_scan(ti, carry):
        rows = pl.ds(pl.multiple_of(ti * S5_ROWS, S5_ROWS), S5_ROWS)
        out = []
        for cblk in range(nb):
            lanes = slice(cblk * LANES, (cblk + 1) * LANES)
            xr = x_ref[cblk, rows, :]
            xi = x_ref[nb + cblk, rows, :]
            for k, ar, ai in shifts:
                sr = pltpu.roll(xr, k, axis=0)
                si = pltpu.roll(xi, k, axis=0)
                xr, xi = (xr + (ar[:, lanes] * sr - ai[:, lanes] * si),
                          xi + (ar[:, lanes] * si + ai[:, lanes] * sr))
            cr, ci = carry[cblk]
            xr, xi = (xr + (pw_r[:, lanes] * cr - pw_i[:, lanes] * ci),
                      xi + (pw_r[:, lanes] * ci + pw_i[:, lanes] * cr))
            x_ref[cblk, rows, :] = xr
            x_ref[nb + cblk, rows, :] = xi
            out.append((jnp.broadcast_to(xr[S5_ROWS - 1:, :], (S5_ROWS, LANES)),
                        jnp.broadcast_to(xi[S5_ROWS - 1:, :], (S5_ROWS, LANES))))
        return tuple(out)

    zero = jnp.zeros((S5_ROWS, LANES), F32)
    lax.fori_loop(0, t // S5_ROWS, tile_scan, tuple((zero, zero) for _ in range(nb)), unroll=unroll)

    d = d_ref[...]
    cb = cb_ref[0]
    for r0 in range(0, t, y_rows):
        rows = slice(r0, r0 + y_rows)
        x = jnp.concatenate([x_ref[cblk, rows, :] for cblk in range(2 * nb)], axis=-1).astype(BF16)
        y = _dot(x, cb) + d * u_ref[0, rows, :].astype(F32)
        y_ref[0, rows, :] = _gelu_tanh(y).astype(y_ref.dtype)


def _s5(u, bblk, cblk, lam_r, lam_i, d_skip):
    b, t, width = u.shape
    lanes = S5_TILE_GROUPS * S5_GROUP
    half = S5_TILE_GROUPS * S5_STATE
    tile = lambda shape: pl.BlockSpec((1,) + shape, lambda bi, ci: (ci, 0, 0))
    return pl.pallas_call(
        functools.partial(_s5_kernel, unroll=min(4, t // S5_ROWS), y_rows=min(512, t)),
        grid=(b, width // lanes),
        in_specs=[pl.BlockSpec((1, t, lanes), lambda bi, ci: (bi, 0, ci)),
                  tile((lanes, 2 * half)), tile((2 * half, lanes)),
                  tile((1, half)), tile((1, half)),
                  pl.BlockSpec((1, lanes), lambda bi, ci: (0, ci))],
        out_specs=pl.BlockSpec((1, t, lanes), lambda bi, ci: (bi, 0, ci)),
        out_shape=jax.ShapeDtypeStruct((b, t, width), BF16),
        scratch_shapes=[pltpu.VMEM((2 * half // LANES, t, LANES), F32)],
        compiler_params=_params(("parallel", "parallel"), 48),
        name="s5",
    )(u, bblk, cblk, lam_r, lam_i, d_skip)


def _s5_tables(lam_re, lam_im, log_dt, b_re, b_im, c_re, c_im):
    g, p = lam_re.shape
    tiles = g // S5_TILE_GROUPS
    lre, lim = lam_re.astype(F32), lam_im.astype(F32)
    dt = jnp.exp(log_dt.astype(F32))[:, None]
    mag = jnp.exp(lre * dt)
    bar_re = mag * jnp.cos(lim * dt)
    bar_im = mag * jnp.sin(lim * dt)
    den = lre * lre + lim * lim
    coef_re = ((bar_re - 1.0) * lre + bar_im * lim) / den
    coef_im = (bar_im * lre - (bar_re - 1.0) * lim) / den
    bre, bim = b_re.astype(F32), b_im.astype(F32)
    bbar_re = coef_re[..., None] * bre - coef_im[..., None] * bim
    bbar_im = coef_re[..., None] * bim + coef_im[..., None] * bre
    eye = jnp.eye(S5_TILE_GROUPS, dtype=F32)

    def pack_b(m):
        m = m.reshape(tiles, S5_TILE_GROUPS, p, S5_GROUP)
        return jnp.einsum('tgpi,gh->tgihp', m, eye).reshape(tiles, S5_TILE_GROUPS * S5_GROUP,
                                                            S5_TILE_GROUPS * p)

    def pack_c(m):
        m = m.reshape(tiles, S5_TILE_GROUPS, S5_GROUP, p)
        return jnp.einsum('tgip,gh->tgphi', m, eye).reshape(tiles, S5_TILE_GROUPS * p,
                                                            S5_TILE_GROUPS * S5_GROUP)

    bblk = jnp.concatenate([pack_b(bbar_re), pack_b(bbar_im)], axis=-1).astype(BF16)
    cblk = jnp.concatenate([pack_c(c_re.astype(F32)), -pack_c(c_im.astype(F32))], axis=1).astype(BF16)
    lam_r = bar_re.reshape(tiles, 1, S5_TILE_GROUPS * p)
    lam_i = bar_im.reshape(tiles, 1, S5_TILE_GROUPS * p)
    return bblk, cblk, lam_r, lam_i


def _mem_kv_kernel(m_ref, g_ref, w_ref, kg_ref, k_ref, v_ref, *, heads):
    mn = _rms(m_ref[0], g_ref[...]).astype(BF16)
    kv = _dot(mn, w_ref[...])
    d = kv.shape[1] // 2
    hd = d // heads
    for h in range(heads):
        s = slice(h * hd, (h + 1) * hd)
        k_ref[0, :, s] = _rms(kv[:, s], kg_ref[...]).astype(k_ref.dtype)
    v_ref[0] = kv[:, d:].astype(v_ref.dtype)


def _mem_kv(mem, gain, w_kv, k_gain, heads):
    b, m, d = mem.shape
    blk = pl.BlockSpec((1, m, d), lambda bi: (bi, 0, 0))
    return pl.pallas_call(
        functools.partial(_mem_kv_kernel, heads=heads),
        grid=(b,),
        in_specs=[blk, pl.BlockSpec((1, d), lambda bi: (0, 0)),
                  pl.BlockSpec(w_kv.shape, lambda bi: (0, 0)),
                  pl.BlockSpec(k_gain.shape, lambda bi: (0, 0))],
        out_specs=[blk, blk],
        out_shape=[jax.ShapeDtypeStruct((b, m, d), BF16)] * 2,
        compiler_params=_params(("parallel",), 32),
        name="mem_kv",
    )(mem, gain, w_kv, k_gain)


def _xattn_kernel(h_ref, g_ref, wq_ref, qg_ref, k_ref, v_ref, wo_ref, *rest, heads, glu):
    *mix, o_ref = rest
    h = h_ref[0]
    if glu:
        ya_ref, yb_ref, wglu_ref, bglu_ref, woa_ref, wob_ref = mix
        yb = yb_ref[0]
        gate = jax.nn.sigmoid(_dot(yb, wglu_ref[...]) + bglu_ref[...])
        yb = (yb.astype(F32) * gate).astype(BF16)
        h = h + _dot(ya_ref[0], woa_ref[...]) + _dot(yb, wob_ref[...])
    else:
        a_ref, w_ref = mix
        h = h + _dot(a_ref[0], w_ref[...])
    hn = _rms(h, g_ref[...]).astype(BF16)
    q = _dot(hn, wq_ref[...])
    hd = q.shape[1] // heads
    k = k_ref[0]
    v = v_ref[0]
    outs = []
    for i in range(heads):
        s = slice(i * hd, (i + 1) * hd)
        qh = (_rms(q[:, s], qg_ref[...]) * hd ** -0.5).astype(BF16)
        sc = _dot_nt(qh, k[:, s])
        sc = sc - jnp.max(sc, axis=-1, keepdims=True)
        p = jnp.exp(sc)
        p = p / jnp.sum(p, axis=-1, keepdims=True)
        outs.append(_dot(p.astype(BF16), v[:, s]).astype(BF16))
    o = jnp.concatenate(outs, axis=-1)
    o_ref[0] = h + _dot(o, wo_ref[...])


def _xattn(h, gain, w_q, q_gain, k, v, w_o, mix, heads, tm):
    b, t, d = h.shape
    m = k.shape[1]
    tm = min(tm, t)
    full = lambda a: pl.BlockSpec(a.shape, lambda bi, i: (0,) * a.ndim)
    tokw = lambda n: pl.BlockSpec((1, tm, n), lambda bi, i: (bi, i, 0))
    mem = pl.BlockSpec((1, m, d), lambda bi, i: (bi, 0, 0))
    mix_specs = [tokw(a.shape[2]) if a.ndim == 3 else full(a) for a in mix]
    return pl.pallas_call(
        functools.partial(_xattn_kernel, heads=heads, glu=len(mix) == 6),
        grid=(b, t // tm),
        in_specs=[tokw(d), full(gain), full(w_q), full(q_gain), mem, mem, full(w_o)] + mix_specs,
        out_specs=tokw(d),
        out_shape=jax.ShapeDtypeStruct((b, t, d), F32),
        compiler_params=_params(("parallel", "parallel"), 48),
        name="xattn",
    )(h, gain, w_q, q_gain, k, v, w_o, *mix)


def _ffn_dense_kernel(h_ref, g_ref, wg_ref, wu_ref, wd_ref, o_ref, xn_ref, acc_ref):
    j = pl.program_id(1)

    @pl.when(j == 0)
    def _():
        xn_ref[...] = _rms(h_ref[...], g_ref[...]).astype(BF16)
        acc_ref[...] = jnp.zeros_like(acc_ref)

    xn = xn_ref[...]
    act = (jax.nn.silu(_dot(xn, wg_ref[...])) * _dot(xn, wu_ref[...])).astype(BF16)
    acc_ref[...] += _dot(act, wd_ref[...])

    @pl.when(j == pl.num_programs(1) - 1)
    def _():
        o_ref[...] = h_ref[...] + acc_ref[...]


def _ffn_dense(h, gain, w_gate, w_up, w_down, tm, tf):
    n, d = h.shape
    f = w_gate.shape[1]
    tm, tf = min(tm, n), min(tf, f)
    return pl.pallas_call(
        _ffn_dense_kernel,
        grid=(n // tm, f // tf),
        in_specs=[pl.BlockSpec((tm, d), lambda i, j: (i, 0)),
                  pl.BlockSpec((1, d), lambda i, j: (0, 0)),
                  pl.BlockSpec((d, tf), lambda i, j: (0, j)),
                  pl.BlockSpec((d, tf), lambda i, j: (0, j)),
                  pl.BlockSpec((tf, d), lambda i, j: (j, 0))],
        out_specs=pl.BlockSpec((tm, d), lambda i, j: (i, 0)),
        out_shape=jax.ShapeDtypeStruct((n, d), F32),
        scratch_shapes=[pltpu.VMEM((tm, d), BF16), pltpu.VMEM((tm, d), F32)],
        compiler_params=_params(("parallel", "arbitrary"), 48),
        name="ffn_dense",
    )(h, gain, w_gate, w_up, w_down)


def _pack_words(x):
    bits = lax.bitcast_convert_type(x.astype(BF16).astype(F32), jnp.uint32)
    half = x.shape[1] // 2
    return (bits[:, :half] >> 16) | (bits[:, half:] & jnp.uint32(0xFFFF0000))


def _unpack_words(w):
    lo = lax.bitcast_convert_type(w << 16, F32)
    hi = lax.bitcast_convert_type(w & jnp.uint32(0xFFFF0000), F32)
    return jnp.concatenate([lo, hi], axis=-1)


def _ffn_expert_kernel(te_ref, nu_ref, x_ref, wg_ref, wu_ref, wd_ref, o_ref, xb_ref, acc_ref):
    i = pl.program_id(0)
    j = pl.program_id(1)

    @pl.when(i < nu_ref[0])
    def _():
        @pl.when(j == 0)
        def _():
            xb_ref[...] = _unpack_words(x_ref[...]).astype(BF16)
            acc_ref[...] = jnp.zeros_like(acc_ref)

        x = xb_ref[...]
        act = (jax.nn.silu(_dot(x, wg_ref[0].astype(BF16))) * _dot(x, wu_ref[0].astype(BF16))).astype(BF16)
        acc_ref[...] += _dot(act, wd_ref[0].astype(BF16))

    @pl.when(j == pl.num_programs(1) - 1)
    def _():
        o_ref[...] = _pack_words(acc_ref[...])


def _ffn_expert(tile_expert, n_used, xs, w_gate, w_up, w_down, tm, tf):
    p, words = xs.shape
    d = 2 * words
    f = w_gate.shape[2]
    tf = min(tf, f)
    nj = f // tf

    def col(i, j, te, nu):
        return (te[i], 0, jnp.where(i < nu[0], j, nj - 1))

    def rowb(i, j, te, nu):
        return (te[i], jnp.where(i < nu[0], j, nj - 1), 0)

    grid_spec = pltpu.PrefetchScalarGridSpec(
        num_scalar_prefetch=2,
        grid=(p // tm, nj),
        in_specs=[pl.BlockSpec((tm, words), lambda i, j, te, nu: (i, 0)),
                  pl.BlockSpec((1, d, tf), col),
                  pl.BlockSpec((1, d, tf), col),
                  pl.BlockSpec((1, tf, d), rowb)],
        out_specs=pl.BlockSpec((tm, words), lambda i, j, te, nu: (i, 0)),
        scratch_shapes=[pltpu.VMEM((tm, d), BF16), pltpu.VMEM((tm, d), F32)],
    )
    return pl.pallas_call(
        _ffn_expert_kernel,
        grid_spec=grid_spec,
        out_shape=jax.ShapeDtypeStruct((p, words), jnp.uint32),
        compiler_params=_params(("arbitrary", "arbitrary"), 48),
        name="ffn_expert",
    )(tile_expert, n_used, xs, w_gate, w_up, w_down)


def _qkv_kernel(x_ref, g_ref, w_ref, qkg_ref, qk_ref, v_ref, *, n_qk):
    xn = _rms(x_ref[...], g_ref[...]).astype(BF16)
    z = _dot(xn, w_ref[...])
    blk = 256
    ones = _group_ones(blk, DIFF_HEAD)
    for c in range(n_qk // blk):
        s = slice(c * blk, (c + 1) * blk)
        zc = z[:, s]
        ms = _dot((zc * zc).astype(BF16), ones) * (1.0 / DIFF_HEAD)
        qk_ref[:, s] = (zc * lax.rsqrt(ms + NORM_EPS) * qkg_ref[:, s]).astype(qk_ref.dtype)
    v_ref[...] = z[:, n_qk:].astype(v_ref.dtype)


def _qkv(x, gain, w, qk_gain, tm):
    n, d = x.shape
    c = w.shape[1]
    n_qk = qk_gain.shape[1]
    tm = min(tm, n)
    return pl.pallas_call(
        functools.partial(_qkv_kernel, n_qk=n_qk),
        grid=(n // tm,),
        in_specs=[pl.BlockSpec((tm, d), lambda i: (i, 0)),
                  pl.BlockSpec((1, d), lambda i: (0, 0)),
                  pl.BlockSpec((d, c), lambda i: (0, 0)),
                  pl.BlockSpec((1, n_qk), lambda i: (0, 0))],
        out_specs=[pl.BlockSpec((tm, n_qk), lambda i: (i, 0)),
                   pl.BlockSpec((tm, c - n_qk), lambda i: (i, 0))],
        out_shape=[jax.ShapeDtypeStruct((n, n_qk), BF16),
                   jax.ShapeDtypeStruct((n, c - n_qk), BF16)],
        compiler_params=_params(("parallel",), 48),
        name="qkv",
    )(x, gain, w, qk_gain)


def _diff_attn_kernel(ti_ref, tj_ref, slope_ref, lam_ref, q_ref, kc_ref, kn_ref, v_ref, sg_ref, o_ref,
                      qs_ref, s0_ref, s1_ref, p_ref, m_ref, l_ref, acc_ref, dg_ref, *, out_scale, rb):
    h = pl.program_id(1)
    step = pl.program_id(2)
    i = ti_ref[step]
    j = tj_ref[step]
    tq = q_ref.shape[1]
    tk = kc_ref.shape[1]
    nl = tk // LANES
    slope = slope_ref[h]
    slot = lax.rem(j, 2)

    @pl.when(j == 0)
    def _():
        q = q_ref[0]
        lane = lax.broadcasted_iota(jnp.int32, q.shape, 1)
        zero = jnp.zeros_like(q)
        qs_ref[0:tq, :] = jnp.where(lane < DIFF_HEAD, q, zero)
        qs_ref[tq:2 * tq, :] = jnp.where(lane < DIFF_HEAD, zero, q)
        m_ref[...] = jnp.full_like(m_ref, NEG_INF)
        l_ref[...] = jnp.zeros_like(l_ref)
        acc_ref[...] = jnp.zeros_like(acc_ref)
        s0_ref[...] = _dot_nt(qs_ref[...], kc_ref[0])

    @pl.when(step == 0)
    def _():
        r = lax.broadcasted_iota(jnp.int32, (tq, tk), 0)
        c = lax.broadcasted_iota(jnp.int32, (tq, tk), 1)
        ahead = jnp.maximum(c - r, 0).astype(F32) * (-2.0 * slope)
        dg_ref[...] = jnp.where(c // ATTN_CHUNK <= r // ATTN_CHUNK, ahead, NEG_INF)

    col = lax.broadcasted_iota(jnp.int32, (1, tk), 1)
    key_bias = slope * (col + (j - i) * tq).astype(F32)

    def update(diag, cur_ref):
        for r0 in range(0, 2 * tq, rb):
            rows = slice(r0, r0 + rb)
            s = cur_ref[rows, :] + key_bias
            if diag:
                d0 = r0 % tq
                s = s + dg_ref[d0:d0 + rb, :]
            blk = s[:, 0:LANES]
            for c in range(1, nl):
                blk = jnp.maximum(blk, s[:, c * LANES:(c + 1) * LANES])
            m_prev = m_ref[rows, :]
            m_new = jnp.maximum(m_prev, jnp.max(blk, axis=-1, keepdims=True))
            corr = jnp.exp2(m_prev - m_new)
            p = jnp.exp2(s - jnp.concatenate([m_new] * nl, axis=1))
            part = p[:, 0:LANES]
            for c in range(1, nl):
                part = part + p[:, c * LANES:(c + 1) * LANES]
            l_ref[rows, :] = corr * l_ref[rows, :] + part
            acc_ref[rows, :] = corr * acc_ref[rows, :]
            m_ref[rows, :] = m_new
            p_ref[rows, :] = p.astype(BF16)
        v = v_ref[0]
        for half in range(2):
            rows = slice(half * tq, (half + 1) * tq)
            acc_ref[rows, :] += _dot(p_ref[rows, :], v)

    for parity, (cur_ref, nxt_ref) in enumerate(((s0_ref, s1_ref), (s1_ref, s0_ref))):
        @pl.when(jnp.logical_and(j < i, slot == parity))
        def _(cur_ref=cur_ref, nxt_ref=nxt_ref):
            nxt_ref[...] = _dot_nt(qs_ref[...], kn_ref[0])
            update(False, cur_ref)

        @pl.when(jnp.logical_and(j == i, slot == parity))
        def _(cur_ref=cur_ref):
            update(True, cur_ref)

    @pl.when(j == i)
    def _():
        o = acc_ref[...] / jnp.sum(l_ref[...], axis=-1, keepdims=True)
        o = o[:tq] - lam_ref[0] * o[tq:]
        o = o * lax.rsqrt(jnp.mean(o * o, axis=-1, keepdims=True) + NORM_EPS) * sg_ref[...]
        o_ref[0] = (o * out_scale).astype(o_ref.dtype)


def _diff_attn(slopes, lam, qk, v, sub_gain, heads, out_scale, tq, rb):
    b, t, _ = v.shape
    hw = 2 * DIFF_HEAD
    tq = min(tq, t)
    rb = min(rb, tq)
    nt = t // tq
    pairs = [(i, j) for i in range(nt) for j in range(i + 1)]
    ti = jnp.asarray([p[0] for p in pairs], jnp.int32)
    tj = jnp.asarray([p[1] for p in pairs], jnp.int32)
    smem = pl.BlockSpec(memory_space=pltpu.SMEM)
    grid_spec = pltpu.PrefetchScalarGridSpec(
        num_scalar_prefetch=2,
        grid=(b, heads, len(pairs)),
        in_specs=[smem, smem,
                  pl.BlockSpec((1, tq, hw), lambda bi, h, s, ti, tj: (bi, ti[s], h)),
                  pl.BlockSpec((1, tq, hw), lambda bi, h, s, ti, tj: (bi, tj[s], heads + h)),
                  pl.BlockSpec((1, tq, hw),
                               lambda bi, h, s, ti, tj: (bi, jnp.minimum(tj[s] + 1, ti[s]), heads + h)),
                  pl.BlockSpec((1, tq, hw), lambda bi, h, s, ti, tj: (bi, tj[s], h)),
                  pl.BlockSpec((1, hw), lambda bi, h, s, ti, tj: (0, 0))],
        out_specs=pl.BlockSpec((1, tq, hw), lambda bi, h, s, ti, tj: (bi, ti[s], h)),
        scratch_shapes=[pltpu.VMEM((2 * tq, hw), BF16),
                        pltpu.VMEM((2 * tq, tq), F32),
                        pltpu.VMEM((2 * tq, tq), F32),
                        pltpu.VMEM((2 * tq, tq), BF16),
                        pltpu.VMEM((2 * tq, LANES), F32),
                        pltpu.VMEM((2 * tq, LANES), F32),
                        pltpu.VMEM((2 * tq, hw), F32),
                        pltpu.VMEM((tq, tq), F32)],
    )
    return pl.pallas_call(
        functools.partial(_diff_attn_kernel, out_scale=out_scale, rb=rb),
        grid_spec=grid_spec,
        out_shape=jax.ShapeDtypeStruct((b, t, heads * hw), BF16),
        compiler_params=_params(("arbitrary", "arbitrary", "arbitrary"), 48),
        name="diff_attn",
    )(ti, tj, slopes, lam, qk, qk, qk, v, sub_gain)


def _router_kernel(h_ref, g_ref, w_ref, b_ref, xn_ref, r_ref):
    xn = _rms(h_ref[...], g_ref[...])
    xn_ref[...] = _pack_words(xn)
    logits = jnp.dot(xn, w_ref[...], preferred_element_type=F32,
                     precision=lax.Precision.HIGHEST) + b_ref[...]
    lane = lax.broadcasted_iota(jnp.int32, logits.shape, 1).astype(F32)
    big = float(ROUTER_LANES)
    l1 = jnp.max(logits, axis=-1, keepdims=True)
    i1 = jnp.min(jnp.where(logits == l1, lane, big), axis=-1, keepdims=True)
    rest = jnp.where(lane == i1, NEG_INF, logits)
    l2 = jnp.max(rest, axis=-1, keepdims=True)
    i2 = jnp.min(jnp.where(rest == l2, lane, big), axis=-1, keepdims=True)
    e = jnp.exp(l2 - l1)
    w1 = 1.0 / (1.0 + e)
    w2 = e / (1.0 + e)
    r_ref[...] = jnp.where(lane == 0.0, i1, jnp.where(lane == 1.0, i2,
                           jnp.where(lane == 2.0, w1, jnp.where(lane == 3.0, w2, 0.0))))


def _router(h, gain, w_pad, b_pad, tm):
    n, d = h.shape
    tm = min(tm, n)
    return pl.pallas_call(
        _router_kernel,
        grid=(n // tm,),
        in_specs=[pl.BlockSpec((tm, d), lambda i: (i, 0)),
                  pl.BlockSpec((1, d), lambda i: (0, 0)),
                  pl.BlockSpec(w_pad.shape, lambda i: (0, 0)),
                  pl.BlockSpec(b_pad.shape, lambda i: (0, 0))],
        out_specs=[pl.BlockSpec((tm, d // 2), lambda i: (i, 0)),
                   pl.BlockSpec((tm, ROUTER_LANES), lambda i: (i, 0))],
        out_shape=[jax.ShapeDtypeStruct((n, d // 2), jnp.uint32),
                   jax.ShapeDtypeStruct((n, ROUTER_LANES), F32)],
        compiler_params=_params(("parallel",), 32),
        name="router",
    )(h, gain, w_pad, b_pad)


GATHER_UNROLL = 8


def _scatter_kernel(pos_ref, x_ref, init_hbm, o_hbm, sem):
    del init_hbm
    rows = x_ref.shape[0]
    base = pl.program_id(0) * rows

    def start(r, c):
        pltpu.make_async_copy(x_ref.at[pl.ds(r, 1)], o_hbm.at[pl.ds(pos_ref[base + r], 1)], sem).start()
        return c

    lax.fori_loop(0, rows, start, 0, unroll=GATHER_UNROLL)
    pltpu.make_async_copy(x_ref, o_hbm.at[pl.ds(0, rows)], sem).wait()


def _scatter_rows(pos, x, cap, tg):
    a = pos.shape[0]
    n, w = x.shape
    tg = min(tg, n)
    assert n % tg == 0 and a % n == 0 and tg % GATHER_UNROLL == 0
    nblk = n // tg
    grid_spec = pltpu.PrefetchScalarGridSpec(
        num_scalar_prefetch=1,
        grid=(a // tg,),
        in_specs=[pl.BlockSpec((tg, w), lambda i, pos_ref: (lax.rem(i, nblk), 0)),
                  pl.BlockSpec(memory_space=pl.ANY)],
        out_specs=pl.BlockSpec(memory_space=pl.ANY),
        scratch_shapes=[pltpu.SemaphoreType.DMA(())],
    )
    return pl.pallas_call(
        _scatter_kernel,
        grid_spec=grid_spec,
        out_shape=jax.ShapeDtypeStruct((cap, w), x.dtype),
        input_output_aliases={2: 0},
        compiler_params=_params(("arbitrary",), 32),
        name="scatter_rows",
    )(pos, x, jnp.zeros((cap, w), x.dtype))


def _combine_kernel(pos_ref, h_ref, r_ref, y_hbm, o_ref, y0_ref, y1_ref, sem, *, n):
    rows = h_ref.shape[0]
    base = pl.program_id(0) * rows

    def start(r, c):
        pltpu.make_async_copy(y_hbm.at[pl.ds(pos_ref[base + r], 1)], y0_ref.at[pl.ds(r, 1)], sem).start()
        pltpu.make_async_copy(y_hbm.at[pl.ds(pos_ref[n + base + r], 1)], y1_ref.at[pl.ds(r, 1)], sem).start()
        return c

    lax.fori_loop(0, rows, start, 0, unroll=GATHER_UNROLL)
    pltpu.make_async_copy(y_hbm.at[pl.ds(0, rows)], y0_ref, sem).wait()
    pltpu.make_async_copy(y_hbm.at[pl.ds(0, rows)], y1_ref, sem).wait()
    r = r_ref[...]
    w0 = r[:, 2:3]
    w1 = r[:, 3:4]
    o_ref[...] = h_ref[...] + w0 * _unpack_words(y0_ref[...]) + w1 * _unpack_words(y1_ref[...])


def _combine(pos, h, route, ys, tm):
    n, d = h.shape
    tm = min(tm, n)
    assert n % tm == 0 and tm % GATHER_UNROLL == 0
    grid_spec = pltpu.PrefetchScalarGridSpec(
        num_scalar_prefetch=1,
        grid=(n // tm,),
        in_specs=[pl.BlockSpec((tm, d), lambda i, pos_ref: (i, 0)),
                  pl.BlockSpec((tm, ROUTER_LANES), lambda i, pos_ref: (i, 0)),
                  pl.BlockSpec(memory_space=pl.ANY)],
        out_specs=pl.BlockSpec((tm, d), lambda i, pos_ref: (i, 0)),
        scratch_shapes=[pltpu.VMEM((tm, d // 2), jnp.uint32),
                        pltpu.VMEM((tm, d // 2), jnp.uint32),
                        pltpu.SemaphoreType.DMA(())],
    )
    return pl.pallas_call(
        functools.partial(_combine_kernel, n=n),
        grid_spec=grid_spec,
        out_shape=jax.ShapeDtypeStruct((n, d), F32),
        compiler_params=_params(("arbitrary",), 32),
        name="combine",
    )(pos, h, route, ys)


def _moe(h, gain, w_router, b_router, w_gate, w_up, w_down, tm, tf):
    n, d = h.shape
    e = w_gate.shape[0]
    w_pad = jnp.zeros((d, ROUTER_LANES), F32).at[:, :e].set(w_router.astype(F32))
    b_pad = jnp.full((1, ROUTER_LANES), NEG_INF, F32).at[0, :e].set(b_router.astype(F32))
    xn, route = _router(h, gain, w_pad, b_pad, 512)

    tm = min(tm, n)
    experts = jnp.concatenate([route[:, 0], route[:, 1]]).astype(jnp.int32)
    onehot = (experts[:, None] == jnp.arange(e, dtype=jnp.int32)[None, :]).astype(jnp.int32)
    counts = jnp.sum(onehot, axis=0)
    padded = ((counts + tm - 1) // tm) * tm
    ends = jnp.cumsum(padded)
    starts = ends - padded
    pos = jnp.sum(onehot * (jnp.cumsum(onehot, axis=0) - onehot + starts[None, :]), axis=1)
    pos = pos.astype(jnp.int32)
    cap = 2 * n + e * tm
    tile_start = jnp.arange(cap // tm, dtype=jnp.int32) * tm
    tile_expert = jnp.minimum(jnp.sum((tile_start[:, None] >= ends[None, :]).astype(jnp.int32), axis=1),
                              e - 1)
    n_used = (ends[-1:] // tm).astype(jnp.int32)

    xs = _scatter_rows(pos, xn, cap, 2048)
    ys = _ffn_expert(tile_expert, n_used, xs, w_gate, w_up, w_down, tm, tf)
    return _combine(pos, h, route, ys, 1024)


def _row(v):
    return v.astype(F32).reshape(1, -1)


def _memory_block(h, mix, mem, norm_x, norm_m, w_q, w_kv, q_gain, k_gain, w_o):
    k, v = _mem_kv(mem, _row(norm_m), w_kv.astype(BF16), _row(k_gain), MEM_HEADS)
    return _xattn(h, _row(norm_x), w_q.astype(BF16), _row(q_gain), k, v, w_o.astype(BF16),
                  mix, MEM_HEADS, 512)


def _even_mixer(h, norm, hy_w_in, rw_mu, rw_w0, rw_w_up, rw_a0, rw_a_up, rw_g_up, rw_k_k, rw_k_a,
                rw_r_k, rw_ln_w, rw_ln_b, s5_lam_re, s5_lam_im, s5_log_dt, s5_b_re, s5_b_im,
                s5_c_re, s5_c_im, s5_d, s5_w_glu, s5_b_glu, hy_w_out):
    b, t, d = h.shape
    width = rw_w0.shape[0]
    n_lora = DECAY_LORA + ICLR_LORA + GATE_LORA
    rwkv_cols = 3 * width + n_lora
    n_r = 3 * width + LORA_PAD
    pad = LORA_PAD - n_lora
    w_in = jnp.concatenate([hy_w_in[:, :rwkv_cols], jnp.zeros((d, pad), hy_w_in.dtype),
                            hy_w_in[:, rwkv_cols:]], axis=1).astype(BF16)
    mu = jnp.concatenate([rw_mu.astype(F32), jnp.zeros((pad,), F32)]).reshape(1, -1)
    w_lora = jnp.zeros((LORA_PAD, 3 * width), F32)
    w_lora = w_lora.at[:DECAY_LORA, :width].set(rw_w_up.astype(F32))
    w_lora = w_lora.at[DECAY_LORA:DECAY_LORA + ICLR_LORA, width:2 * width].set(rw_a_up.astype(F32))
    w_lora = w_lora.at[DECAY_LORA + ICLR_LORA:n_lora, 2 * width:].set(rw_g_up.astype(F32))

    *parts, u = _rwkv_prep(h, _row(norm), w_in, mu, w_lora.astype(BF16), _row(rw_w0), _row(rw_a0),
                           _row(rw_k_k), _row(rw_k_a), width, 512)
    ya = _rwkv_chunk(*parts, _row(rw_r_k), _row(rw_ln_w), _row(rw_ln_b), width // RWKV_HEAD, 2)

    bblk, cblk, lam_r, lam_i = _s5_tables(s5_lam_re, s5_lam_im, s5_log_dt, s5_b_re, s5_b_im,
                                          s5_c_re, s5_c_im)
    yb = _s5(u.reshape(b, t, -1), bblk, cblk, lam_r, lam_i, _row(s5_d))

    w_out = hy_w_out.astype(BF16)
    return (ya, yb, s5_w_glu.astype(BF16), _row(s5_b_glu), w_out[:width], w_out[width:])


def _odd_mixer(h, norm, layer, w_qkv, q_gain, k_gain, lam_q1, lam_k1, lam_q2, lam_k2, sub_gain, w_o):
    b, t, d = h.shape
    heads = w_o.shape[0] // (2 * DIFF_HEAD)
    lambda_init = 0.8 - 0.6 * math.exp(-0.3 * layer)
    lam = (jnp.exp(jnp.sum(lam_q1.astype(F32) * lam_k1.astype(F32)))
           - jnp.exp(jnp.sum(lam_q2.astype(F32) * lam_k2.astype(F32))) + lambda_init).reshape(1)
    log2e = math.log2(math.e)
    slopes = 2.0 ** (-8.0 * jnp.arange(1, heads + 1, dtype=F32) / heads) * log2e
    n_half = heads * 2 * DIFF_HEAD
    qk_gain = jnp.concatenate([jnp.tile(q_gain.astype(F32), n_half // DIFF_HEAD) * (DIFF_HEAD ** -0.5 * log2e),
                               jnp.tile(k_gain.astype(F32), n_half // DIFF_HEAD)]).reshape(1, -1)
    h2 = h.reshape(b * t, d)
    qk, v = _qkv(h2, _row(norm), w_qkv.astype(BF16), qk_gain, 512)
    o = _diff_attn(slopes, lam, qk.reshape(b, t, -1), v.reshape(b, t, -1), _row(sub_gain),
                   heads, 1.0 - lambda_init, 1024, 128)
    return (o, w_o.astype(BF16))


def kernel(x, mem, norm_mix, norm_xattn, norm_mem, norm_ffn, xa_w_q, xa_w_kv, xa_q_gain, xa_k_gain, xa_w_o, hy_w_in, rw_mu, rw_w0, rw_w_up, rw_a0, rw_a_up, rw_g_up, rw_k_k, rw_k_a, rw_r_k, rw_ln_w, rw_ln_b, s5_lam_re, s5_lam_im, s5_log_dt, s5_b_re, s5_b_im, s5_c_re, s5_c_im, s5_d, s5_w_glu, s5_b_glu, hy_w_out, ff_w_gate, ff_w_up, ff_w_down, da_w_qkv, da_q_gain, da_k_gain, da_lam_q1, da_lam_k1, da_lam_q2, da_lam_k2, da_sub_gain, da_w_o, moe_w_router, moe_b_router, moe_w_gate, moe_w_up, moe_w_down):
    depth = norm_mix.shape[0]
    b, t, d = x.shape
    h = x
    for l in range(depth):
        i = l // 2
        if l % 2 == 0:
            mix = _even_mixer(h, norm_mix[l], hy_w_in[i], rw_mu[i], rw_w0[i], rw_w_up[i], rw_a0[i],
                            rw_a_up[i], rw_g_up[i], rw_k_k[i], rw_k_a[i], rw_r_k[i], rw_ln_w[i],
                            rw_ln_b[i], s5_lam_re[i], s5_lam_im[i], s5_log_dt[i], s5_b_re[i],
                            s5_b_im[i], s5_c_re[i], s5_c_im[i], s5_d[i], s5_w_glu[i], s5_b_glu[i],
                            hy_w_out[i])
        else:
            mix = _odd_mixer(h, norm_mix[l], l, da_w_qkv[i], da_q_gain[i], da_k_gain[i], da_lam_q1[i],
                             da_lam_k1[i], da_lam_q2[i], da_lam_k2[i], da_sub_gain[i], da_w_o[i])
        h = _memory_block(h, mix, mem, norm_xattn[l], norm_mem[l], xa_w_q[l], xa_w_kv[l], xa_q_gain[l],
                          xa_k_gain[l], xa_w_o[l])
        h2 = h.reshape(b * t, d)
        if l % 2 == 0:
            h2 = _ffn_dense(h2, _row(norm_ffn[l]), ff_w_gate[i].astype(BF16), ff_w_up[i].astype(BF16),
                            ff_w_down[i].astype(BF16), 512, 1408)
        else:
            h2 = _moe(h2, _row(norm_ffn[l]), moe_w_router[i], moe_b_router[i],
                      moe_w_gate[i], moe_w_up[i], moe_w_down[i], 1024, 512)
        h = h2.reshape(b, t, d)
    return h
```

```python
import functools
import math

import jax
import jax.numpy as jnp
from jax import lax
from jax.experimental import pallas as pl
from jax.experimental.pallas import tpu as pltpu

F32 = jnp.float32
BF16 = jnp.bfloat16

NORM_EPS = 1e-6
NEG_INF = -1e30
ATTN_CHUNK = 64

RWKV_HEAD = 64
RWKV_CHUNK = 64
DECAY_LORA = 64
ICLR_LORA = 64
GATE_LORA = 160
LORA_PAD = 384
RWKV_GN_EPS = 1e-5 * RWKV_HEAD

S5_GROUP = 16
S5_STATE = 64
S5_TILE_GROUPS = 8

DIFF_HEAD = 64
MEM_HEADS = 4
N_EXPERTS = 8
LANES = 128
ROUTER_LANES = LANES

V7X_VMEM_BYTES = 64 * 1024 * 1024


def _params(semantics, vmem_mib):
    assert vmem_mib * 1024 * 1024 < V7X_VMEM_BYTES
    return pltpu.CompilerParams(dimension_semantics=semantics,
                                vmem_limit_bytes=vmem_mib * 1024 * 1024)


def _rms(x, gain):
    return x * lax.rsqrt(jnp.mean(x * x, axis=-1, keepdims=True) + NORM_EPS) * gain


def _dot(a, b):
    return jnp.dot(a, b, preferred_element_type=F32)


def _dot_nt(a, b):
    return lax.dot_general(a, b, (((1,), (1,)), ((), ())), preferred_element_type=F32)


def _dot_tn(a, b):
    return lax.dot_general(a, b, (((0,), (0,)), ((), ())), preferred_element_type=F32)


def _group_ones(width, group):
    r = lax.broadcasted_iota(jnp.int32, (width, width), 0) // group
    c = lax.broadcasted_iota(jnp.int32, (width, width), 1) // group
    return jnp.where(r == c, 1.0, 0.0).astype(BF16)


def _group_sum(x, ones):
    hi = x.astype(BF16)
    lo = (x - hi.astype(F32)).astype(BF16)
    return _dot(hi, ones) + _dot(lo, ones)


def _rwkv_prep_kernel(x_ref, gain_ref, win_ref, mu_ref, wl_ref, w0_ref, a0_ref, kk_ref, ka_ref,
                      r_ref, lw_ref, k_ref, v_ref, kn_ref, a_ref, g_ref, u_ref, carry_ref, *, width):
    i = pl.program_id(1)

    @pl.when(i == 0)
    def _():
        carry_ref[...] = jnp.zeros_like(carry_ref)

    xn = _rms(x_ref[0], gain_ref[...]).astype(BF16)
    zall = _dot(xn, win_ref[...])
    n_r = mu_ref.shape[1]
    u_ref[0] = zall[:, n_r:].astype(u_ref.dtype)
    z = zall[:, :n_r]
    rows = lax.broadcasted_iota(jnp.int32, z.shape, 0)
    prev = jnp.where(rows == 0, carry_ref[...], pltpu.roll(z, 1, axis=0))
    carry_ref[...] = z[z.shape[0] - 1:, :]
    zs = z + (prev - z) * mu_ref[...]

    w = width
    r = zs[:, :w]
    k = zs[:, w:2 * w]
    v = zs[:, 2 * w:3 * w]
    lo = zs[:, 3 * w:]
    lane = lax.broadcasted_iota(jnp.int32, lo.shape, 1)
    lo = jnp.where(lane < DECAY_LORA, jnp.tanh(lo),
                   jnp.where(lane < DECAY_LORA + ICLR_LORA, lo, jax.nn.sigmoid(lo)))
    m = _dot(lo.astype(BF16), wl_ref[...])
    w_log = -jax.nn.softplus(-(w0_ref[...] + m[:, :w])) - 0.5
    a = jax.nn.sigmoid(a0_ref[...] + m[:, w:2 * w])
    kn = k * kk_ref[...]
    ss = _group_sum(kn * kn, _group_ones(w, RWKV_HEAD))
    kn = kn * lax.rsqrt(jnp.maximum(ss, 1e-24))

    r_ref[0] = r
    lw_ref[0] = -jnp.exp(w_log)
    k_ref[0] = k * (1.0 + (a - 1.0) * ka_ref[...])
    v_ref[0] = v
    kn_ref[0] = kn
    a_ref[0] = a
    g_ref[0] = m[:, 2 * w:]


def _rwkv_prep(x, gain, w_in, mu, wl, w0, a0, k_k, k_a, width, tm):
    b, t, d = x.shape
    c = mu.shape[1]
    n_u = w_in.shape[1] - c
    tm = min(tm, t)
    row = lambda n: pl.BlockSpec((1, n), lambda bi, i: (0, 0))
    full = lambda a: pl.BlockSpec(a.shape, lambda bi, i: (0, 0))
    out = lambda n: pl.BlockSpec((1, tm, n), lambda bi, i: (bi, i, 0))
    return pl.pallas_call(
        functools.partial(_rwkv_prep_kernel, width=width),
        grid=(b, t // tm),
        in_specs=[pl.BlockSpec((1, tm, d), lambda bi, i: (bi, i, 0)),
                  row(d), full(w_in), row(c), full(wl),
                  row(width), row(width), row(width), row(width)],
        out_specs=[out(width)] * 7 + [out(n_u)],
        out_shape=[jax.ShapeDtypeStruct((b, t, width), F32)] * 7 + [jax.ShapeDtypeStruct((b, t, n_u), BF16)],
        scratch_shapes=[pltpu.VMEM((1, c), F32)],
        compiler_params=_params(("parallel", "arbitrary"), 48),
        name="rwkv_prep",
    )(x, gain, w_in, mu, wl, w0, a0, k_k, k_a)


def _rwkv_chunk_kernel(r_ref, lw_ref, k_ref, v_ref, kn_ref, a_ref, g_ref,
                       rk_ref, lnw_ref, lnb_ref, o_ref, st_ref, *, heads, chunks, L):
    @pl.when(pl.program_id(2) == 0)
    def _():
        st_ref[...] = jnp.zeros_like(st_ref)

    n = RWKV_HEAD
    ri = lax.broadcasted_iota(jnp.int32, (L, L), 0)
    ci = lax.broadcasted_iota(jnp.int32, (L, L), 1)
    tri = jnp.where(ci <= ri, 1.0, 0.0).astype(BF16)
    strict = ci < ri
    incl = ci <= ri
    eye = jnp.where(ci == ri, 1.0, 0.0).astype(F32)
    rk_all, lnw_all, lnb_all = rk_ref[...], lnw_ref[...], lnb_ref[...]
    inv_steps = max(1, int(math.ceil(math.log2(L))) - 1)

    items = [(c, h) for c in range(chunks) for h in range(heads)]
    fs = []
    for c, h in items:
        rows = slice(c * L, (c + 1) * L)
        s = slice(h * n, (h + 1) * n)
        r, k, v = r_ref[0, rows, s], k_ref[0, rows, s], v_ref[0, rows, s]
        kn, a, lw = kn_ref[0, rows, s], a_ref[0, rows, s], lw_ref[0, rows, s]
        hi = lw.astype(BF16)
        lo = (lw - hi.astype(F32)).astype(BF16)
        fs.append(dict(r=r, k=k, v=v, lw=lw, alpha=-kn, beta=kn * a, v_b=v.astype(BF16),
                       cum=_dot(tri, hi) + _dot(tri, lo)))
    for f in fs:
        cum, lw = f["cum"], f["lw"]
        cum_ex = cum - lw
        c_mid = cum[L // 2 - 1:L // 2, :]
        c_end = cum[L - 1:L, :]
        e_inv = jnp.exp(c_mid - cum)
        e_end = jnp.exp(c_end - cum)
        lhs = jnp.concatenate([f["alpha"] * jnp.exp(cum_ex - c_mid), f["r"] * jnp.exp(cum - c_mid)],
                              axis=0).astype(BF16)
        f["m_b"] = _dot_nt(lhs, (f["beta"] * e_inv).astype(BF16))
        f["m_k"] = _dot_nt(lhs, (f["k"] * e_inv).astype(BF16))
        f["abs_lhs"] = jnp.concatenate([f["alpha"] * jnp.exp(cum_ex), f["r"] * jnp.exp(cum)],
                                       axis=0).astype(BF16)
        f["b_end"] = (f["beta"] * e_end).astype(BF16)
        f["k_end"] = (f["k"] * e_end).astype(BF16)
        f["decay"] = jnp.exp(c_end)
    for f in fs:
        m_ab = jnp.where(strict, f["m_b"][:L], 0.0)
        f["m_rb"] = jnp.where(incl, f["m_b"][L:], 0.0).astype(BF16)
        f["ak_v"] = _dot(jnp.where(strict, f["m_k"][:L], 0.0).astype(BF16), f["v_b"])
        f["rk_v"] = _dot(jnp.where(incl, f["m_k"][L:], 0.0).astype(BF16), f["v_b"])
        f["inv"] = eye + m_ab
        f["p"] = m_ab
    for _ in range(inv_steps):
        for f in fs:
            pb = f["p"].astype(BF16)
            f["p"] = _dot(pb, pb)
        for f in fs:
            f["inv"] = f["inv"] + _dot(f["inv"].astype(BF16), f["p"].astype(BF16))
    for f in fs:
        f["inv"] = f["inv"].astype(BF16)

    st = [st_ref[h] for h in range(heads)]
    for c in range(chunks):
        cf = fs[c * heads:(c + 1) * heads]
        from_state = [_dot_nt(f["abs_lhs"], st[h].astype(BF16)) for h, f in enumerate(cf)]
        u_b = [_dot(f["inv"], (from_state[h][:L] + f["ak_v"]).astype(BF16)).astype(BF16)
               for h, f in enumerate(cf)]
        st = [st[h] * f["decay"] + _dot_tn(u_b[h], f["b_end"]) + _dot_tn(f["v_b"], f["k_end"])
              for h, f in enumerate(cf)]
        for h, f in enumerate(cf):
            rows = slice(c * L, (c + 1) * L)
            s = slice(h * n, (h + 1) * n)
            o = from_state[h][L:] + _dot(f["m_rb"], u_b[h]) + f["rk_v"]
            mean = jnp.mean(o, axis=-1, keepdims=True)
            d = o - mean
            var = jnp.mean(d * d, axis=-1, keepdims=True)
            o = d * lax.rsqrt(var + RWKV_GN_EPS) * lnw_all[:, s] + lnb_all[:, s]
            bonus = jnp.sum(f["r"] * f["k"] * rk_all[:, s], axis=-1, keepdims=True) * f["v"]
            o_ref[0, rows, s] = ((o + bonus) * g_ref[0, rows, s]).astype(o_ref.dtype)
    for h in range(heads):
        st_ref[h] = st[h]


def _rwkv_chunk(r, lw, k, v, kn, a, g, r_k, ln_w, ln_b, heads_per_step, chunks_per_step):
    b, t, width = r.shape
    L = min(RWKV_CHUNK, t)
    chunks = min(chunks_per_step, t // L)
    lanes = heads_per_step * RWKV_HEAD
    seq = pl.BlockSpec((1, chunks * L, lanes), lambda bi, hi, ci: (bi, ci, hi))
    row = pl.BlockSpec((1, lanes), lambda bi, hi, ci: (0, hi))
    return pl.pallas_call(
        functools.partial(_rwkv_chunk_kernel, heads=heads_per_step, chunks=chunks, L=L),
        grid=(b, width // lanes, t // (chunks * L)),
        in_specs=[seq] * 7 + [row] * 3,
        out_specs=seq,
        out_shape=jax.ShapeDtypeStruct((b, t, width), BF16),
        scratch_shapes=[pltpu.VMEM((heads_per_step, RWKV_HEAD, RWKV_HEAD), F32)],
        compiler_params=_params(("parallel", "parallel", "arbitrary"), 32),
        name="rwkv_chunk",
    )(r, lw, k, v, kn, a, g, r_k, ln_w, ln_b)


S5_ROWS = 8


def _gelu_tanh(x):
    return 0.5 * x * (1.0 + jnp.tanh(math.sqrt(2.0 / math.pi) * (x + 0.044715 * (x * x * x))))


def _s5_kernel(u_ref, bb_ref, cb_ref, lr_ref, li_ref, d_ref, y_ref, x_ref, *, unroll, y_rows):
    half = lr_ref.shape[-1]
    t = u_ref.shape[1]
    nb = half // LANES
    lr = lr_ref[0]
    li = li_ref[0]

    for cblk in range(2 * nb):
        x_ref[cblk] = _dot(u_ref[0], bb_ref[0, :, cblk * LANES:(cblk + 1) * LANES])

    pows = [(lr, li)]
    for _ in range(S5_ROWS - 1):
        qr, qi = pows[-1]
        pows.append((qr * lr - qi * li, qr * li + qi * lr))
    pw_r = jnp.concatenate([q[0] for q in pows], axis=0)
    pw_i = jnp.concatenate([q[1] for q in pows], axis=0)
    row = lax.broadcasted_iota(jnp.int32, (S5_ROWS, half), 0)
    shifts = []
    k = 1
    while k < S5_ROWS:
        shifts.append((k, jnp.where(row >= k, pows[k - 1][0], 0.0), jnp.where(row >= k, pows[k - 1][1], 0.0)))
        k *= 2

    def tile_scan(ti, carry):
        rows = pl.ds(pl.multiple_of(ti * S5_ROWS, S5_ROWS), S5_ROWS)
        out = []
        for cblk in range(nb):
            lanes = slice(cblk * LANES, (cblk + 1) * LANES)
            xr = x_ref[cblk, rows, :]
            xi = x_ref[nb + cblk, rows, :]
            for k, ar, ai in shifts:
                sr = pltpu.roll(xr, k, axis=0)
                si = pltpu.roll(xi, k, axis=0)
                xr, xi = (xr + (ar[:, lanes] * sr - ai[:, lanes] * si),
                          xi + (ar[:, lanes] * si + ai[:, lanes] * sr))
            cr, ci = carry[cblk]
            xr, xi = (xr + (pw_r[:, lanes] * cr - pw_i[:, lanes] * ci),
                      xi + (pw_r[:, lanes] * ci + pw_i[:, lanes] * cr))
            x_ref[cblk, rows, :] = xr
            x_ref[nb + cblk, rows, :] = xi
            out.append((jnp.broadcast_to(xr[S5_ROWS - 1:, :], (S5_ROWS, LANES)),
                        jnp.broadcast_to(xi[S5_ROWS - 1:, :], (S5_ROWS, LANES))))
        return tuple(out)

    zero = jnp.zeros((S5_ROWS, LANES), F32)
    lax.fori_loop(0, t // S5_ROWS, tile_scan, tuple((zero, zero) for _ in range(nb)), unroll=unroll)

    d = d_ref[...]
    cb = cb_ref[0]
    for r0 in range(0, t, y_rows):
        rows = slice(r0, r0 + y_rows)
        x = jnp.concatenate([x_ref[cblk, rows, :] for cblk in range(2 * nb)], axis=-1).astype(BF16)
        y = _dot(x, cb) + d * u_ref[0, rows, :].astype(F32)
        y_ref[0, rows, :] = _gelu_tanh(y).astype(y_ref.dtype)


def _s5(u, bblk, cblk, lam_r, lam_i, d_skip):
    b, t, width = u.shape
    lanes = S5_TILE_GROUPS * S5_GROUP
    half = S5_TILE_GROUPS * S5_STATE
    tile = lambda shape: pl.BlockSpec((1,) + shape, lambda bi, ci: (ci, 0, 0))
    return pl.pallas_call(
        functools.partial(_s5_kernel, unroll=min(4, t // S5_ROWS), y_rows=min(512, t)),
        grid=(b, width // lanes),
        in_specs=[pl.BlockSpec((1, t, lanes), lambda bi, ci: (bi, 0, ci)),
                  tile((lanes, 2 * half)), tile((2 * half, lanes)),
                  tile((1, half)), tile((1, half)),
                  pl.BlockSpec((1, lanes), lambda bi, ci: (0, ci))],
        out_specs=pl.BlockSpec((1, t, lanes), lambda bi, ci: (bi, 0, ci)),
        out_shape=jax.ShapeDtypeStruct((b, t, width), BF16),
        scratch_shapes=[pltpu.VMEM((2 * half // LANES, t, LANES), F32)],
        compiler_params=_params(("parallel", "parallel"), 48),
        name="s5",
    )(u, bblk, cblk, lam_r, lam_i, d_skip)


def _s5_tables(lam_re, lam_im, log_dt, b_re, b_im, c_re, c_im):
    g, p = lam_re.shape
    tiles = g // S5_TILE_GROUPS
    lre, lim = lam_re.astype(F32), lam_im.astype(F32)
    dt = jnp.exp(log_dt.astype(F32))[:, None]
    mag = jnp.exp(lre * dt)
    bar_re = mag * jnp.cos(lim * dt)
    bar_im = mag * jnp.sin(lim * dt)
    den = lre * lre + lim * lim
    coef_re = ((bar_re - 1.0) * lre + bar_im * lim) / den
    coef_im = (bar_im * lre - (bar_re - 1.0) * lim) / den
    bre, bim = b_re.astype(F32), b_im.astype(F32)
    bbar_re = coef_re[..., None] * bre - coef_im[..., None] * bim
    bbar_im = coef_re[..., None] * bim + coef_im[..., None] * bre
    eye = jnp.eye(S5_TILE_GROUPS, dtype=F32)

    def pack_b(m):
        m = m.reshape(tiles, S5_TILE_GROUPS, p, S5_GROUP)
        return jnp.einsum('tgpi,gh->tgihp', m, eye).reshape(tiles, S5_TILE_GROUPS * S5_GROUP,
                                                            S5_TILE_GROUPS * p)

    def pack_c(m):
        m = m.reshape(tiles, S5_TILE_GROUPS, S5_GROUP, p)
        return jnp.einsum('tgip,gh->tgphi', m, eye).reshape(tiles, S5_TILE_GROUPS * p,
                                                            S5_TILE_GROUPS * S5_GROUP)

    bblk = jnp.concatenate([pack_b(bbar_re), pack_b(bbar_im)], axis=-1).astype(BF16)
    cblk = jnp.concatenate([pack_c(c_re.astype(F32)), -pack_c(c_im.astype(F32))], axis=1).astype(BF16)
    lam_r = bar_re.reshape(tiles, 1, S5_TILE_GROUPS * p)
    lam_i = bar_im.reshape(tiles, 1, S5_TILE_GROUPS * p)
    return bblk, cblk, lam_r, lam_i


def _mem_kv_kernel(m_ref, g_ref, w_ref, kg_ref, k_ref, v_ref, *, heads):
    mn = _rms(m_ref[0], g_ref[...]).astype(BF16)
    kv = _dot(mn, w_ref[...])
    d = kv.shape[1] // 2
    hd = d // heads
    for h in range(heads):
        s = slice(h * hd, (h + 1) * hd)
        k_ref[0, :, s] = _rms(kv[:, s], kg_ref[...]).astype(k_ref.dtype)
    v_ref[0] = kv[:, d:].astype(v_ref.dtype)


def _mem_kv(mem, gain, w_kv, k_gain, heads):
    b, m, d = mem.shape
    blk = pl.BlockSpec((1, m, d), lambda bi: (bi, 0, 0))
    return pl.pallas_call(
        functools.partial(_mem_kv_kernel, heads=heads),
        grid=(b,),
        in_specs=[blk, pl.BlockSpec((1, d), lambda bi: (0, 0)),
                  pl.BlockSpec(w_kv.shape, lambda bi: (0, 0)),
                  pl.BlockSpec(k_gain.shape, lambda bi: (0, 0))],
        out_specs=[blk, blk],
        out_shape=[jax.ShapeDtypeStruct((b, m, d), BF16)] * 2,
        compiler_params=_params(("parallel",), 32),
        name="mem_kv",
    )(mem, gain, w_kv, k_gain)


def _xattn_kernel(h_ref, g_ref, wq_ref, qg_ref, k_ref, v_ref, wo_ref, *rest, heads, glu):
    *mix, o_ref = rest
    h = h_ref[0]
    if glu:
        ya_ref, yb_ref, wglu_ref, bglu_ref, woa_ref, wob_ref = mix
        yb = yb_ref[0]
        gate = jax.nn.sigmoid(_dot(yb, wglu_ref[...]) + bglu_ref[...])
        yb = (yb.astype(F32) * gate).astype(BF16)
        h = h + _dot(ya_ref[0], woa_ref[...]) + _dot(yb, wob_ref[...])
    else:
        a_ref, w_ref = mix
        h = h + _dot(a_ref[0], w_ref[...])
    hn = _rms(h, g_ref[...]).astype(BF16)
    q = _dot(hn, wq_ref[...])
    hd = q.shape[1] // heads
    k = k_ref[0]
    v = v_ref[0]
    outs = []
    for i in range(heads):
        s = slice(i * hd, (i + 1) * hd)
        qh = (_rms(q[:, s], qg_ref[...]) * hd ** -0.5).astype(BF16)
        sc = _dot_nt(qh, k[:, s])
        sc = sc - jnp.max(sc, axis=-1, keepdims=True)
        p = jnp.exp(sc)
        p = p / jnp.sum(p, axis=-1, keepdims=True)
        outs.append(_dot(p.astype(BF16), v[:, s]).astype(BF16))
    o = jnp.concatenate(outs, axis=-1)
    o_ref[0] = h + _dot(o, wo_ref[...])


def _xattn(h, gain, w_q, q_gain, k, v, w_o, mix, heads, tm):
    b, t, d = h.shape
    m = k.shape[1]
    tm = min(tm, t)
    full = lambda a: pl.BlockSpec(a.shape, lambda bi, i: (0,) * a.ndim)
    tokw = lambda n: pl.BlockSpec((1, tm, n), lambda bi, i: (bi, i, 0))
    mem = pl.BlockSpec((1, m, d), lambda bi, i: (bi, 0, 0))
    mix_specs = [tokw(a.shape[2]) if a.ndim == 3 else full(a) for a in mix]
    return pl.pallas_call(
        functools.partial(_xattn_kernel, heads=heads, glu=len(mix) == 6),
        grid=(b, t // tm),
        in_specs=[tokw(d), full(gain), full(w_q), full(q_gain), mem, mem, full(w_o)] + mix_specs,
        out_specs=tokw(d),
        out_shape=jax.ShapeDtypeStruct((b, t, d), F32),
        compiler_params=_params(("parallel", "parallel"), 48),
        name="xattn",
    )(h, gain, w_q, q_gain, k, v, w_o, *mix)


def _ffn_dense_kernel(h_ref, g_ref, wg_ref, wu_ref, wd_ref, o_ref, xn_ref, acc_ref):
    j = pl.program_id(1)

    @pl.when(j == 0)
    def _():
        xn_ref[...] = _rms(h_ref[...], g_ref[...]).astype(BF16)
        acc_ref[...] = jnp.zeros_like(acc_ref)

    xn = xn_ref[...]
    act = (jax.nn.silu(_dot(xn, wg_ref[...])) * _dot(xn, wu_ref[...])).astype(BF16)
    acc_ref[...] += _dot(act, wd_ref[...])

    @pl.when(j == pl.num_programs(1) - 1)
    def _():
        o_ref[...] = h_ref[...] + acc_ref[...]


def _ffn_dense(h, gain, w_gate, w_up, w_down, tm, tf):
    n, d = h.shape
    f = w_gate.shape[1]
    tm, tf = min(tm, n), min(tf, f)
    return pl.pallas_call(
        _ffn_dense_kernel,
        grid=(n // tm, f // tf),
        in_specs=[pl.BlockSpec((tm, d), lambda i, j: (i, 0)),
                  pl.BlockSpec((1, d), lambda i, j: (0, 0)),
                  pl.BlockSpec((d, tf), lambda i, j: (0, j)),
                  pl.BlockSpec((d, tf), lambda i, j: (0, j)),
                  pl.BlockSpec((tf, d), lambda i, j: (j, 0))],
        out_specs=pl.BlockSpec((tm, d), lambda i, j: (i, 0)),
        out_shape=jax.ShapeDtypeStruct((n, d), F32),
        scratch_shapes=[pltpu.VMEM((tm, d), BF16), pltpu.VMEM((tm, d), F32)],
        compiler_params=_params(("parallel", "arbitrary"), 48),
        name="ffn_dense",
    )(h, gain, w_gate, w_up, w_down)


def _pack_words(x):
    bits = lax.bitcast_convert_type(x.astype(BF16).astype(F32), jnp.uint32)
    half = x.shape[1] // 2
    return (bits[:, :half] >> 16) | (bits[:, half:] & jnp.uint32(0xFFFF0000))


def _unpack_words(w):
    lo = lax.bitcast_convert_type(w << 16, F32)
    hi = lax.bitcast_convert_type(w & jnp.uint32(0xFFFF0000), F32)
    return jnp.concatenate([lo, hi], axis=-1)


def _ffn_expert_kernel(te_ref, nu_ref, x_ref, wg_ref, wu_ref, wd_ref, o_ref, xb_ref, acc_ref):
    i = pl.program_id(0)
    j = pl.program_id(1)

    @pl.when(i < nu_ref[0])
    def _():
        @pl.when(j == 0)
        def _():
            xb_ref[...] = _unpack_words(x_ref[...]).astype(BF16)
            acc_ref[...] = jnp.zeros_like(acc_ref)

        x = xb_ref[...]
        act = (jax.nn.silu(_dot(x, wg_ref[0].astype(BF16))) * _dot(x, wu_ref[0].astype(BF16))).astype(BF16)
        acc_ref[...] += _dot(act, wd_ref[0].astype(BF16))

    @pl.when(j == pl.num_programs(1) - 1)
    def _():
        o_ref[...] = _pack_words(acc_ref[...])


def _ffn_expert(tile_expert, n_used, xs, w_gate, w_up, w_down, tm, tf):
    p, words = xs.shape
    d = 2 * words
    f = w_gate.shape[2]
    tf = min(tf, f)
    nj = f // tf

    def col(i, j, te, nu):
        return (te[i], 0, jnp.where(i < nu[0], j, nj - 1))

    def rowb(i, j, te, nu):
        return (te[i], jnp.where(i < nu[0], j, nj - 1), 0)

    grid_spec = pltpu.PrefetchScalarGridSpec(
        num_scalar_prefetch=2,
        grid=(p // tm, nj),
        in_specs=[pl.BlockSpec((tm, words), lambda i, j, te, nu: (i, 0)),
                  pl.BlockSpec((1, d, tf), col),
                  pl.BlockSpec((1, d, tf), col),
                  pl.BlockSpec((1, tf, d), rowb)],
        out_specs=pl.BlockSpec((tm, words), lambda i, j, te, nu: (i, 0)),
        scratch_shapes=[pltpu.VMEM((tm, d), BF16), pltpu.VMEM((tm, d), F32)],
    )
    return pl.pallas_call(
        _ffn_expert_kernel,
        grid_spec=grid_spec,
        out_shape=jax.ShapeDtypeStruct((p, words), jnp.uint32),
        compiler_params=_params(("arbitrary", "arbitrary"), 48),
        name="ffn_expert",
    )(tile_expert, n_used, xs, w_gate, w_up, w_down)


def _qkv_kernel(x_ref, g_ref, w_ref, qkg_ref, qk_ref, v_ref, *, n_qk):
    xn = _rms(x_ref[...], g_ref[...]).astype(BF16)
    z = _dot(xn, w_ref[...])
    blk = 256
    ones = _group_ones(blk, DIFF_HEAD)
    for c in range(n_qk // blk):
        s = slice(c * blk, (c + 1) * blk)
        zc = z[:, s]
        ms = _dot((zc * zc).astype(BF16), ones) * (1.0 / DIFF_HEAD)
        qk_ref[:, s] = (zc * lax.rsqrt(ms + NORM_EPS) * qkg_ref[:, s]).astype(qk_ref.dtype)
    v_ref[...] = z[:, n_qk:].astype(v_ref.dtype)


def _qkv(x, gain, w, qk_gain, tm):
    n, d = x.shape
    c = w.shape[1]
    n_qk = qk_gain.shape[1]
    tm = min(tm, n)
    return pl.pallas_call(
        functools.partial(_qkv_kernel, n_qk=n_qk),
        grid=(n // tm,),
        in_specs=[pl.BlockSpec((tm, d), lambda i: (i, 0)),
                  pl.BlockSpec((1, d), lambda i: (0, 0)),
                  pl.BlockSpec((d, c), lambda i: (0, 0)),
                  pl.BlockSpec((1, n_qk), lambda i: (0, 0))],
        out_specs=[pl.BlockSpec((tm, n_qk), lambda i: (i, 0)),
                   pl.BlockSpec((tm, c - n_qk), lambda i: (i, 0))],
        out_shape=[jax.ShapeDtypeStruct((n, n_qk), BF16),
                   jax.ShapeDtypeStruct((n, c - n_qk), BF16)],
        compiler_params=_params(("parallel",), 48),
        name="qkv",
    )(x, gain, w, qk_gain)


def _diff_attn_kernel(ti_ref, tj_ref, ts_ref, slope_ref, lam_ref, q_ref, qn_ref, k0_ref, kn_ref, v_ref,
                      sg_ref, o_ref, qs_ref, s0_ref, s1_ref, p_ref, m_ref, l_ref, acc_ref, dg_ref,
                      *, out_scale, rb, nt):
    h = pl.program_id(1)
    step = pl.program_id(2)
    i = ti_ref[step]
    j = tj_ref[step]
    tq = q_ref.shape[1]
    tk = k0_ref.shape[1]
    nl = tk // LANES
    slope = slope_ref[h]
    slot = ts_ref[step]

    def stack_queries(q):
        lane = lax.broadcasted_iota(jnp.int32, q.shape, 1)
        zero = jnp.zeros_like(q)
        qs_ref[0:tq, :] = jnp.where(lane < DIFF_HEAD, q, zero)
        qs_ref[tq:2 * tq, :] = jnp.where(lane < DIFF_HEAD, zero, q)

    @pl.when(j == 0)
    def _():
        m_ref[...] = jnp.full_like(m_ref, NEG_INF)
        l_ref[...] = jnp.zeros_like(l_ref)
        acc_ref[...] = jnp.zeros_like(acc_ref)

    @pl.when(step == 0)
    def _():
        stack_queries(q_ref[0])
        s0_ref[...] = _dot_nt(qs_ref[...], k0_ref[0])

    @pl.when(step == 0)
    def _():
        r = lax.broadcasted_iota(jnp.int32, (tq, tk), 0)
        c = lax.broadcasted_iota(jnp.int32, (tq, tk), 1)
        ahead = jnp.maximum(c - r, 0).astype(F32) * (-2.0 * slope)
        dg_ref[...] = jnp.where(c // ATTN_CHUNK <= r // ATTN_CHUNK, ahead, NEG_INF)

    col = lax.broadcasted_iota(jnp.int32, (1, tk), 1)
    key_bias = slope * (col + (j - i) * tq).astype(F32)

    def update(diag, cur_ref):
        for r0 in range(0, 2 * tq, rb):
            rows = slice(r0, r0 + rb)
            s = cur_ref[rows, :] + key_bias
            if diag:
                d0 = r0 % tq
                s = s + dg_ref[d0:d0 + rb, :]
            blk = s[:, 0:LANES]
            for c in range(1, nl):
                blk = jnp.maximum(blk, s[:, c * LANES:(c + 1) * LANES])
            m_prev = m_ref[rows, :]
            m_new = jnp.maximum(m_prev, jnp.max(blk, axis=-1, keepdims=True))
            corr = jnp.exp2(m_prev - m_new)
            p = jnp.exp2(s - jnp.concatenate([m_new] * nl, axis=1))
            part = p[:, 0:LANES]
            for c in range(1, nl):
                part = part + p[:, c * LANES:(c + 1) * LANES]
            l_ref[rows, :] = corr * l_ref[rows, :] + part
            acc_ref[rows, :] = corr * acc_ref[rows, :]
            m_ref[rows, :] = m_new
            p_ref[rows, :] = p.astype(BF16)
        v = v_ref[0]
        for half in range(2):
            rows = slice(half * tq, (half + 1) * tq)
            acc_ref[rows, :] += _dot(p_ref[rows, :], v)

    for parity, (cur_ref, nxt_ref) in enumerate(((s0_ref, s1_ref), (s1_ref, s0_ref))):
        @pl.when(jnp.logical_and(j < i, slot == parity))
        def _(cur_ref=cur_ref, nxt_ref=nxt_ref):
            nxt_ref[...] = _dot_nt(qs_ref[...], kn_ref[0])
            update(False, cur_ref)

        @pl.when(jnp.logical_and(jnp.logical_and(j == i, i < nt - 1), slot == parity))
        def _(cur_ref=cur_ref, nxt_ref=nxt_ref):
            stack_queries(qn_ref[0])
            nxt_ref[...] = _dot_nt(qs_ref[...], k0_ref[0])
            update(True, cur_ref)

        @pl.when(jnp.logical_and(jnp.logical_and(j == i, i == nt - 1), slot == parity))
        def _(cur_ref=cur_ref):
            update(True, cur_ref)

    @pl.when(j == i)
    def _():
        o = acc_ref[...] / jnp.sum(l_ref[...], axis=-1, keepdims=True)
        o = o[:tq] - lam_ref[0] * o[tq:]
        o = o * lax.rsqrt(jnp.mean(o * o, axis=-1, keepdims=True) + NORM_EPS) * sg_ref[...]
        o_ref[0] = (o * out_scale).astype(o_ref.dtype)


def _diff_attn(slopes, lam, qk, v, sub_gain, heads, out_scale, tq, rb):
    b, t, _ = v.shape
    hw = 2 * DIFF_HEAD
    tq = min(tq, t)
    rb = min(rb, tq)
    nt = t // tq
    pairs = [(i, j) for i in range(nt) for j in range(i + 1)]
    ti = jnp.asarray([p[0] for p in pairs], jnp.int32)
    tj = jnp.asarray([p[1] for p in pairs], jnp.int32)
    slots, first = [], 0
    for i in range(nt):
        slots += [(first + j) % 2 for j in range(i + 1)]
        first = 1 - slots[-1]
    ts = jnp.asarray(slots, jnp.int32)
    smem = pl.BlockSpec(memory_space=pltpu.SMEM)
    grid_spec = pltpu.PrefetchScalarGridSpec(
        num_scalar_prefetch=3,
        grid=(b, heads, len(pairs)),
        in_specs=[smem, smem,
                  pl.BlockSpec((1, tq, hw), lambda bi, h, s, ti, tj, ts: (bi, ti[s], h)),
                  pl.BlockSpec((1, tq, hw),
                               lambda bi, h, s, ti, tj, ts: (bi, jnp.minimum(ti[s] + 1, nt - 1), h)),
                  pl.BlockSpec((1, tq, hw), lambda bi, h, s, ti, tj, ts: (bi, 0, heads + h)),
                  pl.BlockSpec((1, tq, hw),
                               lambda bi, h, s, ti, tj, ts: (bi, jnp.minimum(tj[s] + 1, ti[s]), heads + h)),
                  pl.BlockSpec((1, tq, hw), lambda bi, h, s, ti, tj, ts: (bi, tj[s], h)),
                  pl.BlockSpec((1, hw), lambda bi, h, s, ti, tj, ts: (0, 0))],
        out_specs=pl.BlockSpec((1, tq, hw), lambda bi, h, s, ti, tj, ts: (bi, ti[s], h)),
        scratch_shapes=[pltpu.VMEM((2 * tq, hw), BF16),
                        pltpu.VMEM((2 * tq, tq), F32),
                        pltpu.VMEM((2 * tq, tq), F32),
                        pltpu.VMEM((2 * tq, tq), BF16),
                        pltpu.VMEM((2 * tq, LANES), F32),
                        pltpu.VMEM((2 * tq, LANES), F32),
                        pltpu.VMEM((2 * tq, hw), F32),
                        pltpu.VMEM((tq, tq), F32)],
    )
    return pl.pallas_call(
        functools.partial(_diff_attn_kernel, out_scale=out_scale, rb=rb, nt=nt),
        grid_spec=grid_spec,
        out_shape=jax.ShapeDtypeStruct((b, t, heads * hw), BF16),
        compiler_params=_params(("arbitrary", "arbitrary", "arbitrary"), 48),
        name="diff_attn",
    )(ti, tj, ts, slopes, lam, qk, qk, qk, qk, v, sub_gain)


def _router_kernel(h_ref, g_ref, w_ref, b_ref, xn_ref, r_ref):
    xn = _rms(h_ref[...], g_ref[...])
    xn_ref[...] = _pack_words(xn)
    w = w_ref[...]
    x_hi = xn.astype(BF16)
    x_lo = (xn - x_hi.astype(F32)).astype(BF16)
    w_hi = w.astype(BF16)
    w_lo = (w - w_hi.astype(F32)).astype(BF16)
    logits = _dot(x_hi, w_hi) + (_dot(x_hi, w_lo) + _dot(x_lo, w_hi)) + b_ref[...]
    lane = lax.broadcasted_iota(jnp.int32, logits.shape, 1).astype(F32)
    big = float(ROUTER_LANES)
    l1 = jnp.max(logits, axis=-1, keepdims=True)
    i1 = jnp.min(jnp.where(logits == l1, lane, big), axis=-1, keepdims=True)
    rest = jnp.where(lane == i1, NEG_INF, logits)
    l2 = jnp.max(rest, axis=-1, keepdims=True)
    i2 = jnp.min(jnp.where(rest == l2, lane, big), axis=-1, keepdims=True)
    e = jnp.exp(l2 - l1)
    w1 = 1.0 / (1.0 + e)
    w2 = e / (1.0 + e)
    r_ref[...] = jnp.where(lane == 0.0, i1, jnp.where(lane == 1.0, i2,
                           jnp.where(lane == 2.0, w1, jnp.where(lane == 3.0, w2, 0.0))))


def _router(h, gain, w_pad, b_pad, tm):
    n, d = h.shape
    tm = min(tm, n)
    return pl.pallas_call(
        _router_kernel,
        grid=(n // tm,),
        in_specs=[pl.BlockSpec((tm, d), lambda i: (i, 0)),
                  pl.BlockSpec((1, d), lambda i: (0, 0)),
                  pl.BlockSpec(w_pad.shape, lambda i: (0, 0)),
                  pl.BlockSpec(b_pad.shape, lambda i: (0, 0))],
        out_specs=[pl.BlockSpec((tm, d // 2), lambda i: (i, 0)),
                   pl.BlockSpec((tm, ROUTER_LANES), lambda i: (i, 0))],
        out_shape=[jax.ShapeDtypeStruct((n, d // 2), jnp.uint32),
                   jax.ShapeDtypeStruct((n, ROUTER_LANES), F32)],
        compiler_params=_params(("parallel",), 32),
        name="router",
    )(h, gain, w_pad, b_pad)


GATHER_UNROLL = 8


def _scatter_kernel(pos_ref, x_ref, init_hbm, o_hbm, sem):
    del init_hbm
    rows = x_ref.shape[0]
    base = pl.program_id(0) * rows

    def start(r, c):
        pltpu.make_async_copy(x_ref.at[pl.ds(r, 1)], o_hbm.at[pl.ds(pos_ref[base + r], 1)], sem).start()
        return c

    lax.fori_loop(0, rows, start, 0, unroll=GATHER_UNROLL)
    pltpu.make_async_copy(x_ref, o_hbm.at[pl.ds(0, rows)], sem).wait()


def _scatter_rows(pos, x, cap, tg):
    a = pos.shape[0]
    n, w = x.shape
    tg = min(tg, n)
    assert n % tg == 0 and a % n == 0 and tg % GATHER_UNROLL == 0
    nblk = n // tg
    grid_spec = pltpu.PrefetchScalarGridSpec(
        num_scalar_prefetch=1,
        grid=(a // tg,),
        in_specs=[pl.BlockSpec((tg, w), lambda i, pos_ref: (lax.rem(i, nblk), 0)),
                  pl.BlockSpec(memory_space=pl.ANY)],
        out_specs=pl.BlockSpec(memory_space=pl.ANY),
        scratch_shapes=[pltpu.SemaphoreType.DMA(())],
    )
    return pl.pallas_call(
        _scatter_kernel,
        grid_spec=grid_spec,
        out_shape=jax.ShapeDtypeStruct((cap, w), x.dtype),
        input_output_aliases={2: 0},
        compiler_params=_params(("arbitrary",), 32),
        name="scatter_rows",
    )(pos, x, jnp.zeros((cap, w), x.dtype))


def _combine_kernel(pos_ref, h_ref, r_ref, y_hbm, o_ref, y0_ref, y1_ref, sem, *, n):
    rows = h_ref.shape[0]
    base = pl.program_id(0) * rows

    def start(r, c):
        pltpu.make_async_copy(y_hbm.at[pl.ds(pos_ref[base + r], 1)], y0_ref.at[pl.ds(r, 1)], sem).start()
        pltpu.make_async_copy(y_hbm.at[pl.ds(pos_ref[n + base + r], 1)], y1_ref.at[pl.ds(r, 1)], sem).start()
        return c

    lax.fori_loop(0, rows, start, 0, unroll=GATHER_UNROLL)
    pltpu.make_async_copy(y_hbm.at[pl.ds(0, rows)], y0_ref, sem).wait()
    pltpu.make_async_copy(y_hbm.at[pl.ds(0, rows)], y1_ref, sem).wait()
    r = r_ref[...]
    w0 = r[:, 2:3]
    w1 = r[:, 3:4]
    o_ref[...] = h_ref[...] + w0 * _unpack_words(y0_ref[...]) + w1 * _unpack_words(y1_ref[...])


def _combine(pos, h, route, ys, tm):
    n, d = h.shape
    tm = min(tm, n)
    assert n % tm == 0 and tm % GATHER_UNROLL == 0
    grid_spec = pltpu.PrefetchScalarGridSpec(
        num_scalar_prefetch=1,
        grid=(n // tm,),
        in_specs=[pl.BlockSpec((tm, d), lambda i, pos_ref: (i, 0)),
                  pl.BlockSpec((tm, ROUTER_LANES), lambda i, pos_ref: (i, 0)),
                  pl.BlockSpec(memory_space=pl.ANY)],
        out_specs=pl.BlockSpec((tm, d), lambda i, pos_ref: (i, 0)),
        scratch_shapes=[pltpu.VMEM((tm, d // 2), jnp.uint32),
                        pltpu.VMEM((tm, d // 2), jnp.uint32),
                        pltpu.SemaphoreType.DMA(())],
    )
    return pl.pallas_call(
        functools.partial(_combine_kernel, n=n),
        grid_spec=grid_spec,
        out_shape=jax.ShapeDtypeStruct((n, d), F32),
        compiler_params=_params(("arbitrary",), 32),
        name="combine",
    )(pos, h, route, ys)


def _moe(h, gain, w_router, b_router, w_gate, w_up, w_down, tm, tf):
    n, d = h.shape
    e = w_gate.shape[0]
    w_pad = jnp.zeros((d, ROUTER_LANES), F32).at[:, :e].set(w_router.astype(F32))
    b_pad = jnp.full((1, ROUTER_LANES), NEG_INF, F32).at[0, :e].set(b_router.astype(F32))
    xn, route = _router(h, gain, w_pad, b_pad, 512)

    tm = min(tm, n)
    experts = jnp.concatenate([route[:, 0], route[:, 1]]).astype(jnp.int32)
    onehot = (experts[:, None] == jnp.arange(e, dtype=jnp.int32)[None, :]).astype(jnp.int32)
    counts = jnp.sum(onehot, axis=0)
    padded = ((counts + tm - 1) // tm) * tm
    ends = jnp.cumsum(padded)
    starts = ends - padded
    pos = jnp.sum(onehot * (jnp.cumsum(onehot, axis=0) - onehot + starts[None, :]), axis=1)
    pos = pos.astype(jnp.int32)
    cap = 2 * n + e * tm
    tile_start = jnp.arange(cap // tm, dtype=jnp.int32) * tm
    tile_expert = jnp.minimum(jnp.sum((tile_start[:, None] >= ends[None, :]).astype(jnp.int32), axis=1),
                              e - 1)
    n_used = (ends[-1:] // tm).astype(jnp.int32)

    xs = _scatter_rows(pos, xn, cap, 2048)
    ys = _ffn_expert(tile_expert, n_used, xs, w_gate, w_up, w_down, tm, tf)
    return _combine(pos, h, route, ys, 1024)


def _row(v):
    return v.astype(F32).reshape(1, -1)


def _memory_block(h, mix, mem, norm_x, norm_m, w_q, w_kv, q_gain, k_gain, w_o):
    k, v = _mem_kv(mem, _row(norm_m), w_kv.astype(BF16), _row(k_gain), MEM_HEADS)
    return _xattn(h, _row(norm_x), w_q.astype(BF16), _row(q_gain), k, v, w_o.astype(BF16),
                  mix, MEM_HEADS, 512)


def _even_mixer(h, norm, hy_w_in, rw_mu, rw_w0, rw_w_up, rw_a0, rw_a_up, rw_g_up, rw_k_k, rw_k_a,
                rw_r_k, rw_ln_w, rw_ln_b, s5_lam_re, s5_lam_im, s5_log_dt, s5_b_re, s5_b_im,
                s5_c_re, s5_c_im, s5_d, s5_w_glu, s5_b_glu, hy_w_out):
    b, t, d = h.shape
    width = rw_w0.shape[0]
    n_lora = DECAY_LORA + ICLR_LORA + GATE_LORA
    rwkv_cols = 3 * width + n_lora
    n_r = 3 * width + LORA_PAD
    pad = LORA_PAD - n_lora
    w_in = jnp.concatenate([hy_w_in[:, :rwkv_cols], jnp.zeros((d, pad), hy_w_in.dtype),
                            hy_w_in[:, rwkv_cols:]], axis=1).astype(BF16)
    mu = jnp.concatenate([rw_mu.astype(F32), jnp.zeros((pad,), F32)]).reshape(1, -1)
    w_lora = jnp.zeros((LORA_PAD, 3 * width), F32)
    w_lora = w_lora.at[:DECAY_LORA, :width].set(rw_w_up.astype(F32))
    w_lora = w_lora.at[DECAY_LORA:DECAY_LORA + ICLR_LORA, width:2 * width].set(rw_a_up.astype(F32))
    w_lora = w_lora.at[DECAY_LORA + ICLR_LORA:n_lora, 2 * width:].set(rw_g_up.astype(F32))

    *parts, u = _rwkv_prep(h, _row(norm), w_in, mu, w_lora.astype(BF16), _row(rw_w0), _row(rw_a0),
                           _row(rw_k_k), _row(rw_k_a), width, 512)
    ya = _rwkv_chunk(*parts, _row(rw_r_k), _row(rw_ln_w), _row(rw_ln_b), width // RWKV_HEAD, 2)

    bblk, cblk, lam_r, lam_i = _s5_tables(s5_lam_re, s5_lam_im, s5_log_dt, s5_b_re, s5_b_im,
                                          s5_c_re, s5_c_im)
    yb = _s5(u.reshape(b, t, -1), bblk, cblk, lam_r, lam_i, _row(s5_d))

    w_out = hy_w_out.astype(BF16)
    return (ya, yb, s5_w_glu.astype(BF16), _row(s5_b_glu), w_out[:width], w_out[width:])


def _odd_mixer(h, norm, layer, w_qkv, q_gain, k_gain, lam_q1, lam_k1, lam_q2, lam_k2, sub_gain, w_o):
    b, t, d = h.shape
    heads = w_o.shape[0] // (2 * DIFF_HEAD)
    lambda_init = 0.8 - 0.6 * math.exp(-0.3 * layer)
    lam = (jnp.exp(jnp.sum(lam_q1.astype(F32) * lam_k1.astype(F32)))
           - jnp.exp(jnp.sum(lam_q2.astype(F32) * lam_k2.astype(F32))) + lambda_init).reshape(1)
    log2e = math.log2(math.e)
    slopes = 2.0 ** (-8.0 * jnp.arange(1, heads + 1, dtype=F32) / heads) * log2e
    n_half = heads * 2 * DIFF_HEAD
    qk_gain = jnp.concatenate([jnp.tile(q_gain.astype(F32), n_half // DIFF_HEAD) * (DIFF_HEAD ** -0.5 * log2e),
                               jnp.tile(k_gain.astype(F32), n_half // DIFF_HEAD)]).reshape(1, -1)
    h2 = h.reshape(b * t, d)
    qk, v = _qkv(h2, _row(norm), w_qkv.astype(BF16), qk_gain, 512)
    o = _diff_attn(slopes, lam, qk.reshape(b, t, -1), v.reshape(b, t, -1), _row(sub_gain),
                   heads, 1.0 - lambda_init, 1024, 128)
    return (o, w_o.astype(BF16))


def kernel(x, mem, norm_mix, norm_xattn, norm_mem, norm_ffn, xa_w_q, xa_w_kv, xa_q_gain, xa_k_gain, xa_w_o, hy_w_in, rw_mu, rw_w0, rw_w_up, rw_a0, rw_a_up, rw_g_up, rw_k_k, rw_k_a, rw_r_k, rw_ln_w, rw_ln_b, s5_lam_re, s5_lam_im, s5_log_dt, s5_b_re, s5_b_im, s5_c_re, s5_c_im, s5_d, s5_w_glu, s5_b_glu, hy_w_out, ff_w_gate, ff_w_up, ff_w_down, da_w_qkv, da_q_gain, da_k_gain, da_lam_q1, da_lam_k1, da_lam_q2, da_lam_k2, da_sub_gain, da_w_o, moe_w_router, moe_b_router, moe_w_gate, moe_w_up, moe_w_down):
    depth = norm_mix.shape[0]
    b, t, d = x.shape
    h = x
    for l in range(depth):
        i = l // 2
        if l % 2 == 0:
            mix = _even_mixer(h, norm_mix[l], hy_w_in[i], rw_mu[i], rw_w0[i], rw_w_up[i], rw_a0[i],
                            rw_a_up[i], rw_g_up[i], rw_k_k[i], rw_k_a[i], rw_r_k[i], rw_ln_w[i],
                            rw_ln_b[i], s5_lam_re[i], s5_lam_im[i], s5_log_dt[i], s5_b_re[i],
                            s5_b_im[i], s5_c_re[i], s5_c_im[i], s5_d[i], s5_w_glu[i], s5_b_glu[i],
                            hy_w_out[i])
        else:
            mix = _odd_mixer(h, norm_mix[l], l, da_w_qkv[i], da_q_gain[i], da_k_gain[i], da_lam_q1[i],
                             da_lam_k1[i], da_lam_q2[i], da_lam_k2[i], da_sub_gain[i], da_w_o[i])
        h = _memory_block(h, mix, mem, norm_xattn[l], norm_mem[l], xa_w_q[l], xa_w_kv[l], xa_q_gain[l],
                          xa_k_gain[l], xa_w_o[l])
        h2 = h.reshape(b * t, d)
        if l % 2 == 0:
            h2 = _ffn_dense(h2, _row(norm_ffn[l]), ff_w_gate[i].astype(BF16), ff_w_up[i].astype(BF16),
                            ff_w_down[i].astype(BF16), 512, 1408)
        else:
            h2 = _moe(h2, _row(norm_ffn[l]), moe_w_router[i], moe_b_router[i],
                      moe_w_gate[i], moe_w_up[i], moe_w_down[i], 1024, 512)
        h = h2.reshape(b, t, d)
    return h
```

```python
import functools
import math

import jax
import jax.numpy as jnp
from jax import lax
from jax.experimental import pallas as pl
from jax.experimental.pallas import tpu as pltpu

F32 = jnp.float32
BF16 = jnp.bfloat16

NORM_EPS = 1e-6
NEG_INF = -1e30
ATTN_CHUNK = 64

RWKV_HEAD = 64
RWKV_CHUNK = 64
DECAY_LORA = 64
ICLR_LORA = 64
GATE_LORA = 160
LORA_PAD = 384
RWKV_GN_EPS = 1e-5 * RWKV_HEAD

S5_GROUP = 16
S5_STATE = 64
S5_TILE_GROUPS = 8

DIFF_HEAD = 64
MEM_HEADS = 4
N_EXPERTS = 8
LANES = 128
ROUTER_LANES = LANES

V7X_VMEM_BYTES = 64 * 1024 * 1024


def _params(semantics, vmem_mib):
    assert vmem_mib * 1024 * 1024 < V7X_VMEM_BYTES
    return pltpu.CompilerParams(dimension_semantics=semantics,
                                vmem_limit_bytes=vmem_mib * 1024 * 1024)


def _rms(x, gain):
    return x * lax.rsqrt(jnp.mean(x * x, axis=-1, keepdims=True) + NORM_EPS) * gain


def _dot(a, b):
    return jnp.dot(a, b, preferred_element_type=F32)


def _dot_nt(a, b):
    return lax.dot_general(a, b, (((1,), (1,)), ((), ())), preferred_element_type=F32)


def _dot_tn(a, b):
    return lax.dot_general(a, b, (((0,), (0,)), ((), ())), preferred_element_type=F32)


def _group_ones(width, group):
    r = lax.broadcasted_iota(jnp.int32, (width, width), 0) // group
    c = lax.broadcasted_iota(jnp.int32, (width, width), 1) // group
    return jnp.where(r == c, 1.0, 0.0).astype(BF16)


def _group_sum(x, ones):
    hi = x.astype(BF16)
    lo = (x - hi.astype(F32)).astype(BF16)
    return _dot(hi, ones) + _dot(lo, ones)


def _rwkv_prep_kernel(x_ref, gain_ref, win_ref, mu_ref, wl_ref, w0_ref, a0_ref, kk_ref, ka_ref,
                      r_ref, lw_ref, k_ref, v_ref, kn_ref, a_ref, g_ref, u_ref, carry_ref, *, width):
    i = pl.program_id(1)

    @pl.when(i == 0)
    def _():
        carry_ref[...] = jnp.zeros_like(carry_ref)

    xn = _rms(x_ref[0], gain_ref[...]).astype(BF16)
    zall = _dot(xn, win_ref[...])
    n_r = mu_ref.shape[1]
    u_ref[0] = zall[:, n_r:].astype(u_ref.dtype)
    z = zall[:, :n_r]
    rows = lax.broadcasted_iota(jnp.int32, z.shape, 0)
    prev = jnp.where(rows == 0, carry_ref[...], pltpu.roll(z, 1, axis=0))
    carry_ref[...] = z[z.shape[0] - 1:, :]
    zs = z + (prev - z) * mu_ref[...]

    w = width
    r = zs[:, :w]
    k = zs[:, w:2 * w]
    v = zs[:, 2 * w:3 * w]
    lo = zs[:, 3 * w:]
    lane = lax.broadcasted_iota(jnp.int32, lo.shape, 1)
    lo = jnp.where(lane < DECAY_LORA, jnp.tanh(lo),
                   jnp.where(lane < DECAY_LORA + ICLR_LORA, lo, jax.nn.sigmoid(lo)))
    m = _dot(lo.astype(BF16), wl_ref[...])
    w_log = -jax.nn.softplus(-(w0_ref[...] + m[:, :w])) - 0.5
    a = jax.nn.sigmoid(a0_ref[...] + m[:, w:2 * w])
    kn = k * kk_ref[...]
    ss = _group_sum(kn * kn, _group_ones(w, RWKV_HEAD))
    kn = kn * lax.rsqrt(jnp.maximum(ss, 1e-24))

    r_ref[0] = r
    lw_ref[0] = -jnp.exp(w_log)
    k_ref[0] = k * (1.0 + (a - 1.0) * ka_ref[...])
    v_ref[0] = v
    kn_ref[0] = kn
    a_ref[0] = a
    g_ref[0] = m[:, 2 * w:]


def _rwkv_prep(x, gain, w_in, mu, wl, w0, a0, k_k, k_a, width, tm):
    b, t, d = x.shape
    c = mu.shape[1]
    n_u = w_in.shape[1] - c
    tm = min(tm, t)
    row = lambda n: pl.BlockSpec((1, n), lambda bi, i: (0, 0))
    full = lambda a: pl.BlockSpec(a.shape, lambda bi, i: (0, 0))
    out = lambda n: pl.BlockSpec((1, tm, n), lambda bi, i: (bi, i, 0))
    return pl.pallas_call(
        functools.partial(_rwkv_prep_kernel, width=width),
        grid=(b, t // tm),
        in_specs=[pl.BlockSpec((1, tm, d), lambda bi, i: (bi, i, 0)),
                  row(d), full(w_in), row(c), full(wl),
                  row(width), row(width), row(width), row(width)],
        out_specs=[out(width)] * 7 + [out(n_u)],
        out_shape=[jax.ShapeDtypeStruct((b, t, width), F32)] * 7 + [jax.ShapeDtypeStruct((b, t, n_u), BF16)],
        scratch_shapes=[pltpu.VMEM((1, c), F32)],
        compiler_params=_params(("parallel", "arbitrary"), 48),
        name="rwkv_prep",
    )(x, gain, w_in, mu, wl, w0, a0, k_k, k_a)


def _rwkv_chunk_kernel(r_ref, lw_ref, k_ref, v_ref, kn_ref, a_ref, g_ref,
                       rk_ref, lnw_ref, lnb_ref, o_ref, st_ref, *, heads, chunks, L):
    @pl.when(pl.program_id(2) == 0)
    def _():
        st_ref[...] = jnp.zeros_like(st_ref)

    n = RWKV_HEAD
    ri = lax.broadcasted_iota(jnp.int32, (L, L), 0)
    ci = lax.broadcasted_iota(jnp.int32, (L, L), 1)
    tri = jnp.where(ci <= ri, 1.0, 0.0).astype(BF16)
    strict = ci < ri
    incl = ci <= ri
    eye = jnp.where(ci == ri, 1.0, 0.0).astype(F32)
    rk_all, lnw_all, lnb_all = rk_ref[...], lnw_ref[...], lnb_ref[...]
    inv_steps = max(1, int(math.ceil(math.log2(L))) - 1)

    items = [(c, h) for c in range(chunks) for h in range(heads)]
    fs = []
    for c, h in items:
        rows = slice(c * L, (c + 1) * L)
        s = slice(h * n, (h + 1) * n)
        r, k, v = r_ref[0, rows, s], k_ref[0, rows, s], v_ref[0, rows, s]
        kn, a, lw = kn_ref[0, rows, s], a_ref[0, rows, s], lw_ref[0, rows, s]
        hi = lw.astype(BF16)
        lo = (lw - hi.astype(F32)).astype(BF16)
        fs.append(dict(r=r, k=k, v=v, lw=lw, alpha=-kn, beta=kn * a, v_b=v.astype(BF16),
                       cum=_dot(tri, hi) + _dot(tri, lo)))
    for f in fs:
        cum, lw = f["cum"], f["lw"]
        cum_ex = cum - lw
        c_mid = cum[L // 2 - 1:L // 2, :]
        c_end = cum[L - 1:L, :]
        e_inv = jnp.exp(c_mid - cum)
        e_end = jnp.exp(c_end - cum)
        lhs = jnp.concatenate([f["alpha"] * jnp.exp(cum_ex - c_mid), f["r"] * jnp.exp(cum - c_mid)],
                              axis=0).astype(BF16)
        f["m_b"] = _dot_nt(lhs, (f["beta"] * e_inv).astype(BF16))
        f["m_k"] = _dot_nt(lhs, (f["k"] * e_inv).astype(BF16))
        f["abs_lhs"] = jnp.concatenate([f["alpha"] * jnp.exp(cum_ex), f["r"] * jnp.exp(cum)],
                                       axis=0).astype(BF16)
        f["b_end"] = (f["beta"] * e_end).astype(BF16)
        f["k_end"] = (f["k"] * e_end).astype(BF16)
        f["decay"] = jnp.exp(c_end)
    for f in fs:
        m_ab = jnp.where(strict, f["m_b"][:L], 0.0)
        f["m_rb"] = jnp.where(incl, f["m_b"][L:], 0.0).astype(BF16)
        f["ak_v"] = _dot(jnp.where(strict, f["m_k"][:L], 0.0).astype(BF16), f["v_b"])
        f["rk_v"] = _dot(jnp.where(incl, f["m_k"][L:], 0.0).astype(BF16), f["v_b"])
        f["inv"] = eye + m_ab
        f["p"] = m_ab
    for _ in range(inv_steps):
        for f in fs:
            pb = f["p"].astype(BF16)
            f["p"] = _dot(pb, pb)
        for f in fs:
            f["inv"] = f["inv"] + _dot(f["inv"].astype(BF16), f["p"].astype(BF16))
    for f in fs:
        f["inv"] = f["inv"].astype(BF16)

    st = [st_ref[h] for h in range(heads)]
    for c in range(chunks):
        cf = fs[c * heads:(c + 1) * heads]
        from_state = [_dot_nt(f["abs_lhs"], st[h].astype(BF16)) for h, f in enumerate(cf)]
        u_b = [_dot(f["inv"], (from_state[h][:L] + f["ak_v"]).astype(BF16)).astype(BF16)
               for h, f in enumerate(cf)]
        st = [st[h] * f["decay"] + _dot_tn(u_b[h], f["b_end"]) + _dot_tn(f["v_b"], f["k_end"])
              for h, f in enumerate(cf)]
        for h, f in enumerate(cf):
            rows = slice(c * L, (c + 1) * L)
            s = slice(h * n, (h + 1) * n)
            o = from_state[h][L:] + _dot(f["m_rb"], u_b[h]) + f["rk_v"]
            mean = jnp.mean(o, axis=-1, keepdims=True)
            d = o - mean
            var = jnp.mean(d * d, axis=-1, keepdims=True)
            o = d * lax.rsqrt(var + RWKV_GN_EPS) * lnw_all[:, s] + lnb_all[:, s]
            bonus = jnp.sum(f["r"] * f["k"] * rk_all[:, s], axis=-1, keepdims=True) * f["v"]
            o_ref[0, rows, s] = ((o + bonus) * g_ref[0, rows, s]).astype(o_ref.dtype)
    for h in range(heads):
        st_ref[h] = st[h]


def _rwkv_chunk(r, lw, k, v, kn, a, g, r_k, ln_w, ln_b, heads_per_step, chunks_per_step):
    b, t, width = r.shape
    L = min(RWKV_CHUNK, t)
    chunks = min(chunks_per_step, t // L)
    lanes = heads_per_step * RWKV_HEAD
    seq = pl.BlockSpec((1, chunks * L, lanes), lambda bi, hi, ci: (bi, ci, hi))
    row = pl.BlockSpec((1, lanes), lambda bi, hi, ci: (0, hi))
    return pl.pallas_call(
        functools.partial(_rwkv_chunk_kernel, heads=heads_per_step, chunks=chunks, L=L),
        grid=(b, width // lanes, t // (chunks * L)),
        in_specs=[seq] * 7 + [row] * 3,
        out_specs=seq,
        out_shape=jax.ShapeDtypeStruct((b, t, width), BF16),
        scratch_shapes=[pltpu.VMEM((heads_per_step, RWKV_HEAD, RWKV_HEAD), F32)],
        compiler_params=_params(("parallel", "parallel", "arbitrary"), 32),
        name="rwkv_chunk",
    )(r, lw, k, v, kn, a, g, r_k, ln_w, ln_b)


S5_ROWS = 8


def _gelu_tanh(x):
    return 0.5 * x * (1.0 + jnp.tanh(math.sqrt(2.0 / math.pi) * (x + 0.044715 * (x * x * x))))


def _s5_kernel(u_ref, bb_ref, cb_ref, lr_ref, li_ref, d_ref, y_ref, x_ref, *, unroll, y_rows):
    half = lr_ref.shape[-1]
    t = u_ref.shape[1]
    nb = half // LANES
    lr = lr_ref[0]
    li = li_ref[0]

    for cblk in range(2 * nb):
        x_ref[cblk] = _dot(u_ref[0], bb_ref[0, :, cblk * LANES:(cblk + 1) * LANES])

    pows = [(lr, li)]
    for _ in range(S5_ROWS - 1):
        qr, qi = pows[-1]
        pows.append((qr * lr - qi * li, qr * li + qi * lr))
    pw_r = jnp.concatenate([q[0] for q in pows], axis=0)
    pw_i = jnp.concatenate([q[1] for q in pows], axis=0)
    row = lax.broadcasted_iota(jnp.int32, (S5_ROWS, half), 0)
    shifts = []
    k = 1
    while k < S5_ROWS:
        shifts.append((k, jnp.where(row >= k, pows[k - 1][0], 0.0), jnp.where(row >= k, pows[k - 1][1], 0.0)))
        k *= 2

    def tile_scan(ti, carry):
        rows = pl.ds(pl.multiple_of(ti * S5_ROWS, S5_ROWS), S5_ROWS)
        out = []
        for cblk in range(nb):
            lanes = slice(cblk * LANES, (cblk + 1) * LANES)
            xr = x_ref[cblk, rows, :]
            xi = x_ref[nb + cblk, rows, :]
            for k, ar, ai in shifts:
                sr = pltpu.roll(xr, k, axis=0)
                si = pltpu.roll(xi, k, axis=0)
                xr, xi = (xr + (ar[:, lanes] * sr - ai[:, lanes] * si),
                          xi + (ar[:, lanes] * si + ai[:, lanes] * sr))
            cr, ci = carry[cblk]
            xr, xi = (xr + (pw_r[:, lanes] * cr - pw_i[:, lanes] * ci),
                      xi + (pw_r[:, lanes] * ci + pw_i[:, lanes] * cr))
            x_ref[cblk, rows, :] = xr
            x_ref[nb + cblk, rows, :] = xi
            out.append((jnp.broadcast_to(xr[S5_ROWS - 1:, :], (S5_ROWS, LANES)),
                        jnp.broadcast_to(xi[S5_ROWS - 1:, :], (S5_ROWS, LANES))))
        return tuple(out)

    zero = jnp.zeros((S5_ROWS, LANES), F32)
    lax.fori_loop(0, t // S5_ROWS, tile_scan, tuple((zero, zero) for _ in range(nb)), unroll=unroll)

    d = d_ref[...]
    cb = cb_ref[0]
    for r0 in range(0, t, y_rows):
        rows = slice(r0, r0 + y_rows)
        x = jnp.concatenate([x_ref[cblk, rows, :] for cblk in range(2 * nb)], axis=-1).astype(BF16)
        y = _dot(x, cb) + d * u_ref[0, rows, :].astype(F32)
        y_ref[0, rows, :] = _gelu_tanh(y).astype(y_ref.dtype)


def _s5(u, bblk, cblk, lam_r, lam_i, d_skip):
    b, t, width = u.shape
    lanes = S5_TILE_GROUPS * S5_GROUP
    half = S5_TILE_GROUPS * S5_STATE
    tile = lambda shape: pl.BlockSpec((1,) + shape, lambda bi, ci: (ci, 0, 0))
    return pl.pallas_call(
        functools.partial(_s5_kernel, unroll=min(4, t // S5_ROWS), y_rows=min(512, t)),
        grid=(b, width // lanes),
        in_specs=[pl.BlockSpec((1, t, lanes), lambda bi, ci: (bi, 0, ci)),
                  tile((lanes, 2 * half)), tile((2 * half, lanes)),
                  tile((1, half)), tile((1, half)),
                  pl.BlockSpec((1, lanes), lambda bi, ci: (0, ci))],
        out_specs=pl.BlockSpec((1, t, lanes), lambda bi, ci: (bi, 0, ci)),
        out_shape=jax.ShapeDtypeStruct((b, t, width), BF16),
        scratch_shapes=[pltpu.VMEM((2 * half // LANES, t, LANES), F32)],
        compiler_params=_params(("parallel", "parallel"), 48),
        name="s5",
    )(u, bblk, cblk, lam_r, lam_i, d_skip)


def _s5_tables(lam_re, lam_im, log_dt, b_re, b_im, c_re, c_im):
    g, p = lam_re.shape
    tiles = g // S5_TILE_GROUPS
    lre, lim = lam_re.astype(F32), lam_im.astype(F32)
    dt = jnp.exp(log_dt.astype(F32))[:, None]
    mag = jnp.exp(lre * dt)
    bar_re = mag * jnp.cos(lim * dt)
    bar_im = mag * jnp.sin(lim * dt)
    den = lre * lre + lim * lim
    coef_re = ((bar_re - 1.0) * lre + bar_im * lim) / den
    coef_im = (bar_im * lre - (bar_re - 1.0) * lim) / den
    bre, bim = b_re.astype(F32), b_im.astype(F32)
    bbar_re = coef_re[..., None] * bre - coef_im[..., None] * bim
    bbar_im = coef_re[..., None] * bim + coef_im[..., None] * bre
    eye = jnp.eye(S5_TILE_GROUPS, dtype=F32)

    def pack_b(m):
        m = m.reshape(tiles, S5_TILE_GROUPS, p, S5_GROUP)
        return jnp.einsum('tgpi,gh->tgihp', m, eye).reshape(tiles, S5_TILE_GROUPS * S5_GROUP,
                                                            S5_TILE_GROUPS * p)

    def pack_c(m):
        m = m.reshape(tiles, S5_TILE_GROUPS, S5_GROUP, p)
        return jnp.einsum('tgip,gh->tgphi', m, eye).reshape(tiles, S5_TILE_GROUPS * p,
                                                            S5_TILE_GROUPS * S5_GROUP)

    bblk = jnp.concatenate([pack_b(bbar_re), pack_b(bbar_im)], axis=-1).astype(BF16)
    cblk = jnp.concatenate([pack_c(c_re.astype(F32)), -pack_c(c_im.astype(F32))], axis=1).astype(BF16)
    lam_r = bar_re.reshape(tiles, 1, S5_TILE_GROUPS * p)
    lam_i = bar_im.reshape(tiles, 1, S5_TILE_GROUPS * p)
    return bblk, cblk, lam_r, lam_i


def _mem_kv_kernel(m_ref, g_ref, w_ref, kg_ref, k_ref, v_ref, *, heads):
    mn = _rms(m_ref[0], g_ref[...]).astype(BF16)
    kv = _dot(mn, w_ref[...])
    d = kv.shape[1] // 2
    hd = d // heads
    for h in range(heads):
        s = slice(h * hd, (h + 1) * hd)
        k_ref[0, :, s] = _rms(kv[:, s], kg_ref[...]).astype(k_ref.dtype)
    v_ref[0] = kv[:, d:].astype(v_ref.dtype)


def _mem_kv(mem, gain, w_kv, k_gain, heads):
    b, m, d = mem.shape
    blk = pl.BlockSpec((1, m, d), lambda bi: (bi, 0, 0))
    return pl.pallas_call(
        functools.partial(_mem_kv_kernel, heads=heads),
        grid=(b,),
        in_specs=[blk, pl.BlockSpec((1, d), lambda bi: (0, 0)),
                  pl.BlockSpec(w_kv.shape, lambda bi: (0, 0)),
                  pl.BlockSpec(k_gain.shape, lambda bi: (0, 0))],
        out_specs=[blk, blk],
        out_shape=[jax.ShapeDtypeStruct((b, m, d), BF16)] * 2,
        compiler_params=_params(("parallel",), 32),
        name="mem_kv",
    )(mem, gain, w_kv, k_gain)


def _xattn_kernel(h_ref, g_ref, wq_ref, qg_ref, k_ref, v_ref, wo_ref, *rest, heads, glu):
    *mix, o_ref = rest
    h = h_ref[0]
    if glu:
        ya_ref, yb_ref, wglu_ref, bglu_ref, woa_ref, wob_ref = mix
        yb = yb_ref[0]
        gate = jax.nn.sigmoid(_dot(yb, wglu_ref[...]) + bglu_ref[...])
        yb = (yb.astype(F32) * gate).astype(BF16)
        h = h + _dot(ya_ref[0], woa_ref[...]) + _dot(yb, wob_ref[...])
    else:
        a_ref, w_ref = mix
        h = h + _dot(a_ref[0], w_ref[...])
    hn = _rms(h, g_ref[...]).astype(BF16)
    q = _dot(hn, wq_ref[...])
    hd = q.shape[1] // heads
    k = k_ref[0]
    v = v_ref[0]
    outs = []
    for i in range(heads):
        s = slice(i * hd, (i + 1) * hd)
        qh = (_rms(q[:, s], qg_ref[...]) * hd ** -0.5).astype(BF16)
        sc = _dot_nt(qh, k[:, s])
        sc = sc - jnp.max(sc, axis=-1, keepdims=True)
        p = jnp.exp(sc)
        p = p / jnp.sum(p, axis=-1, keepdims=True)
        outs.append(_dot(p.astype(BF16), v[:, s]).astype(BF16))
    o = jnp.concatenate(outs, axis=-1)
    o_ref[0] = h + _dot(o, wo_ref[...])


def _xattn(h, gain, w_q, q_gain, k, v, w_o, mix, heads, tm):
    b, t, d = h.shape
    m = k.shape[1]
    tm = min(tm, t)
    full = lambda a: pl.BlockSpec(a.shape, lambda bi, i: (0,) * a.ndim)
    tokw = lambda n: pl.BlockSpec((1, tm, n), lambda bi, i: (bi, i, 0))
    mem = pl.BlockSpec((1, m, d), lambda bi, i: (bi, 0, 0))
    mix_specs = [tokw(a.shape[2]) if a.ndim == 3 else full(a) for a in mix]
    return pl.pallas_call(
        functools.partial(_xattn_kernel, heads=heads, glu=len(mix) == 6),
        grid=(b, t // tm),
        in_specs=[tokw(d), full(gain), full(w_q), full(q_gain), mem, mem, full(w_o)] + mix_specs,
        out_specs=tokw(d),
        out_shape=jax.ShapeDtypeStruct((b, t, d), F32),
        compiler_params=_params(("parallel", "parallel"), 48),
        name="xattn",
    )(h, gain, w_q, q_gain, k, v, w_o, *mix)


def _ffn_dense_kernel(h_ref, g_ref, wg_ref, wu_ref, wd_ref, o_ref, xn_ref, acc_ref):
    j = pl.program_id(1)

    @pl.when(j == 0)
    def _():
        xn_ref[...] = _rms(h_ref[...], g_ref[...]).astype(BF16)
        acc_ref[...] = jnp.zeros_like(acc_ref)

    xn = xn_ref[...]
    act = (jax.nn.silu(_dot(xn, wg_ref[...])) * _dot(xn, wu_ref[...])).astype(BF16)
    acc_ref[...] += _dot(act, wd_ref[...])

    @pl.when(j == pl.num_programs(1) - 1)
    def _():
        o_ref[...] = h_ref[...] + acc_ref[...]


def _ffn_dense(h, gain, w_gate, w_up, w_down, tm, tf):
    n, d = h.shape
    f = w_gate.shape[1]
    tm, tf = min(tm, n), min(tf, f)
    return pl.pallas_call(
        _ffn_dense_kernel,
        grid=(n // tm, f // tf),
        in_specs=[pl.BlockSpec((tm, d), lambda i, j: (i, 0)),
                  pl.BlockSpec((1, d), lambda i, j: (0, 0)),
                  pl.BlockSpec((d, tf), lambda i, j: (0, j)),
                  pl.BlockSpec((d, tf), lambda i, j: (0, j)),
                  pl.BlockSpec((tf, d), lambda i, j: (j, 0))],
        out_specs=pl.BlockSpec((tm, d), lambda i, j: (i, 0)),
        out_shape=jax.ShapeDtypeStruct((n, d), F32),
        scratch_shapes=[pltpu.VMEM((tm, d), BF16), pltpu.VMEM((tm, d), F32)],
        compiler_params=_params(("parallel", "arbitrary"), 48),
        name="ffn_dense",
    )(h, gain, w_gate, w_up, w_down)


def _pack_words(x):
    bits = lax.bitcast_convert_type(x.astype(BF16).astype(F32), jnp.uint32)
    half = x.shape[1] // 2
    return (bits[:, :half] >> 16) | (bits[:, half:] & jnp.uint32(0xFFFF0000))


def _unpack_words(w):
    lo = lax.bitcast_convert_type(w << 16, F32)
    hi = lax.bitcast_convert_type(w & jnp.uint32(0xFFFF0000), F32)
    return jnp.concatenate([lo, hi], axis=-1)


def _ffn_expert_kernel(te_ref, nu_ref, x_ref, wg_ref, wu_ref, wd_ref, o_ref, xb_ref, acc_ref):
    i = pl.program_id(0)
    j = pl.program_id(1)

    @pl.when(i < nu_ref[0])
    def _():
        @pl.when(j == 0)
        def _():
            xb_ref[...] = _unpack_words(x_ref[...]).astype(BF16)
            acc_ref[...] = jnp.zeros_like(acc_ref)

        x = xb_ref[...]
        act = (jax.nn.silu(_dot(x, wg_ref[0].astype(BF16))) * _dot(x, wu_ref[0].astype(BF16))).astype(BF16)
        acc_ref[...] += _dot(act, wd_ref[0].astype(BF16))

    @pl.when(j == pl.num_programs(1) - 1)
    def _():
        o_ref[...] = _pack_words(acc_ref[...])


def _ffn_expert(tile_expert, n_used, xs, w_gate, w_up, w_down, tm, tf):
    p, words = xs.shape
    d = 2 * words
    f = w_gate.shape[2]
    tf = min(tf, f)
    nj = f // tf

    def col(i, j, te, nu):
        return (te[i], 0, jnp.where(i < nu[0], j, nj - 1))

    def rowb(i, j, te, nu):
        return (te[i], jnp.where(i < nu[0], j, nj - 1), 0)

    grid_spec = pltpu.PrefetchScalarGridSpec(
        num_scalar_prefetch=2,
        grid=(p // tm, nj),
        in_specs=[pl.BlockSpec((tm, words), lambda i, j, te, nu: (i, 0)),
                  pl.BlockSpec((1, d, tf), col),
                  pl.BlockSpec((1, d, tf), col),
                  pl.BlockSpec((1, tf, d), rowb)],
        out_specs=pl.BlockSpec((tm, words), lambda i, j, te, nu: (i, 0)),
        scratch_shapes=[pltpu.VMEM((tm, d), BF16), pltpu.VMEM((tm, d), F32)],
    )
    return pl.pallas_call(
        _ffn_expert_kernel,
        grid_spec=grid_spec,
        out_shape=jax.ShapeDtypeStruct((p, words), jnp.uint32),
        compiler_params=_params(("arbitrary", "arbitrary"), 48),
        name="ffn_expert",
    )(tile_expert, n_used, xs, w_gate, w_up, w_down)


def _qkv_kernel(x_ref, g_ref, w_ref, qkg_ref, qk_ref, v_ref, *, n_qk):
    xn = _rms(x_ref[...], g_ref[...]).astype(BF16)
    z = _dot(xn, w_ref[...])
    blk = 256
    ones = _group_ones(blk, DIFF_HEAD)
    for c in range(n_qk // blk):
        s = slice(c * blk, (c + 1) * blk)
        zc = z[:, s]
        ms = _dot((zc * zc).astype(BF16), ones) * (1.0 / DIFF_HEAD)
        qk_ref[:, s] = (zc * lax.rsqrt(ms + NORM_EPS) * qkg_ref[:, s]).astype(qk_ref.dtype)
    v_ref[...] = z[:, n_qk:].astype(v_ref.dtype)


def _qkv(x, gain, w, qk_gain, tm):
    n, d = x.shape
    c = w.shape[1]
    n_qk = qk_gain.shape[1]
    tm = min(tm, n)
    return pl.pallas_call(
        functools.partial(_qkv_kernel, n_qk=n_qk),
        grid=(n // tm,),
        in_specs=[pl.BlockSpec((tm, d), lambda i: (i, 0)),
                  pl.BlockSpec((1, d), lambda i: (0, 0)),
                  pl.BlockSpec((d, c), lambda i: (0, 0)),
                  pl.BlockSpec((1, n_qk), lambda i: (0, 0))],
        out_specs=[pl.BlockSpec((tm, n_qk), lambda i: (i, 0)),
                   pl.BlockSpec((tm, c - n_qk), lambda i: (i, 0))],
        out_shape=[jax.ShapeDtypeStruct((n, n_qk), BF16),
                   jax.ShapeDtypeStruct((n, c - n_qk), BF16)],
        compiler_params=_params(("parallel",), 48),
        name="qkv",
    )(x, gain, w, qk_gain)


def _diff_attn_kernel(ti_ref, tj_ref, slope_ref, lam_ref, q_ref, kc_ref, kn_ref, v_ref, sg_ref, o_ref,
                      qs_ref, s0_ref, s1_ref, p_ref, m_ref, l_ref, acc_ref, dg_ref, *, out_scale, rb):
    h = pl.program_id(1)
    step = pl.program_id(2)
    i = ti_ref[step]
    j = tj_ref[step]
    tq = q_ref.shape[1]
    tk = kc_ref.shape[1]
    nl = tk // LANES
    slope = slope_ref[h]
    slot = lax.rem(j, 2)

    @pl.when(j == 0)
    def _():
        q = q_ref[0]
        lane = lax.broadcasted_iota(jnp.int32, q.shape, 1)
        zero = jnp.zeros_like(q)
        qs_ref[0:tq, :] = jnp.where(lane < DIFF_HEAD, q, zero)
        qs_ref[tq:2 * tq, :] = jnp.where(lane < DIFF_HEAD, zero, q)
        m_ref[...] = jnp.full_like(m_ref, NEG_INF)
        l_ref[...] = jnp.zeros_like(l_ref)
        acc_ref[...] = jnp.zeros_like(acc_ref)
        s0_ref[...] = _dot_nt(qs_ref[...], kc_ref[0])

    @pl.when(step == 0)
    def _():
        r = lax.broadcasted_iota(jnp.int32, (tq, tk), 0)
        c = lax.broadcasted_iota(jnp.int32, (tq, tk), 1)
        ahead = jnp.maximum(c - r, 0).astype(F32) * (-2.0 * slope)
        dg_ref[...] = jnp.where(c // ATTN_CHUNK <= r // ATTN_CHUNK, ahead, NEG_INF)

    col = lax.broadcasted_iota(jnp.int32, (1, tk), 1)
    key_bias = slope * (col + (j - i) * tq).astype(F32)

    def update(diag, cur_ref):
        for r0 in range(0, 2 * tq, rb):
            rows = slice(r0, r0 + rb)
            s = cur_ref[rows, :] + key_bias
            if diag:
                d0 = r0 % tq
                s = s + dg_ref[d0:d0 + rb, :]
            blk = s[:, 0:LANES]
            for c in range(1, nl):
                blk = jnp.maximum(blk, s[:, c * LANES:(c + 1) * LANES])
            m_prev = m_ref[rows, :]
            m_new = jnp.maximum(m_prev, jnp.max(blk, axis=-1, keepdims=True))
            corr = jnp.exp2(m_prev - m_new)
            p = jnp.exp2(s - jnp.concatenate([m_new] * nl, axis=1))
            part = p[:, 0:LANES]
            for c in range(1, nl):
                part = part + p[:, c * LANES:(c + 1) * LANES]
            l_ref[rows, :] = corr * l_ref[rows, :] + part
            acc_ref[rows, :] = corr * acc_ref[rows, :]
            m_ref[rows, :] = m_new
            p_ref[rows, :] = p.astype(BF16)
        v = v_ref[0]
        for half in range(2):
            rows = slice(half * tq, (half + 1) * tq)
            acc_ref[rows, :] += _dot(p_ref[rows, :], v)

    for parity, (cur_ref, nxt_ref) in enumerate(((s0_ref, s1_ref), (s1_ref, s0_ref))):
        @pl.when(jnp.logical_and(j < i, slot == parity))
        def _(cur_ref=cur_ref, nxt_ref=nxt_ref):
            nxt_ref[...] = _dot_nt(qs_ref[...], kn_ref[0])
            update(False, cur_ref)

        @pl.when(jnp.logical_and(j == i, slot == parity))
        def _(cur_ref=cur_ref):
            update(True, cur_ref)

    @pl.when(j == i)
    def _():
        o = acc_ref[...] / jnp.sum(l_ref[...], axis=-1, keepdims=True)
        o = o[:tq] - lam_ref[0] * o[tq:]
        o = o * lax.rsqrt(jnp.mean(o * o, axis=-1, keepdims=True) + NORM_EPS) * sg_ref[...]
        o_ref[0] = (o * out_scale).astype(o_ref.dtype)


def _diff_attn(slopes, lam, qk, v, sub_gain, heads, out_scale, tq, rb):
    b, t, _ = v.shape
    hw = 2 * DIFF_HEAD
    tq = min(tq, t)
    rb = min(rb, tq)
    nt = t // tq
    pairs = [(i, j) for i in range(nt) for j in range(i + 1)]
    ti = jnp.asarray([p[0] for p in pairs], jnp.int32)
    tj = jnp.asarray([p[1] for p in pairs], jnp.int32)
    smem = pl.BlockSpec(memory_space=pltpu.SMEM)
    grid_spec = pltpu.PrefetchScalarGridSpec(
        num_scalar_prefetch=2,
        grid=(b, heads, len(pairs)),
        in_specs=[smem, smem,
                  pl.BlockSpec((1, tq, hw), lambda bi, h, s, ti, tj: (bi, ti[s], h)),
                  pl.BlockSpec((1, tq, hw), lambda bi, h, s, ti, tj: (bi, tj[s], heads + h)),
                  pl.BlockSpec((1, tq, hw),
                               lambda bi, h, s, ti, tj: (bi, jnp.minimum(tj[s] + 1, ti[s]), heads + h)),
                  pl.BlockSpec((1, tq, hw), lambda bi, h, s, ti, tj: (bi, tj[s], h)),
                  pl.BlockSpec((1, hw), lambda bi, h, s, ti, tj: (0, 0))],
        out_specs=pl.BlockSpec((1, tq, hw), lambda bi, h, s, ti, tj: (bi, ti[s], h)),
        scratch_shapes=[pltpu.VMEM((2 * tq, hw), BF16),
                        pltpu.VMEM((2 * tq, tq), F32),
                        pltpu.VMEM((2 * tq, tq), F32),
                        pltpu.VMEM((2 * tq, tq), BF16),
                        pltpu.VMEM((2 * tq, LANES), F32),
                        pltpu.VMEM((2 * tq, LANES), F32),
                        pltpu.VMEM((2 * tq, hw), F32),
                        pltpu.VMEM((tq, tq), F32)],
    )
    return pl.pallas_call(
        functools.partial(_diff_attn_kernel, out_scale=out_scale, rb=rb),
        grid_spec=grid_spec,
        out_shape=jax.ShapeDtypeStruct((b, t, heads * hw), BF16),
        compiler_params=_params(("arbitrary", "arbitrary", "arbitrary"), 48),
        name="diff_attn",
    )(ti, tj, slopes, lam, qk, qk, qk, v, sub_gain)


def _router_kernel(h_ref, g_ref, w_ref, b_ref, xn_ref, r_ref):
    xn = _rms(h_ref[...], g_ref[...])
    xn_ref[...] = _pack_words(xn)
    w = w_ref[...]
    x_hi = xn.astype(BF16)
    x_lo = (xn - x_hi.astype(F32)).astype(BF16)
    w_hi = w.astype(BF16)
    w_lo = (w - w_hi.astype(F32)).astype(BF16)
    logits = _dot(x_hi, w_hi) + (_dot(x_hi, w_lo) + _dot(x_lo, w_hi)) + b_ref[...]
    lane = lax.broadcasted_iota(jnp.int32, logits.shape, 1).astype(F32)
    big = float(ROUTER_LANES)
    l1 = jnp.max(logits, axis=-1, keepdims=True)
    i1 = jnp.min(jnp.where(logits == l1, lane, big), axis=-1, keepdims=True)
    rest = jnp.where(lane == i1, NEG_INF, logits)
    l2 = jnp.max(rest, axis=-1, keepdims=True)
    i2 = jnp.min(jnp.where(rest == l2, lane, big), axis=-1, keepdims=True)
    e = jnp.exp(l2 - l1)
    w1 = 1.0 / (1.0 + e)
    w2 = e / (1.0 + e)
    r_ref[...] = jnp.where(lane == 0.0, i1, jnp.where(lane == 1.0, i2,
                           jnp.where(lane == 2.0, w1, jnp.where(lane == 3.0, w2, 0.0))))


def _router(h, gain, w_pad, b_pad, tm):
    n, d = h.shape
    tm = min(tm, n)
    return pl.pallas_call(
        _router_kernel,
        grid=(n // tm,),
        in_specs=[pl.BlockSpec((tm, d), lambda i: (i, 0)),
                  pl.BlockSpec((1, d), lambda i: (0, 0)),
                  pl.BlockSpec(w_pad.shape, lambda i: (0, 0)),
                  pl.BlockSpec(b_pad.shape, lambda i: (0, 0))],
        out_specs=[pl.BlockSpec((tm, d // 2), lambda i: (i, 0)),
                   pl.BlockSpec((tm, ROUTER_LANES), lambda i: (i, 0))],
        out_shape=[jax.ShapeDtypeStruct((n, d // 2), jnp.uint32),
                   jax.ShapeDtypeStruct((n, ROUTER_LANES), F32)],
        compiler_params=_params(("parallel",), 32),
        name="router",
    )(h, gain, w_pad, b_pad)


GATHER_UNROLL = 8


def _scatter_kernel(pos_ref, x_ref, init_hbm, o_hbm, sem):
    del init_hbm
    rows = x_ref.shape[0]
    base = pl.program_id(0) * rows

    def start(r, c):
        pltpu.make_async_copy(x_ref.at[pl.ds(r, 1)], o_hbm.at[pl.ds(pos_ref[base + r], 1)], sem).start()
        return c

    lax.fori_loop(0, rows, start, 0, unroll=GATHER_UNROLL)
    pltpu.make_async_copy(x_ref, o_hbm.at[pl.ds(0, rows)], sem).wait()


def _scatter_rows(pos, x, cap, tg):
    a = pos.shape[0]
    n, w = x.shape
    tg = min(tg, n)
    assert n % tg == 0 and a % n == 0 and tg % GATHER_UNROLL == 0
    nblk = n // tg
    grid_spec = pltpu.PrefetchScalarGridSpec(
        num_scalar_prefetch=1,
        grid=(a // tg,),
        in_specs=[pl.BlockSpec((tg, w), lambda i, pos_ref: (lax.rem(i, nblk), 0)),
                  pl.BlockSpec(memory_space=pl.ANY)],
        out_specs=pl.BlockSpec(memory_space=pl.ANY),
        scratch_shapes=[pltpu.SemaphoreType.DMA(())],
    )
    return pl.pallas_call(
        _scatter_kernel,
        grid_spec=grid_spec,
        out_shape=jax.ShapeDtypeStruct((cap, w), x.dtype),
        input_output_aliases={2: 0},
        compiler_params=_params(("arbitrary",), 32),
        name="scatter_rows",
    )(pos, x, jnp.zeros((cap, w), x.dtype))


def _combine_kernel(pos_ref, h_ref, r_ref, y_hbm, o_ref, y0_ref, y1_ref, sem, *, n):
    rows = h_ref.shape[0]
    base = pl.program_id(0) * rows

    def start(r, c):
        pltpu.make_async_copy(y_hbm.at[pl.ds(pos_ref[base + r], 1)], y0_ref.at[pl.ds(r, 1)], sem).start()
        pltpu.make_async_copy(y_hbm.at[pl.ds(pos_ref[n + base + r], 1)], y1_ref.at[pl.ds(r, 1)], sem).start()
        return c

    lax.fori_loop(0, rows, start, 0, unroll=GATHER_UNROLL)
    pltpu.make_async_copy(y_hbm.at[pl.ds(0, rows)], y0_ref, sem).wait()
    pltpu.make_async_copy(y_hbm.at[pl.ds(0, rows)], y1_ref, sem).wait()
    r = r_ref[...]
    w0 = r[:, 2:3]
    w1 = r[:, 3:4]
    o_ref[...] = h_ref[...] + w0 * _unpack_words(y0_ref[...]) + w1 * _unpack_words(y1_ref[...])


def _combine(pos, h, route, ys, tm):
    n, d = h.shape
    tm = min(tm, n)
    assert n % tm == 0 and tm % GATHER_UNROLL == 0
    grid_spec = pltpu.PrefetchScalarGridSpec(
        num_scalar_prefetch=1,
        grid=(n // tm,),
        in_specs=[pl.BlockSpec((tm, d), lambda i, pos_ref: (i, 0)),
                  pl.BlockSpec((tm, ROUTER_LANES), lambda i, pos_ref: (i, 0)),
                  pl.BlockSpec(memory_space=pl.ANY)],
        out_specs=pl.BlockSpec((tm, d), lambda i, pos_ref: (i, 0)),
        scratch_shapes=[pltpu.VMEM((tm, d // 2), jnp.uint32),
                        pltpu.VMEM((tm, d // 2), jnp.uint32),
                        pltpu.SemaphoreType.DMA(())],
    )
    return pl.pallas_call(
        functools.partial(_combine_kernel, n=n),
        grid_spec=grid_spec,
        out_shape=jax.ShapeDtypeStruct((n, d), F32),
        compiler_params=_params(("arbitrary",), 32),
        name="combine",
    )(pos, h, route, ys)


def _moe(h, gain, w_router, b_router, w_gate, w_up, w_down, tm, tf):
    n, d = h.shape
    e = w_gate.shape[0]
    w_pad = jnp.zeros((d, ROUTER_LANES), F32).at[:, :e].set(w_router.astype(F32))
    b_pad = jnp.full((1, ROUTER_LANES), NEG_INF, F32).at[0, :e].set(b_router.astype(F32))
    xn, route = _router(h, gain, w_pad, b_pad, 512)

    tm = min(tm, n)
    experts = jnp.concatenate([route[:, 0], route[:, 1]]).astype(jnp.int32)
    onehot = (experts[:, None] == jnp.arange(e, dtype=jnp.int32)[None, :]).astype(jnp.int32)
    counts = jnp.sum(onehot, axis=0)
    padded = ((counts + tm - 1) // tm) * tm
    ends = jnp.cumsum(padded)
    starts = ends - padded
    pos = jnp.sum(onehot * (jnp.cumsum(onehot, axis=0) - onehot + starts[None, :]), axis=1)
    pos = pos.astype(jnp.int32)
    cap = 2 * n + e * tm
    tile_start = jnp.arange(cap // tm, dtype=jnp.int32) * tm
    tile_expert = jnp.minimum(jnp.sum((tile_start[:, None] >= ends[None, :]).astype(jnp.int32), axis=1),
                              e - 1)
    n_used = (ends[-1:] // tm).astype(jnp.int32)

    xs = _scatter_rows(pos, xn, cap, 2048)
    ys = _ffn_expert(tile_expert, n_used, xs, w_gate, w_up, w_down, tm, tf)
    return _combine(pos, h, route, ys, 1024)


def _row(v):
    return v.astype(F32).reshape(1, -1)


def _memory_block(h, mix, mem, norm_x, norm_m, w_q, w_kv, q_gain, k_gain, w_o):
    k, v = _mem_kv(mem, _row(norm_m), w_kv.astype(BF16), _row(k_gain), MEM_HEADS)
    return _xattn(h, _row(norm_x), w_q.astype(BF16), _row(q_gain), k, v, w_o.astype(BF16),
                  mix, MEM_HEADS, 512)


def _even_mixer(h, norm, hy_w_in, rw_mu, rw_w0, rw_w_up, rw_a0, rw_a_up, rw_g_up, rw_k_k, rw_k_a,
                rw_r_k, rw_ln_w, rw_ln_b, s5_lam_re, s5_lam_im, s5_log_dt, s5_b_re, s5_b_im,
                s5_c_re, s5_c_im, s5_d, s5_w_glu, s5_b_glu, hy_w_out):
    b, t, d = h.shape
    width = rw_w0.shape[0]
    n_lora = DECAY_LORA + ICLR_LORA + GATE_LORA
    rwkv_cols = 3 * width + n_lora
    n_r = 3 * width + LORA_PAD
    pad = LORA_PAD - n_lora
    w_in = jnp.concatenate([hy_w_in[:, :rwkv_cols], jnp.zeros((d, pad), hy_w_in.dtype),
                            hy_w_in[:, rwkv_cols:]], axis=1).astype(BF16)
    mu = jnp.concatenate([rw_mu.astype(F32), jnp.zeros((pad,), F32)]).reshape(1, -1)
    w_lora = jnp.zeros((LORA_PAD, 3 * width), F32)
    w_lora = w_lora.at[:DECAY_LORA, :width].set(rw_w_up.astype(F32))
    w_lora = w_lora.at[DECAY_LORA:DECAY_LORA + ICLR_LORA, width:2 * width].set(rw_a_up.astype(F32))
    w_lora = w_lora.at[DECAY_LORA + ICLR_LORA:n_lora, 2 * width:].set(rw_g_up.astype(F32))

    *parts, u = _rwkv_prep(h, _row(norm), w_in, mu, w_lora.astype(BF16), _row(rw_w0), _row(rw_a0),
                           _row(rw_k_k), _row(rw_k_a), width, 512)
    ya = _rwkv_chunk(*parts, _row(rw_r_k), _row(rw_ln_w), _row(rw_ln_b), width // RWKV_HEAD, 2)

    bblk, cblk, lam_r, lam_i = _s5_tables(s5_lam_re, s5_lam_im, s5_log_dt, s5_b_re, s5_b_im,
                                          s5_c_re, s5_c_im)
    yb = _s5(u.reshape(b, t, -1), bblk, cblk, lam_r, lam_i, _row(s5_d))

    w_out = hy_w_out.astype(BF16)
    return (ya, yb, s5_w_glu.astype(BF16), _row(s5_b_glu), w_out[:width], w_out[width:])


def _odd_mixer(h, norm, layer, w_qkv, q_gain, k_gain, lam_q1, lam_k1, lam_q2, lam_k2, sub_gain, w_o):
    b, t, d = h.shape
    heads = w_o.shape[0] // (2 * DIFF_HEAD)
    lambda_init = 0.8 - 0.6 * math.exp(-0.3 * layer)
    lam = (jnp.exp(jnp.sum(lam_q1.astype(F32) * lam_k1.astype(F32)))
           - jnp.exp(jnp.sum(lam_q2.astype(F32) * lam_k2.astype(F32))) + lambda_init).reshape(1)
    log2e = math.log2(math.e)
    slopes = 2.0 ** (-8.0 * jnp.arange(1, heads + 1, dtype=F32) / heads) * log2e
    n_half = heads * 2 * DIFF_HEAD
    qk_gain = jnp.concatenate([jnp.tile(q_gain.astype(F32), n_half // DIFF_HEAD) * (DIFF_HEAD ** -0.5 * log2e),
                               jnp.tile(k_gain.astype(F32), n_half // DIFF_HEAD)]).reshape(1, -1)
    h2 = h.reshape(b * t, d)
    qk, v = _qkv(h2, _row(norm), w_qkv.astype(BF16), qk_gain, 512)
    o = _diff_attn(slopes, lam, qk.reshape(b, t, -1), v.reshape(b, t, -1), _row(sub_gain),
                   heads, 1.0 - lambda_init, 1024, 128)
    return (o, w_o.astype(BF16))


def kernel(x, mem, norm_mix, norm_xattn, norm_mem, norm_ffn, xa_w_q, xa_w_kv, xa_q_gain, xa_k_gain, xa_w_o, hy_w_in, rw_mu, rw_w0, rw_w_up, rw_a0, rw_a_up, rw_g_up, rw_k_k, rw_k_a, rw_r_k, rw_ln_w, rw_ln_b, s5_lam_re, s5_lam_im, s5_log_dt, s5_b_re, s5_b_im, s5_c_re, s5_c_im, s5_d, s5_w_glu, s5_b_glu, hy_w_out, ff_w_gate, ff_w_up, ff_w_down, da_w_qkv, da_q_gain, da_k_gain, da_lam_q1, da_lam_k1, da_lam_q2, da_lam_k2, da_sub_gain, da_w_o, moe_w_router, moe_b_router, moe_w_gate, moe_w_up, moe_w_down):
    depth = norm_mix.shape[0]
    b, t, d = x.shape
    h = x
    for l in range(depth):
        i = l // 2
        if l % 2 == 0:
            mix = _even_mixer(h, norm_mix[l], hy_w_in[i], rw_mu[i], rw_w0[i], rw_w_up[i], rw_a0[i],
                            rw_a_up[i], rw_g_up[i], rw_k_k[i], rw_k_a[i], rw_r_k[i], rw_ln_w[i],
                            rw_ln_b[i], s5_lam_re[i], s5_lam_im[i], s5_log_dt[i], s5_b_re[i],
                            s5_b_im[i], s5_c_re[i], s5_c_im[i], s5_d[i], s5_w_glu[i], s5_b_glu[i],
                            hy_w_out[i])
        else:
            mix = _odd_mixer(h, norm_mix[l], l, da_w_qkv[i], da_q_gain[i], da_k_gain[i], da_lam_q1[i],
                             da_lam_k1[i], da_lam_q2[i], da_lam_k2[i], da_sub_gain[i], da_w_o[i])
        h = _memory_block(h, mix, mem, norm_xattn[l], norm_mem[l], xa_w_q[l], xa_w_kv[l], xa_q_gain[l],
                          xa_k_gain[l], xa_w_o[l])
        h2 = h.reshape(b * t, d)
        if l % 2 == 0:
            h2 = _ffn_dense(h2, _row(norm_ffn[l]), ff_w_gate[i].astype(BF16), ff_w_up[i].astype(BF16),
                            ff_w_down[i].astype(BF16), 512, 1408)
        else:
            h2 = _moe(h2, _row(norm_ffn[l]), moe_w_router[i], moe_b_router[i],
                      moe_w_gate[i], moe_w_up[i], moe_w_down[i], 1024, 512)
        h = h2.reshape(b, t, d)
    return h
```

```python
import functools
import math

import jax
import jax.numpy as jnp
from jax import lax
from jax.experimental import pallas as pl
from jax.experimental.pallas import tpu as pltpu

F32 = jnp.float32
BF16 = jnp.bfloat16

NORM_EPS = 1e-6
NEG_INF = -1e30
ATTN_CHUNK = 64

RWKV_HEAD = 64
RWKV_CHUNK = 64
DECAY_LORA = 64
ICLR_LORA = 64
GATE_LORA = 160
LORA_PAD = 384
RWKV_GN_EPS = 1e-5 * RWKV_HEAD

S5_GROUP = 16
S5_STATE = 64
S5_TILE_GROUPS = 8

DIFF_HEAD = 64
MEM_HEADS = 4
N_EXPERTS = 8
LANES = 128
ROUTER_LANES = LANES

V7X_VMEM_BYTES = 64 * 1024 * 1024


def _params(semantics, vmem_mib):
    assert vmem_mib * 1024 * 1024 < V7X_VMEM_BYTES
    return pltpu.CompilerParams(dimension_semantics=semantics,
                                vmem_limit_bytes=vmem_mib * 1024 * 1024)


def _rms(x, gain):
    return x * lax.rsqrt(jnp.mean(x * x, axis=-1, keepdims=True) + NORM_EPS) * gain


def _dot(a, b):
    return jnp.dot(a, b, preferred_element_type=F32)


def _dot_nt(a, b):
    return lax.dot_general(a, b, (((1,), (1,)), ((), ())), preferred_element_type=F32)


def _dot_tn(a, b):
    return lax.dot_general(a, b, (((0,), (0,)), ((), ())), preferred_element_type=F32)


def _group_ones(width, group):
    r = lax.broadcasted_iota(jnp.int32, (width, width), 0) // group
    c = lax.broadcasted_iota(jnp.int32, (width, width), 1) // group
    return jnp.where(r == c, 1.0, 0.0).astype(BF16)


def _group_sum(x, ones):
    hi = x.astype(BF16)
    lo = (x - hi.astype(F32)).astype(BF16)
    return _dot(hi, ones) + _dot(lo, ones)


def _rwkv_prep_kernel(x_ref, gain_ref, win_ref, mu_ref, wl_ref, w0_ref, a0_ref, kk_ref, ka_ref,
                      r_ref, lw_ref, k_ref, v_ref, kn_ref, a_ref, g_ref, u_ref, carry_ref, *, width):
    i = pl.program_id(1)

    @pl.when(i == 0)
    def _():
        carry_ref[...] = jnp.zeros_like(carry_ref)

    xn = _rms(x_ref[0], gain_ref[...]).astype(BF16)
    zall = _dot(xn, win_ref[...])
    n_r = mu_ref.shape[1]
    u_ref[0] = zall[:, n_r:].astype(u_ref.dtype)
    z = zall[:, :n_r]
    rows = lax.broadcasted_iota(jnp.int32, z.shape, 0)
    prev = jnp.where(rows == 0, carry_ref[...], pltpu.roll(z, 1, axis=0))
    carry_ref[...] = z[z.shape[0] - 1:, :]
    zs = z + (prev - z) * mu_ref[...]

    w = width
    r = zs[:, :w]
    k = zs[:, w:2 * w]
    v = zs[:, 2 * w:3 * w]
    lo = zs[:, 3 * w:]
    lane = lax.broadcasted_iota(jnp.int32, lo.shape, 1)
    lo = jnp.where(lane < DECAY_LORA, jnp.tanh(lo),
                   jnp.where(lane < DECAY_LORA + ICLR_LORA, lo, jax.nn.sigmoid(lo)))
    m = _dot(lo.astype(BF16), wl_ref[...])
    w_log = -jax.nn.softplus(-(w0_ref[...] + m[:, :w])) - 0.5
    a = jax.nn.sigmoid(a0_ref[...] + m[:, w:2 * w])
    kn = k * kk_ref[...]
    ss = _group_sum(kn * kn, _group_ones(w, RWKV_HEAD))
    kn = kn * lax.rsqrt(jnp.maximum(ss, 1e-24))

    r_ref[0] = r
    lw_ref[0] = -jnp.exp(w_log)
    k_ref[0] = k * (1.0 + (a - 1.0) * ka_ref[...])
    v_ref[0] = v
    kn_ref[0] = kn
    a_ref[0] = a
    g_ref[0] = m[:, 2 * w:]


def _rwkv_prep(x, gain, w_in, mu, wl, w0, a0, k_k, k_a, width, tm):
    b, t, d = x.shape
    c = mu.shape[1]
    n_u = w_in.shape[1] - c
    tm = min(tm, t)
    row = lambda n: pl.BlockSpec((1, n), lambda bi, i: (0, 0))
    full = lambda a: pl.BlockSpec(a.shape, lambda bi, i: (0, 0))
    out = lambda n: pl.BlockSpec((1, tm, n), lambda bi, i: (bi, i, 0))
    return pl.pallas_call(
        functools.partial(_rwkv_prep_kernel, width=width),
        grid=(b, t // tm),
        in_specs=[pl.BlockSpec((1, tm, d), lambda bi, i: (bi, i, 0)),
                  row(d), full(w_in), row(c), full(wl),
                  row(width), row(width), row(width), row(width)],
        out_specs=[out(width)] * 7 + [out(n_u)],
        out_shape=[jax.ShapeDtypeStruct((b, t, width), F32)] * 7 + [jax.ShapeDtypeStruct((b, t, n_u), BF16)],
        scratch_shapes=[pltpu.VMEM((1, c), F32)],
        compiler_params=_params(("parallel", "arbitrary"), 48),
        name="rwkv_prep",
    )(x, gain, w_in, mu, wl, w0, a0, k_k, k_a)


def _rwkv_chunk_kernel(r_ref, lw_ref, k_ref, v_ref, kn_ref, a_ref, g_ref,
                       rk_ref, lnw_ref, lnb_ref, o_ref, st_ref, *, heads, chunks, L):
    @pl.when(pl.program_id(2) == 0)
    def _():
        st_ref[...] = jnp.zeros_like(st_ref)

    n = RWKV_HEAD
    ri = lax.broadcasted_iota(jnp.int32, (L, L), 0)
    ci = lax.broadcasted_iota(jnp.int32, (L, L), 1)
    tri = jnp.where(ci <= ri, 1.0, 0.0).astype(BF16)
    strict = ci < ri
    incl = ci <= ri
    eye = jnp.where(ci == ri, 1.0, 0.0).astype(F32)
    rk_all, lnw_all, lnb_all = rk_ref[...], lnw_ref[...], lnb_ref[...]
    inv_steps = max(1, int(math.ceil(math.log2(L))) - 1)

    items = [(c, h) for c in range(chunks) for h in range(heads)]
    fs = []
    for c, h in items:
        rows = slice(c * L, (c + 1) * L)
        s = slice(h * n, (h + 1) * n)
        r, k, v = r_ref[0, rows, s], k_ref[0, rows, s], v_ref[0, rows, s]
        kn, a, lw = kn_ref[0, rows, s], a_ref[0, rows, s], lw_ref[0, rows, s]
        hi = lw.astype(BF16)
        lo = (lw - hi.astype(F32)).astype(BF16)
        fs.append(dict(r=r, k=k, v=v, lw=lw, alpha=-kn, beta=kn * a, v_b=v.astype(BF16),
                       cum=_dot(tri, hi) + _dot(tri, lo)))
    for f in fs:
        cum, lw = f["cum"], f["lw"]
        cum_ex = cum - lw
        c_mid = cum[L // 2 - 1:L // 2, :]
        c_end = cum[L - 1:L, :]
        e_inv = jnp.exp(c_mid - cum)
        e_end = jnp.exp(c_end - cum)
        lhs = jnp.concatenate([f["alpha"] * jnp.exp(cum_ex - c_mid), f["r"] * jnp.exp(cum - c_mid)],
                              axis=0).astype(BF16)
        f["m_b"] = _dot_nt(lhs, (f["beta"] * e_inv).astype(BF16))
        f["m_k"] = _dot_nt(lhs, (f["k"] * e_inv).astype(BF16))
        f["abs_lhs"] = jnp.concatenate([f["alpha"] * jnp.exp(cum_ex), f["r"] * jnp.exp(cum)],
                                       axis=0).astype(BF16)
        f["b_end"] = (f["beta"] * e_end).astype(BF16)
        f["k_end"] = (f["k"] * e_end).astype(BF16)
        f["decay"] = jnp.exp(c_end)
    for f in fs:
        m_ab = jnp.where(strict, f["m_b"][:L], 0.0)
        f["m_rb"] = jnp.where(incl, f["m_b"][L:], 0.0).astype(BF16)
        f["ak_v"] = _dot(jnp.where(strict, f["m_k"][:L], 0.0).astype(BF16), f["v_b"])
        f["rk_v"] = _dot(jnp.where(incl, f["m_k"][L:], 0.0).astype(BF16), f["v_b"])
        f["inv"] = eye + m_ab
        f["p"] = m_ab
    for _ in range(inv_steps):
        for f in fs:
            pb = f["p"].astype(BF16)
            f["p"] = _dot(pb, pb)
        for f in fs:
            f["inv"] = f["inv"] + _dot(f["inv"].astype(BF16), f["p"].astype(BF16))
    for f in fs:
        f["inv"] = f["inv"].astype(BF16)

    st = [st_ref[h] for h in range(heads)]
    for c in range(chunks):
        cf = fs[c * heads:(c + 1) * heads]
        from_state = [_dot_nt(f["abs_lhs"], st[h].astype(BF16)) for h, f in enumerate(cf)]
        u_b = [_dot(f["inv"], (from_state[h][:L] + f["ak_v"]).astype(BF16)).astype(BF16)
               for h, f in enumerate(cf)]
        st = [st[h] * f["decay"] + _dot_tn(u_b[h], f["b_end"]) + _dot_tn(f["v_b"], f["k_end"])
              for h, f in enumerate(cf)]
        for h, f in enumerate(cf):
            rows = slice(c * L, (c + 1) * L)
            s = slice(h * n, (h + 1) * n)
            o = from_state[h][L:] + _dot(f["m_rb"], u_b[h]) + f["rk_v"]
            mean = jnp.mean(o, axis=-1, keepdims=True)
            d = o - mean
            var = jnp.mean(d * d, axis=-1, keepdims=True)
            o = d * lax.rsqrt(var + RWKV_GN_EPS) * lnw_all[:, s] + lnb_all[:, s]
            bonus = jnp.sum(f["r"] * f["k"] * rk_all[:, s], axis=-1, keepdims=True) * f["v"]
            o_ref[0, rows, s] = ((o + bonus) * g_ref[0, rows, s]).astype(o_ref.dtype)
    for h in range(heads):
        st_ref[h] = st[h]


def _rwkv_chunk(r, lw, k, v, kn, a, g, r_k, ln_w, ln_b, heads_per_step, chunks_per_step):
    b, t, width = r.shape
    L = min(RWKV_CHUNK, t)
    chunks = min(chunks_per_step, t // L)
    lanes = heads_per_step * RWKV_HEAD
    seq = pl.BlockSpec((1, chunks * L, lanes), lambda bi, hi, ci: (bi, ci, hi))
    row = pl.BlockSpec((1, lanes), lambda bi, hi, ci: (0, hi))
    return pl.pallas_call(
        functools.partial(_rwkv_chunk_kernel, heads=heads_per_step, chunks=chunks, L=L),
        grid=(b, width // lanes, t // (chunks * L)),
        in_specs=[seq] * 7 + [row] * 3,
        out_specs=seq,
        out_shape=jax.ShapeDtypeStruct((b, t, width), BF16),
        scratch_shapes=[pltpu.VMEM((heads_per_step, RWKV_HEAD, RWKV_HEAD), F32)],
        compiler_params=_params(("parallel", "parallel", "arbitrary"), 32),
        name="rwkv_chunk",
    )(r, lw, k, v, kn, a, g, r_k, ln_w, ln_b)


S5_ROWS = 8


def _gelu_tanh(x):
    return 0.5 * x * (1.0 + jnp.tanh(math.sqrt(2.0 / math.pi) * (x + 0.044715 * (x * x * x))))


def _s5_kernel(u_ref, bb_ref, cb_ref, lr_ref, li_ref, d_ref, y_ref, x_ref, *, unroll, y_rows):
    half = lr_ref.shape[-1]
    t = u_ref.shape[1]
    nb = half // LANES
    lr = lr_ref[0]
    li = li_ref[0]

    for cblk in range(2 * nb):
        x_ref[cblk] = _dot(u_ref[0], bb_ref[0, :, cblk * LANES:(cblk + 1) * LANES])

    pows = [(lr, li)]
    for _ in range(S5_ROWS - 1):
        qr, qi = pows[-1]
        pows.append((qr * lr - qi * li, qr * li + qi * lr))
    pw_r = jnp.concatenate([q[0] for q in pows], axis=0)
    pw_i = jnp.concatenate([q[1] for q in pows], axis=0)
    row = lax.broadcasted_iota(jnp.int32, (S5_ROWS, half), 0)
    shifts = []
    k = 1
    while k < S5_ROWS:
        shifts.append((k, jnp.where(row >= k, pows[k - 1][0], 0.0), jnp.where(row >= k, pows[k - 1][1], 0.0)))
        k *= 2

    def tile_scan(ti, carry):
        rows = pl.ds(pl.multiple_of(ti * S5_ROWS, S5_ROWS), S5_ROWS)
        out = []
        for cblk in range(nb):
            lanes = slice(cblk * LANES, (cblk + 1) * LANES)
            xr = x_ref[cblk, rows, :]
            xi = x_ref[nb + cblk, rows, :]
            for k, ar, ai in shifts:
                sr = pltpu.roll(xr, k, axis=0)
                si = pltpu.roll(xi, k, axis=0)
                xr, xi = (xr + (ar[:, lanes] * sr - ai[:, lanes] * si),
                          xi + (ar[:, lanes] * si + ai[:, lanes] * sr))
            cr, ci = carry[cblk]
            xr, xi = (xr + (pw_r[:, lanes] * cr - pw_i[:, lanes] * ci),
                      xi + (pw_r[:, lanes] * ci + pw_i[:, lanes] * cr))
            x_ref[cblk, rows, :] = xr
            x_ref[nb + cblk, rows, :] = xi
            out.append((jnp.broadcast_to(xr[S5_ROWS - 1:, :], (S5_ROWS, LANES)),
                        jnp.broadcast_to(xi[S5_ROWS - 1:, :], (S5_ROWS, LANES))))
        return tuple(out)

    zero = jnp.zeros((S5_ROWS, LANES), F32)
    lax.fori_loop(0, t // S5_ROWS, tile_scan, tuple((zero, zero) for _ in range(nb)), unroll=unroll)

    d = d_ref[...]
    cb = cb_ref[0]
    for r0 in range(0, t, y_rows):
        rows = slice(r0, r0 + y_rows)
        x = jnp.concatenate([x_ref[cblk, rows, :] for cblk in range(2 * nb)], axis=-1).astype(BF16)
        y = _dot(x, cb) + d * u_ref[0, rows, :].astype(F32)
        y_ref[0, rows, :] = _gelu_tanh(y).astype(y_ref.dtype)


def _s5(u, bblk, cblk, lam_r, lam_i, d_skip):
    b, t, width = u.shape
    lanes = S5_TILE_GROUPS * S5_GROUP
    half = S5_TILE_GROUPS * S5_STATE
    tile = lambda shape: pl.BlockSpec((1,) + shape, lambda bi, ci: (ci, 0, 0))
    return pl.pallas_call(
        functools.partial(_s5_kernel, unroll=min(4, t // S5_ROWS), y_rows=min(512, t)),
        grid=(b, width // lanes),
        in_specs=[pl.BlockSpec((1, t, lanes), lambda bi, ci: (bi, 0, ci)),
                  tile((lanes, 2 * half)), tile((2 * half, lanes)),
                  tile((1, half)), tile((1, half)),
                  pl.BlockSpec((1, lanes), lambda bi, ci: (0, ci))],
        out_specs=pl.BlockSpec((1, t, lanes), lambda bi, ci: (bi, 0, ci)),
        out_shape=jax.ShapeDtypeStruct((b, t, width), BF16),
        scratch_shapes=[pltpu.VMEM((2 * half // LANES, t, LANES), F32)],
        compiler_params=_params(("parallel", "parallel"), 48),
        name="s5",
    )(u, bblk, cblk, lam_r, lam_i, d_skip)


def _s5_tables(lam_re, lam_im, log_dt, b_re, b_im, c_re, c_im):
    g, p = lam_re.shape
    tiles = g // S5_TILE_GROUPS
    lre, lim = lam_re.astype(F32), lam_im.astype(F32)
    dt = jnp.exp(log_dt.astype(F32))[:, None]
    mag = jnp.exp(lre * dt)
    bar_re = mag * jnp.cos(lim * dt)
    bar_im = mag * jnp.sin(lim * dt)
    den = lre * lre + lim * lim
    coef_re = ((bar_re - 1.0) * lre + bar_im * lim) / den
    coef_im = (bar_im * lre - (bar_re - 1.0) * lim) / den
    bre, bim = b_re.astype(F32), b_im.astype(F32)
    bbar_re = coef_re[..., None] * bre - coef_im[..., None] * bim
    bbar_im = coef_re[..., None] * bim + coef_im[..., None] * bre
    eye = jnp.eye(S5_TILE_GROUPS, dtype=F32)

    def pack_b(m):
        m = m.reshape(tiles, S5_TILE_GROUPS, p, S5_GROUP)
        return jnp.einsum('tgpi,gh->tgihp', m, eye).reshape(tiles, S5_TILE_GROUPS * S5_GROUP,
                                                            S5_TILE_GROUPS * p)

    def pack_c(m):
        m = m.reshape(tiles, S5_TILE_GROUPS, S5_GROUP, p)
        return jnp.einsum('tgip,gh->tgphi', m, eye).reshape(tiles, S5_TILE_GROUPS * p,
                                                            S5_TILE_GROUPS * S5_GROUP)

    bblk = jnp.concatenate([pack_b(bbar_re), pack_b(bbar_im)], axis=-1).astype(BF16)
    cblk = jnp.concatenate([pack_c(c_re.astype(F32)), -pack_c(c_im.astype(F32))], axis=1).astype(BF16)
    lam_r = bar_re.reshape(tiles, 1, S5_TILE_GROUPS * p)
    lam_i = bar_im.reshape(tiles, 1, S5_TILE_GROUPS * p)
    return bblk, cblk, lam_r, lam_i


def _mem_kv_kernel(m_ref, g_ref, w_ref, kg_ref, k_ref, v_ref, *, heads):
    mn = _rms(m_ref[0], g_ref[...]).astype(BF16)
    kv = _dot(mn, w_ref[...])
    d = kv.shape[1] // 2
    hd = d // heads
    for h in range(heads):
        s = slice(h * hd, (h + 1) * hd)
        k_ref[0, :, s] = _rms(kv[:, s], kg_ref[...]).astype(k_ref.dtype)
    v_ref[0] = kv[:, d:].astype(v_ref.dtype)


def _mem_kv(mem, gain, w_kv, k_gain, heads):
    b, m, d = mem.shape
    blk = pl.BlockSpec((1, m, d), lambda bi: (bi, 0, 0))
    return pl.pallas_call(
        functools.partial(_mem_kv_kernel, heads=heads),
        grid=(b,),
        in_specs=[blk, pl.BlockSpec((1, d), lambda bi: (0, 0)),
                  pl.BlockSpec(w_kv.shape, lambda bi: (0, 0)),
                  pl.BlockSpec(k_gain.shape, lambda bi: (0, 0))],
        out_specs=[blk, blk],
        out_shape=[jax.ShapeDtypeStruct((b, m, d), BF16)] * 2,
        compiler_params=_params(("parallel",), 32),
        name="mem_kv",
    )(mem, gain, w_kv, k_gain)


def _xattn_kernel(h_ref, g_ref, wq_ref, qg_ref, k_ref, v_ref, wo_ref, *rest, heads, glu):
    *mix, o_ref = rest
    h = h_ref[0]
    if glu:
        ya_ref, yb_ref, wglu_ref, bglu_ref, woa_ref, wob_ref = mix
        yb = yb_ref[0]
        gate = jax.nn.sigmoid(_dot(yb, wglu_ref[...]) + bglu_ref[...])
        yb = (yb.astype(F32) * gate).astype(BF16)
        h = h + _dot(ya_ref[0], woa_ref[...]) + _dot(yb, wob_ref[...])
    else:
        a_ref, w_ref = mix
        h = h + _dot(a_ref[0], w_ref[...])
    hn = _rms(h, g_ref[...]).astype(BF16)
    q = _dot(hn, wq_ref[...])
    hd = q.shape[1] // heads
    k = k_ref[0]
    v = v_ref[0]
    outs = []
    for i in range(heads):
        s = slice(i * hd, (i + 1) * hd)
        qh = (_rms(q[:, s], qg_ref[...]) * hd ** -0.5).astype(BF16)
        sc = _dot_nt(qh, k[:, s])
        sc = sc - jnp.max(sc, axis=-1, keepdims=True)
        p = jnp.exp(sc)
        p = p / jnp.sum(p, axis=-1, keepdims=True)
        outs.append(_dot(p.astype(BF16), v[:, s]).astype(BF16))
    o = jnp.concatenate(outs, axis=-1)
    o_ref[0] = h + _dot(o, wo_ref[...])


def _xattn(h, gain, w_q, q_gain, k, v, w_o, mix, heads, tm):
    b, t, d = h.shape
    m = k.shape[1]
    tm = min(tm, t)
    full = lambda a: pl.BlockSpec(a.shape, lambda bi, i: (0,) * a.ndim)
    tokw = lambda n: pl.BlockSpec((1, tm, n), lambda bi, i: (bi, i, 0))
    mem = pl.BlockSpec((1, m, d), lambda bi, i: (bi, 0, 0))
    mix_specs = [tokw(a.shape[2]) if a.ndim == 3 else full(a) for a in mix]
    return pl.pallas_call(
        functools.partial(_xattn_kernel, heads=heads, glu=len(mix) == 6),
        grid=(b, t // tm),
        in_specs=[tokw(d), full(gain), full(w_q), full(q_gain), mem, mem, full(w_o)] + mix_specs,
        out_specs=tokw(d),
        out_shape=jax.ShapeDtypeStruct((b, t, d), F32),
        compiler_params=_params(("parallel", "parallel"), 48),
        name="xattn",
    )(h, gain, w_q, q_gain, k, v, w_o, *mix)


def _ffn_dense_kernel(h_ref, g_ref, wg_ref, wu_ref, wd_ref, o_ref, xn_ref, acc_ref):
    j = pl.program_id(1)

    @pl.when(j == 0)
    def _():
        xn_ref[...] = _rms(h_ref[...], g_ref[...]).astype(BF16)
        acc_ref[...] = jnp.zeros_like(acc_ref)

    xn = xn_ref[...]
    act = (jax.nn.silu(_dot(xn, wg_ref[...])) * _dot(xn, wu_ref[...])).astype(BF16)
    acc_ref[...] += _dot(act, wd_ref[...])

    @pl.when(j == pl.num_programs(1) - 1)
    def _():
        o_ref[...] = h_ref[...] + acc_ref[...]


def _ffn_dense(h, gain, w_gate, w_up, w_down, tm, tf):
    n, d = h.shape
    f = w_gate.shape[1]
    tm, tf = min(tm, n), min(tf, f)
    weight_mode = pl.Buffered(1) if tf == f else None
    return pl.pallas_call(
        _ffn_dense_kernel,
        grid=(n // tm, f // tf),
        in_specs=[pl.BlockSpec((tm, d), lambda i, j: (i, 0)),
                  pl.BlockSpec((1, d), lambda i, j: (0, 0)),
                  pl.BlockSpec((d, tf), lambda i, j: (0, j), pipeline_mode=weight_mode),
                  pl.BlockSpec((d, tf), lambda i, j: (0, j), pipeline_mode=weight_mode),
                  pl.BlockSpec((tf, d), lambda i, j: (j, 0), pipeline_mode=weight_mode)],
        out_specs=pl.BlockSpec((tm, d), lambda i, j: (i, 0)),
        out_shape=jax.ShapeDtypeStruct((n, d), F32),
        scratch_shapes=[pltpu.VMEM((tm, d), BF16), pltpu.VMEM((tm, d), F32)],
        compiler_params=_params(("parallel", "arbitrary"), 48),
        name="ffn_dense",
    )(h, gain, w_gate, w_up, w_down)


def _pack_words(x):
    bits = lax.bitcast_convert_type(x.astype(BF16).astype(F32), jnp.uint32)
    half = x.shape[1] // 2
    return (bits[:, :half] >> 16) | (bits[:, half:] & jnp.uint32(0xFFFF0000))


def _unpack_words(w):
    lo = lax.bitcast_convert_type(w << 16, F32)
    hi = lax.bitcast_convert_type(w & jnp.uint32(0xFFFF0000), F32)
    return jnp.concatenate([lo, hi], axis=-1)


def _ffn_expert_kernel(te_ref, nu_ref, x_ref, wg_ref, wu_ref, wd_ref, o_ref, xb_ref, acc_ref):
    i = pl.program_id(0)
    j = pl.program_id(1)

    @pl.when(i < nu_ref[0])
    def _():
        @pl.when(j == 0)
        def _():
            xb_ref[...] = _unpack_words(x_ref[...]).astype(BF16)
            acc_ref[...] = jnp.zeros_like(acc_ref)

        x = xb_ref[...]
        act = (jax.nn.silu(_dot(x, wg_ref[0].astype(BF16))) * _dot(x, wu_ref[0].astype(BF16))).astype(BF16)
        acc_ref[...] += _dot(act, wd_ref[0].astype(BF16))

    @pl.when(j == pl.num_programs(1) - 1)
    def _():
        o_ref[...] = _pack_words(acc_ref[...])


def _ffn_expert(tile_expert, n_used, xs, w_gate, w_up, w_down, tm, tf):
    p, words = xs.shape
    d = 2 * words
    f = w_gate.shape[2]
    tf = min(tf, f)
    nj = f // tf

    def col(i, j, te, nu):
        return (te[i], 0, jnp.where(i < nu[0], j, nj - 1))

    def rowb(i, j, te, nu):
        return (te[i], jnp.where(i < nu[0], j, nj - 1), 0)

    grid_spec = pltpu.PrefetchScalarGridSpec(
        num_scalar_prefetch=2,
        grid=(p // tm, nj),
        in_specs=[pl.BlockSpec((tm, words), lambda i, j, te, nu: (i, 0)),
                  pl.BlockSpec((1, d, tf), col),
                  pl.BlockSpec((1, d, tf), col),
                  pl.BlockSpec((1, tf, d), rowb)],
        out_specs=pl.BlockSpec((tm, words), lambda i, j, te, nu: (i, 0)),
        scratch_shapes=[pltpu.VMEM((tm, d), BF16), pltpu.VMEM((tm, d), F32)],
    )
    return pl.pallas_call(
        _ffn_expert_kernel,
        grid_spec=grid_spec,
        out_shape=jax.ShapeDtypeStruct((p, words), jnp.uint32),
        compiler_params=_params(("arbitrary", "arbitrary"), 48),
        name="ffn_expert",
    )(tile_expert, n_used, xs, w_gate, w_up, w_down)


def _qkv_kernel(x_ref, g_ref, w_ref, qkg_ref, qk_ref, v_ref, *, n_qk):
    xn = _rms(x_ref[...], g_ref[...]).astype(BF16)
    z = _dot(xn, w_ref[...])
    blk = 256
    ones = _group_ones(blk, DIFF_HEAD)
    for c in range(n_qk // blk):
        s = slice(c * blk, (c + 1) * blk)
        zc = z[:, s]
        ms = _dot((zc * zc).astype(BF16), ones) * (1.0 / DIFF_HEAD)
        qk_ref[:, s] = (zc * lax.rsqrt(ms + NORM_EPS) * qkg_ref[:, s]).astype(qk_ref.dtype)
    v_ref[...] = z[:, n_qk:].astype(v_ref.dtype)


def _qkv(x, gain, w, qk_gain, tm):
    n, d = x.shape
    c = w.shape[1]
    n_qk = qk_gain.shape[1]
    tm = min(tm, n)
    return pl.pallas_call(
        functools.partial(_qkv_kernel, n_qk=n_qk),
        grid=(n // tm,),
        in_specs=[pl.BlockSpec((tm, d), lambda i: (i, 0)),
                  pl.BlockSpec((1, d), lambda i: (0, 0)),
                  pl.BlockSpec((d, c), lambda i: (0, 0)),
                  pl.BlockSpec((1, n_qk), lambda i: (0, 0))],
        out_specs=[pl.BlockSpec((tm, n_qk), lambda i: (i, 0)),
                   pl.BlockSpec((tm, c - n_qk), lambda i: (i, 0))],
        out_shape=[jax.ShapeDtypeStruct((n, n_qk), BF16),
                   jax.ShapeDtypeStruct((n, c - n_qk), BF16)],
        compiler_params=_params(("parallel",), 48),
        name="qkv",
    )(x, gain, w, qk_gain)


def _diff_attn_kernel(ti_ref, tj_ref, slope_ref, lam_ref, q_ref, kc_ref, kn_ref, v_ref, sg_ref, o_ref,
                      qs_ref, s0_ref, s1_ref, p_ref, m_ref, l_ref, acc_ref, dg_ref, *, out_scale, rb):
    h = pl.program_id(1)
    step = pl.program_id(2)
    i = ti_ref[step]
    j = tj_ref[step]
    tq = q_ref.shape[1]
    tk = kc_ref.shape[1]
    nl = tk // LANES
    slope = slope_ref[h]
    slot = lax.rem(j, 2)

    @pl.when(j == 0)
    def _():
        q = q_ref[0]
        lane = lax.broadcasted_iota(jnp.int32, q.shape, 1)
        zero = jnp.zeros_like(q)
        qs_ref[0:tq, :] = jnp.where(lane < DIFF_HEAD, q, zero)
        qs_ref[tq:2 * tq, :] = jnp.where(lane < DIFF_HEAD, zero, q)
        m_ref[...] = jnp.full_like(m_ref, NEG_INF)
        l_ref[...] = jnp.zeros_like(l_ref)
        acc_ref[...] = jnp.zeros_like(acc_ref)
        s0_ref[...] = _dot_nt(qs_ref[...], kc_ref[0])

    @pl.when(step == 0)
    def _():
        r = lax.broadcasted_iota(jnp.int32, (tq, tk), 0)
        c = lax.broadcasted_iota(jnp.int32, (tq, tk), 1)
        ahead = jnp.maximum(c - r, 0).astype(F32) * (-2.0 * slope)
        dg_ref[...] = jnp.where(c // ATTN_CHUNK <= r // ATTN_CHUNK, ahead, NEG_INF)

    col = lax.broadcasted_iota(jnp.int32, (1, tk), 1)
    key_bias = slope * (col + (j - i) * tq).astype(F32)

    def update(diag, cur_ref):
        for r0 in range(0, 2 * tq, rb):
            rows = slice(r0, r0 + rb)
            s = cur_ref[rows, :] + key_bias
            if diag:
                d0 = r0 % tq
                s = s + dg_ref[d0:d0 + rb, :]
            blk = s[:, 0:LANES]
            for c in range(1, nl):
                blk = jnp.maximum(blk, s[:, c * LANES:(c + 1) * LANES])
            m_prev = m_ref[rows, :]
            m_new = jnp.maximum(m_prev, jnp.max(blk, axis=-1, keepdims=True))
            corr = jnp.exp2(m_prev - m_new)
            p = jnp.exp2(s - jnp.concatenate([m_new] * nl, axis=1))
            part = p[:, 0:LANES]
            for c in range(1, nl):
                part = part + p[:, c * LANES:(c + 1) * LANES]
            l_ref[rows, :] = corr * l_ref[rows, :] + part
            acc_ref[rows, :] = corr * acc_ref[rows, :]
            m_ref[rows, :] = m_new
            p_ref[rows, :] = p.astype(BF16)
        v = v_ref[0]
        for half in range(2):
            rows = slice(half * tq, (half + 1) * tq)
            acc_ref[rows, :] += _dot(p_ref[rows, :], v)

    for parity, (cur_ref, nxt_ref) in enumerate(((s0_ref, s1_ref), (s1_ref, s0_ref))):
        @pl.when(jnp.logical_and(j < i, slot == parity))
        def _(cur_ref=cur_ref, nxt_ref=nxt_ref):
            nxt_ref[...] = _dot_nt(qs_ref[...], kn_ref[0])
            update(False, cur_ref)

        @pl.when(jnp.logical_and(j == i, slot == parity))
        def _(cur_ref=cur_ref):
            update(True, cur_ref)

    @pl.when(j == i)
    def _():
        o = acc_ref[...] / jnp.sum(l_ref[...], axis=-1, keepdims=True)
        o = o[:tq] - lam_ref[0] * o[tq:]
        o = o * lax.rsqrt(jnp.mean(o * o, axis=-1, keepdims=True) + NORM_EPS) * sg_ref[...]
        o_ref[0] = (o * out_scale).astype(o_ref.dtype)


def _diff_attn(slopes, lam, qk, v, sub_gain, heads, out_scale, tq, rb):
    b, t, _ = v.shape
    hw = 2 * DIFF_HEAD
    tq = min(tq, t)
    rb = min(rb, tq)
    nt = t // tq
    pairs = [(i, j) for i in range(nt) for j in range(i + 1)]
    ti = jnp.asarray([p[0] for p in pairs], jnp.int32)
    tj = jnp.asarray([p[1] for p in pairs], jnp.int32)
    smem = pl.BlockSpec(memory_space=pltpu.SMEM)
    grid_spec = pltpu.PrefetchScalarGridSpec(
        num_scalar_prefetch=2,
        grid=(b, heads, len(pairs)),
        in_specs=[smem, smem,
                  pl.BlockSpec((1, tq, hw), lambda bi, h, s, ti, tj: (bi, ti[s], h)),
                  pl.BlockSpec((1, tq, hw), lambda bi, h, s, ti, tj: (bi, tj[s], heads + h)),
                  pl.BlockSpec((1, tq, hw),
                               lambda bi, h, s, ti, tj: (bi, jnp.minimum(tj[s] + 1, ti[s]), heads + h)),
                  pl.BlockSpec((1, tq, hw), lambda bi, h, s, ti, tj: (bi, tj[s], h)),
                  pl.BlockSpec((1, hw), lambda bi, h, s, ti, tj: (0, 0))],
        out_specs=pl.BlockSpec((1, tq, hw), lambda bi, h, s, ti, tj: (bi, ti[s], h)),
        scratch_shapes=[pltpu.VMEM((2 * tq, hw), BF16),
                        pltpu.VMEM((2 * tq, tq), F32),
                        pltpu.VMEM((2 * tq, tq), F32),
                        pltpu.VMEM((2 * tq, tq), BF16),
                        pltpu.VMEM((2 * tq, LANES), F32),
                        pltpu.VMEM((2 * tq, LANES), F32),
                        pltpu.VMEM((2 * tq, hw), F32),
                        pltpu.VMEM((tq, tq), F32)],
    )
    return pl.pallas_call(
        functools.partial(_diff_attn_kernel, out_scale=out_scale, rb=rb),
        grid_spec=grid_spec,
        out_shape=jax.ShapeDtypeStruct((b, t, heads * hw), BF16),
        compiler_params=_params(("arbitrary", "arbitrary", "arbitrary"), 48),
        name="diff_attn",
    )(ti, tj, slopes, lam, qk, qk, qk, v, sub_gain)


def _router_kernel(h_ref, g_ref, w_ref, b_ref, xn_ref, r_ref):
    xn = _rms(h_ref[...], g_ref[...])
    xn_ref[...] = _pack_words(xn)
    w = w_ref[...]
    x_hi = xn.astype(BF16)
    x_lo = (xn - x_hi.astype(F32)).astype(BF16)
    w_hi = w.astype(BF16)
    w_lo = (w - w_hi.astype(F32)).astype(BF16)
    logits = _dot(x_hi, w_hi) + (_dot(x_hi, w_lo) + _dot(x_lo, w_hi)) + b_ref[...]
    lane = lax.broadcasted_iota(jnp.int32, logits.shape, 1).astype(F32)
    big = float(ROUTER_LANES)
    l1 = jnp.max(logits, axis=-1, keepdims=True)
    i1 = jnp.min(jnp.where(logits == l1, lane, big), axis=-1, keepdims=True)
    rest = jnp.where(lane == i1, NEG_INF, logits)
    l2 = jnp.max(rest, axis=-1, keepdims=True)
    i2 = jnp.min(jnp.where(rest == l2, lane, big), axis=-1, keepdims=True)
    e = jnp.exp(l2 - l1)
    w1 = 1.0 / (1.0 + e)
    w2 = e / (1.0 + e)
    r_ref[...] = jnp.where(lane == 0.0, i1, jnp.where(lane == 1.0, i2,
                           jnp.where(lane == 2.0, w1, jnp.where(lane == 3.0, w2, 0.0))))


def _router(h, gain, w_pad, b_pad, tm):
    n, d = h.shape
    tm = min(tm, n)
    return pl.pallas_call(
        _router_kernel,
        grid=(n // tm,),
        in_specs=[pl.BlockSpec((tm, d), lambda i: (i, 0)),
                  pl.BlockSpec((1, d), lambda i: (0, 0)),
                  pl.BlockSpec(w_pad.shape, lambda i: (0, 0)),
                  pl.BlockSpec(b_pad.shape, lambda i: (0, 0))],
        out_specs=[pl.BlockSpec((tm, d // 2), lambda i: (i, 0)),
                   pl.BlockSpec((tm, ROUTER_LANES), lambda i: (i, 0))],
        out_shape=[jax.ShapeDtypeStruct((n, d // 2), jnp.uint32),
                   jax.ShapeDtypeStruct((n, ROUTER_LANES), F32)],
        compiler_params=_params(("parallel",), 32),
        name="router",
    )(h, gain, w_pad, b_pad)


GATHER_UNROLL = 8


def _scatter_kernel(pos_ref, x_ref, init_hbm, o_hbm, sem):
    del init_hbm
    rows = x_ref.shape[0]
    base = pl.program_id(0) * rows

    def start(r, c):
        pltpu.make_async_copy(x_ref.at[pl.ds(r, 1)], o_hbm.at[pl.ds(pos_ref[base + r], 1)], sem).start()
        return c

    lax.fori_loop(0, rows, start, 0, unroll=GATHER_UNROLL)
    pltpu.make_async_copy(x_ref, o_hbm.at[pl.ds(0, rows)], sem).wait()


def _scatter_rows(pos, x, cap, tg):
    a = pos.shape[0]
    n, w = x.shape
    tg = min(tg, n)
    assert n % tg == 0 and a % n == 0 and tg % GATHER_UNROLL == 0
    nblk = n // tg
    grid_spec = pltpu.PrefetchScalarGridSpec(
        num_scalar_prefetch=1,
        grid=(a // tg,),
        in_specs=[pl.BlockSpec((tg, w), lambda i, pos_ref: (lax.rem(i, nblk), 0)),
                  pl.BlockSpec(memory_space=pl.ANY)],
        out_specs=pl.BlockSpec(memory_space=pl.ANY),
        scratch_shapes=[pltpu.SemaphoreType.DMA(())],
    )
    return pl.pallas_call(
        _scatter_kernel,
        grid_spec=grid_spec,
        out_shape=jax.ShapeDtypeStruct((cap, w), x.dtype),
        input_output_aliases={2: 0},
        compiler_params=_params(("arbitrary",), 32),
        name="scatter_rows",
    )(pos, x, jnp.zeros((cap, w), x.dtype))


def _combine_kernel(pos_ref, h_ref, r_ref, y_hbm, o_ref, y0_ref, y1_ref, sem, *, n):
    rows = h_ref.shape[0]
    base = pl.program_id(0) * rows

    def start(r, c):
        pltpu.make_async_copy(y_hbm.at[pl.ds(pos_ref[base + r], 1)], y0_ref.at[pl.ds(r, 1)], sem).start()
        pltpu.make_async_copy(y_hbm.at[pl.ds(pos_ref[n + base + r], 1)], y1_ref.at[pl.ds(r, 1)], sem).start()
        return c

    lax.fori_loop(0, rows, start, 0, unroll=GATHER_UNROLL)
    pltpu.make_async_copy(y_hbm.at[pl.ds(0, rows)], y0_ref, sem).wait()
    pltpu.make_async_copy(y_hbm.at[pl.ds(0, rows)], y1_ref, sem).wait()
    r = r_ref[...]
    w0 = r[:, 2:3]
    w1 = r[:, 3:4]
    o_ref[...] = h_ref[...] + w0 * _unpack_words(y0_ref[...]) + w1 * _unpack_words(y1_ref[...])


def _combine(pos, h, route, ys, tm):
    n, d = h.shape
    tm = min(tm, n)
    assert n % tm == 0 and tm % GATHER_UNROLL == 0
    grid_spec = pltpu.PrefetchScalarGridSpec(
        num_scalar_prefetch=1,
        grid=(n // tm,),
        in_specs=[pl.BlockSpec((tm, d), lambda i, pos_ref: (i, 0)),
                  pl.BlockSpec((tm, ROUTER_LANES), lambda i, pos_ref: (i, 0)),
                  pl.BlockSpec(memory_space=pl.ANY)],
        out_specs=pl.BlockSpec((tm, d), lambda i, pos_ref: (i, 0)),
        scratch_shapes=[pltpu.VMEM((tm, d // 2), jnp.uint32),
                        pltpu.VMEM((tm, d // 2), jnp.uint32),
                        pltpu.SemaphoreType.DMA(())],
    )
    return pl.pallas_call(
        functools.partial(_combine_kernel, n=n),
        grid_spec=grid_spec,
        out_shape=jax.ShapeDtypeStruct((n, d), F32),
        compiler_params=_params(("arbitrary",), 32),
        name="combine",
    )(pos, h, route, ys)


def _moe(h, gain, w_router, b_router, w_gate, w_up, w_down, tm, tf):
    n, d = h.shape
    e = w_gate.shape[0]
    w_pad = jnp.zeros((d, ROUTER_LANES), F32).at[:, :e].set(w_router.astype(F32))
    b_pad = jnp.full((1, ROUTER_LANES), NEG_INF, F32).at[0, :e].set(b_router.astype(F32))
    xn, route = _router(h, gain, w_pad, b_pad, 512)

    tm = min(tm, n)
    experts = jnp.concatenate([route[:, 0], route[:, 1]]).astype(jnp.int32)
    onehot = (experts[:, None] == jnp.arange(e, dtype=jnp.int32)[None, :]).astype(jnp.int32)
    counts = jnp.sum(onehot, axis=0)
    padded = ((counts + tm - 1) // tm) * tm
    ends = jnp.cumsum(padded)
    starts = ends - padded
    pos = jnp.sum(onehot * (jnp.cumsum(onehot, axis=0) - onehot + starts[None, :]), axis=1)
    pos = pos.astype(jnp.int32)
    cap = 2 * n + e * tm
    tile_start = jnp.arange(cap // tm, dtype=jnp.int32) * tm
    tile_expert = jnp.minimum(jnp.sum((tile_start[:, None] >= ends[None, :]).astype(jnp.int32), axis=1),
                              e - 1)
    n_used = (ends[-1:] // tm).astype(jnp.int32)

    xs = _scatter_rows(pos, xn, cap, 2048)
    ys = _ffn_expert(tile_expert, n_used, xs, w_gate, w_up, w_down, tm, tf)
    return _combine(pos, h, route, ys, 1024)


def _row(v):
    return v.astype(F32).reshape(1, -1)


def _memory_block(h, mix, mem, norm_x, norm_m, w_q, w_kv, q_gain, k_gain, w_o):
    k, v = _mem_kv(mem, _row(norm_m), w_kv.astype(BF16), _row(k_gain), MEM_HEADS)
    return _xattn(h, _row(norm_x), w_q.astype(BF16), _row(q_gain), k, v, w_o.astype(BF16),
                  mix, MEM_HEADS, 512)


def _even_mixer(h, norm, hy_w_in, rw_mu, rw_w0, rw_w_up, rw_a0, rw_a_up, rw_g_up, rw_k_k, rw_k_a,
                rw_r_k, rw_ln_w, rw_ln_b, s5_lam_re, s5_lam_im, s5_log_dt, s5_b_re, s5_b_im,
                s5_c_re, s5_c_im, s5_d, s5_w_glu, s5_b_glu, hy_w_out):
    b, t, d = h.shape
    width = rw_w0.shape[0]
    n_lora = DECAY_LORA + ICLR_LORA + GATE_LORA
    rwkv_cols = 3 * width + n_lora
    n_r = 3 * width + LORA_PAD
    pad = LORA_PAD - n_lora
    w_in = jnp.concatenate([hy_w_in[:, :rwkv_cols], jnp.zeros((d, pad), hy_w_in.dtype),
                            hy_w_in[:, rwkv_cols:]], axis=1).astype(BF16)
    mu = jnp.concatenate([rw_mu.astype(F32), jnp.zeros((pad,), F32)]).reshape(1, -1)
    w_lora = jnp.zeros((LORA_PAD, 3 * width), F32)
    w_lora = w_lora.at[:DECAY_LORA, :width].set(rw_w_up.astype(F32))
    w_lora = w_lora.at[DECAY_LORA:DECAY_LORA + ICLR_LORA, width:2 * width].set(rw_a_up.astype(F32))
    w_lora = w_lora.at[DECAY_LORA + ICLR_LORA:n_lora, 2 * width:].set(rw_g_up.astype(F32))

    *parts, u = _rwkv_prep(h, _row(norm), w_in, mu, w_lora.astype(BF16), _row(rw_w0), _row(rw_a0),
                           _row(rw_k_k), _row(rw_k_a), width, 512)
    ya = _rwkv_chunk(*parts, _row(rw_r_k), _row(rw_ln_w), _row(rw_ln_b), width // RWKV_HEAD, 2)

    bblk, cblk, lam_r, lam_i = _s5_tables(s5_lam_re, s5_lam_im, s5_log_dt, s5_b_re, s5_b_im,
                                          s5_c_re, s5_c_im)
    yb = _s5(u.reshape(b, t, -1), bblk, cblk, lam_r, lam_i, _row(s5_d))

    w_out = hy_w_out.astype(BF16)
    return (ya, yb, s5_w_glu.astype(BF16), _row(s5_b_glu), w_out[:width], w_out[width:])


def _odd_mixer(h, norm, layer, w_qkv, q_gain, k_gain, lam_q1, lam_k1, lam_q2, lam_k2, sub_gain, w_o):
    b, t, d = h.shape
    heads = w_o.shape[0] // (2 * DIFF_HEAD)
    lambda_init = 0.8 - 0.6 * math.exp(-0.3 * layer)
    lam = (jnp.exp(jnp.sum(lam_q1.astype(F32) * lam_k1.astype(F32)))
           - jnp.exp(jnp.sum(lam_q2.astype(F32) * lam_k2.astype(F32))) + lambda_init).reshape(1)
    log2e = math.log2(math.e)
    slopes = 2.0 ** (-8.0 * jnp.arange(1, heads + 1, dtype=F32) / heads) * log2e
    n_half = heads * 2 * DIFF_HEAD
    qk_gain = jnp.concatenate([jnp.tile(q_gain.astype(F32), n_half // DIFF_HEAD) * (DIFF_HEAD ** -0.5 * log2e),
                               jnp.tile(k_gain.astype(F32), n_half // DIFF_HEAD)]).reshape(1, -1)
    h2 = h.reshape(b * t, d)
    qk, v = _qkv(h2, _row(norm), w_qkv.astype(BF16), qk_gain, 512)
    o = _diff_attn(slopes, lam, qk.reshape(b, t, -1), v.reshape(b, t, -1), _row(sub_gain),
                   heads, 1.0 - lambda_init, 1024, 128)
    return (o, w_o.astype(BF16))


def kernel(x, mem, norm_mix, norm_xattn, norm_mem, norm_ffn, xa_w_q, xa_w_kv, xa_q_gain, xa_k_gain, xa_w_o, hy_w_in, rw_mu, rw_w0, rw_w_up, rw_a0, rw_a_up, rw_g_up, rw_k_k, rw_k_a, rw_r_k, rw_ln_w, rw_ln_b, s5_lam_re, s5_lam_im, s5_log_dt, s5_b_re, s5_b_im, s5_c_re, s5_c_im, s5_d, s5_w_glu, s5_b_glu, hy_w_out, ff_w_gate, ff_w_up, ff_w_down, da_w_qkv, da_q_gain, da_k_gain, da_lam_q1, da_lam_k1, da_lam_q2, da_lam_k2, da_sub_gain, da_w_o, moe_w_router, moe_b_router, moe_w_gate, moe_w_up, moe_w_down):
    depth = norm_mix.shape[0]
    b, t, d = x.shape
    h = x
    for l in range(depth):
        i = l // 2
        if l % 2 == 0:
            mix = _even_mixer(h, norm_mix[l], hy_w_in[i], rw_mu[i], rw_w0[i], rw_w_up[i], rw_a0[i],
                            rw_a_up[i], rw_g_up[i], rw_k_k[i], rw_k_a[i], rw_r_k[i], rw_ln_w[i],
                            rw_ln_b[i], s5_lam_re[i], s5_lam_im[i], s5_log_dt[i], s5_b_re[i],
                            s5_b_im[i], s5_c_re[i], s5_c_im[i], s5_d[i], s5_w_glu[i], s5_b_glu[i],
                            hy_w_out[i])
        else:
            mix = _odd_mixer(h, norm_mix[l], l, da_w_qkv[i], da_q_gain[i], da_k_gain[i], da_lam_q1[i],
                             da_lam_k1[i], da_lam_q2[i], da_lam_k2[i], da_sub_gain[i], da_w_o[i])
        h = _memory_block(h, mix, mem, norm_xattn[l], norm_mem[l], xa_w_q[l], xa_w_kv[l], xa_q_gain[l],
                          xa_k_gain[l], xa_w_o[l])
        h2 = h.reshape(b * t, d)
        if l % 2 == 0:
            h2 = _ffn_dense(h2, _row(norm_ffn[l]), ff_w_gate[i].astype(BF16), ff_w_up[i].astype(BF16),
                            ff_w_down[i].astype(BF16), 512, ff_w_gate.shape[2])
        else:
            h2 = _moe(h2, _row(norm_ffn[l]), moe_w_router[i], moe_b_router[i],
                      moe_w_gate[i], moe_w_up[i], moe_w_down[i], 1024, 512)
        h = h2.reshape(b, t, d)
    return h
```
